```python
import math
import jax, jax.numpy as jnp
from jax import lax
import numpy as np

D_MODEL = 2048
BATCH = 1
SEQ = 8192
DEPTH = 2

D_FF = 4 * D_MODEL
MIX_WIDTH = D_MODEL
GROUP_WIDTH = MIX_WIDTH // 2
CHUNK = 64
Q_BLOCK = 128
ROPE_THETA = 10000.0
EPS = 1e-6

HG_HEADS = 8
HG_DK = 128
HG_DV = GROUP_WIDTH // HG_HEADS
MLA_HEADS = 8
MLA_NOPE = 128
MLA_ROPE = 64
MLA_V = GROUP_WIDTH // MLA_HEADS
MLA_Q_RANK = 512
MLA_KV_RANK = 256
RET_HEADS = 4
RET_DK = 128
RET_DV = GROUP_WIDTH // RET_HEADS
GDN_HEADS = 8
GDN_DK = 128
GDN_DV = GROUP_WIDTH // GDN_HEADS
CONV_WIDTH = 5
GDN_QKV = GDN_HEADS * (2 * GDN_DK + GDN_DV)

N_EVEN = (DEPTH + 1) // 2
N_ODD = DEPTH // 2

EVEN_COLS = (HG_HEADS * HG_DK, HG_HEADS * HG_DK, HG_HEADS * HG_DK, HG_HEADS * HG_DV,
             HG_HEADS * HG_DV, MLA_Q_RANK, MLA_KV_RANK + MLA_ROPE)
ODD_COLS = (RET_HEADS * RET_DK, RET_HEADS * RET_DK, RET_HEADS * RET_DV, RET_HEADS * RET_DV,
            GDN_QKV, GDN_HEADS, GDN_HEADS, GDN_HEADS, GDN_HEADS, GDN_HEADS * GDN_DV)

kernel_name = 'hybrid_hgrn2_mla_retention_gdn_encoder'


def split_cols(h, sizes):
    offsets = []
    total = 0
    for s in sizes[:-1]:
        total += s
        offsets.append(total)
    return jnp.split(h, offsets, axis=-1)


def rmsnorm(x, w):
    xf = x.astype(jnp.float32)
    y = xf * lax.rsqrt(jnp.mean(xf * xf, axis=-1, keepdims=True) + EPS)
    return (y * w.astype(jnp.float32)).astype(x.dtype)


def head_rmsnorm(o, w):
    B, S, H, d = o.shape
    y = o * lax.rsqrt(jnp.mean(o * o, axis=-1, keepdims=True) + EPS)
    return y.reshape(B, S, H * d) * w.astype(jnp.float32)


def rope(x, positions):
    half = x.shape[-1] // 2
    inv = ROPE_THETA ** (-jnp.arange(half, dtype=jnp.float32) / half)
    ang = positions.astype(jnp.float32)[..., None] * inv
    cos = jnp.cos(ang)[:, :, None, :]
    sin = jnp.sin(ang)[:, :, None, :]
    x1 = x[..., :half].astype(jnp.float32)
    x2 = x[..., half:].astype(jnp.float32)
    return jnp.concatenate([x1 * cos - x2 * sin, x1 * sin + x2 * cos], axis=-1).astype(x.dtype)


def to_bhtd(a):
    return a.astype(jnp.float32).transpose(0, 2, 1, 3)


def flip_t(a):
    return jnp.flip(a, axis=2)


def chunk_gla(q, k, v, g):
    B, H, T, dk = q.shape
    dv = v.shape[-1]
    n = T // CHUNK
    q, k, v, g = (a.reshape(B, H, n, CHUNK, a.shape[-1]) for a in (q, k, v, g))
    b = jnp.cumsum(g, axis=-2)
    b_last = b[..., -1:, :]
    b_mid = b[..., CHUNK // 2:CHUNK // 2 + 1, :]
    causal = jnp.tril(jnp.ones((CHUNK, CHUNK), dtype=bool))
    scores = jnp.einsum('bhntd,bhnsd->bhnts', q * jnp.exp(b - b_mid), k * jnp.exp(b_mid - b))
    o_intra = jnp.einsum('bhnts,bhnsv->bhntv', jnp.where(causal, scores, 0.0), v)
    q_dec = q * jnp.exp(b)
    k_dec = k * jnp.exp(b_last - b)
    chunk_decay = jnp.exp(b_last[..., 0, :])

    def step(state, xs):
        qd, kd, vc, dec = xs
        o = jnp.einsum('bhcd,bhdv->bhcv', qd, state)
        state = state * dec[..., None] + jnp.einsum('bhcd,bhcv->bhdv', kd, vc)
        return state, o

    xs = tuple(jnp.moveaxis(a, 2, 0) for a in (q_dec, k_dec, v, chunk_decay))
    _, o_inter = lax.scan(step, jnp.zeros((B, H, dk, dv), jnp.float32), xs)
    o = jnp.moveaxis(o_inter, 0, 2) + o_intra
    return o.reshape(B, H, T, dv)


def chunk_gated_delta(q, k, v, log_alpha, beta):
    B, H, T, dk = q.shape
    dv = v.shape[-1]
    n = T // CHUNK
    q, k, v = (a.reshape(B, H, n, CHUNK, a.shape[-1]) for a in (q, k, v))
    log_alpha = log_alpha.reshape(B, H, n, CHUNK)
    beta = beta.reshape(B, H, n, CHUNK)
    g = jnp.cumsum(log_alpha, axis=-1)
    incl = jnp.tril(jnp.ones((CHUNK, CHUNK), dtype=bool))
    strict = jnp.tril(jnp.ones((CHUNK, CHUNK), dtype=bool), -1)
    decay = jnp.exp(jnp.where(incl, g[..., :, None] - g[..., None, :], -jnp.inf))
    k_beta = k * beta[..., None]
    a_mat = jnp.where(strict, jnp.einsum('bhntd,bhnsd->bhnts', k_beta, k) * decay, 0.0)
    eye = jnp.eye(CHUNK, dtype=q.dtype)
    rhs = jnp.concatenate([v * beta[..., None], k_beta * jnp.exp(g)[..., None]], axis=-1)
    sol = lax.linalg.triangular_solve(a_mat + eye, rhs, left_side=True, lower=True,
                                      unit_diagonal=True)
    u, w = sol[..., :dv], sol[..., dv:]
    qk = jnp.einsum('bhntd,bhnsd->bhnts', q, k) * decay
    g_last = g[..., -1:]
    q_dec = q * jnp.exp(g)[..., None]
    k_dec = k * jnp.exp(g_last - g)[..., None]
    chunk_decay = jnp.exp(g_last[..., 0])

    def step(state, xs):
        qd, kd, qk_c, u_c, w_c, dec = xs
        v_new = u_c - jnp.einsum('bhcd,bhdv->bhcv', w_c, state)
        o = jnp.einsum('bhcd,bhdv->bhcv', qd, state) + jnp.einsum('bhts,bhsv->bhtv', qk_c, v_new)
        state = state * dec[..., None, None] + jnp.einsum('bhcd,bhcv->bhdv', kd, v_new)
        return state, o

    xs = tuple(jnp.moveaxis(a, 2, 0) for a in (q_dec, k_dec, qk, u, w, chunk_decay))
    _, o = lax.scan(step, jnp.zeros((B, H, dk, dv), jnp.float32), xs)
    return jnp.moveaxis(o, 0, 2).reshape(B, H, T, dv)


def hgrn2_mixer(hq, hf_fwd, hf_bwd, hi, hgate, lb, norm_w):
    B, S, _ = hq.shape
    heads = lambda a, d: a.reshape(B, S, HG_HEADS, d).transpose(0, 2, 1, 3)
    q = heads(jax.nn.silu(hq.astype(jnp.float32)), HG_DK) * HG_DK ** -0.5
    v = heads(hi.astype(jnp.float32), HG_DV)

    def direction(hf):
        z = hf.astype(jnp.float32)
        f = lb + (1.0 - lb) * jax.nn.sigmoid(z)
        one_minus_f = (1.0 - lb) * jax.nn.sigmoid(-z)
        return heads(one_minus_f, HG_DK), heads(jnp.log(f), HG_DK)

    k_f, g_f = direction(hf_fwd)
    k_b, g_b = direction(hf_bwd)
    o = chunk_gla(q, k_f, v, g_f) + flip_t(chunk_gla(flip_t(q), flip_t(k_b), flip_t(v), flip_t(g_b)))
    o = head_rmsnorm(jnp.swapaxes(o, 1, 2), norm_w) * jax.nn.silu(hgate.astype(jnp.float32))
    return o.astype(hq.dtype)


def mla_mixer(c_q, kv_a, positions, q_norm_w, w_q_b, kv_norm_w, w_kv_b):
    B, S, _ = c_q.shape
    q = (rmsnorm(c_q, q_norm_w) @ w_q_b).reshape(B, S, MLA_HEADS, MLA_NOPE + MLA_ROPE)
    q = jnp.concatenate([q[..., :MLA_NOPE], rope(q[..., MLA_NOPE:], positions)], axis=-1)
    c_kv, k_pe = kv_a[..., :MLA_KV_RANK], kv_a[..., MLA_KV_RANK:]
    kv = (rmsnorm(c_kv, kv_norm_w) @ w_kv_b).reshape(B, S, MLA_HEADS, MLA_NOPE + MLA_V)
    k_nope, v = kv[..., :MLA_NOPE], kv[..., MLA_NOPE:]
    k_pe = rope(k_pe[:, :, None, :], positions)
    k = jnp.concatenate([k_nope, jnp.broadcast_to(k_pe, (B, S, MLA_HEADS, MLA_ROPE))], axis=-1)
    scale = (MLA_NOPE + MLA_ROPE) ** -0.5
    q_blocks = q.reshape(B, S // Q_BLOCK, Q_BLOCK, MLA_HEADS, MLA_NOPE + MLA_ROPE).transpose(1, 0, 2, 3, 4)

    def attend(qb):
        s = jnp.einsum('bqhd,bkhd->bhqk', qb, k, preferred_element_type=jnp.float32) * scale
        p = jax.nn.softmax(s, axis=-1)
        return jnp.einsum('bhqk,bkhd->bqhd', p.astype(v.dtype), v)

    o = lax.map(attend, q_blocks)
    return o.transpose(1, 0, 2, 3, 4).reshape(B, S, MLA_HEADS * MLA_V)


def retention_mixer(hq, hk, hv, hgate, positions, norm_w):
    B, S, _ = hq.shape
    q = to_bhtd(rope(hq.reshape(B, S, RET_HEADS, RET_DK), positions)) * RET_DK ** -0.5
    k = to_bhtd(rope(hk.reshape(B, S, RET_HEADS, RET_DK), positions))
    v = to_bhtd(hv.reshape(B, S, RET_HEADS, RET_DV))
    log_gamma = jnp.log(1.0 - jnp.exp2(-5.0 - jnp.arange(RET_HEADS, dtype=jnp.float32)))
    g_fwd = jnp.broadcast_to(log_gamma[None, :, None, None], q.shape)
    g_bwd = jnp.broadcast_to(log_gamma[::-1][None, :, None, None], q.shape)
    o = chunk_gla(q, k, v, g_fwd) + flip_t(chunk_gla(flip_t(q), flip_t(k), flip_t(v), g_bwd))
    o = head_rmsnorm(jnp.swapaxes(o, 1, 2), norm_w) * jax.nn.silu(hgate.astype(jnp.float32))
    return o.astype(hq.dtype)


def gdn_mixer(hqkv, ha_f, ha_b, hb_f, hb_b, hgate, conv_w, a_log, dt_bias, norm_w):
    B, S, C = hqkv.shape
    qkv = jax.nn.silu(lax.conv_general_dilated(
        hqkv, conv_w[:, None, :], window_strides=(1,),
        padding=[(CONV_WIDTH // 2, CONV_WIDTH // 2)],
        dimension_numbers=('NWC', 'WIO', 'NWC'), feature_group_count=C))
    q, k, v = jnp.split(qkv, [GDN_HEADS * GDN_DK, 2 * GDN_HEADS * GDN_DK], axis=-1)
    heads = lambda a, d: to_bhtd(a.reshape(B, S, GDN_HEADS, d))
    l2 = lambda a: a * lax.rsqrt(jnp.sum(a * a, axis=-1, keepdims=True) + EPS)
    q = l2(heads(q, GDN_DK)) * GDN_DK ** -0.5
    k = l2(heads(k, GDN_DK))
    v = heads(v, GDN_DV)

    def gates(ha, hb, d):
        la = -jnp.exp(a_log[d].astype(jnp.float32)) * jax.nn.softplus(
            ha.astype(jnp.float32) + dt_bias[d].astype(jnp.float32))
        beta = jax.nn.sigmoid(hb.astype(jnp.float32))
        return la.transpose(0, 2, 1), beta.transpose(0, 2, 1)

    la_f, beta_f = gates(ha_f, hb_f, 0)
    la_b, beta_b = gates(ha_b, hb_b, 1)
    o = chunk_gated_delta(q, k, v, la_f, beta_f) + flip_t(
        chunk_gated_delta(flip_t(q), flip_t(k), flip_t(v), flip_t(la_b), flip_t(beta_b)))
    o = head_rmsnorm(jnp.swapaxes(o, 1, 2), norm_w) * jax.nn.silu(hgate.astype(jnp.float32))
    return o.astype(hqkv.dtype)


def setup_inputs(seed: int = 0) -> dict:
    key = jax.random.key(seed)
    ks = jax.random.split(key, 24)
    f32 = jnp.float32
    nrm = lambda k, shape, fan_in: jax.random.normal(k, shape, f32) * fan_in ** -0.5
    gain = lambda k, shape: 1.0 + 0.02 * jax.random.normal(k, shape, f32)
    x = jax.random.normal(ks[0], (BATCH, SEQ, D_MODEL), f32)
    offset = jax.random.randint(ks[1], (BATCH, 1), 0, SEQ, dtype=jnp.int32)
    positions = jnp.arange(SEQ, dtype=jnp.int32)[None, :] + offset
    dt = jnp.exp(jax.random.uniform(ks[2], (N_ODD, 2, GDN_HEADS), f32, math.log(1e-3), math.log(1e-1)))
    return {
        'x': x,
        'positions': positions,
        'norm_mix_w': gain(ks[3], (DEPTH, D_MODEL)),
        'norm_ffn_w': gain(ks[4], (DEPTH, D_MODEL)),
        'final_norm_w': gain(ks[5], (D_MODEL,)),
        'hg_lb_logits': 0.1 * jax.random.normal(ks[6], (DEPTH + 1, HG_HEADS * HG_DK), f32),
        'even_w_in': nrm(ks[7], (N_EVEN, D_MODEL, sum(EVEN_COLS)), D_MODEL),
        'hg_norm_w': gain(ks[8], (N_EVEN, HG_HEADS * HG_DV)),
        'mla_q_norm_w': gain(ks[9], (N_EVEN, MLA_Q_RANK)),
        'mla_w_q_b': nrm(ks[10], (N_EVEN, MLA_Q_RANK, MLA_HEADS * (MLA_NOPE + MLA_ROPE)), MLA_Q_RANK),
        'mla_kv_norm_w': gain(ks[11], (N_EVEN, MLA_KV_RANK)),
        'mla_w_kv_b': nrm(ks[12], (N_EVEN, MLA_KV_RANK, MLA_HEADS * (MLA_NOPE + MLA_V)), MLA_KV_RANK),
        'even_w_out': nrm(ks[13], (N_EVEN, MIX_WIDTH, D_MODEL), MIX_WIDTH),
        'odd_w_in': nrm(ks[14], (N_ODD, D_MODEL, sum(ODD_COLS)), D_MODEL),
        'ret_norm_w': gain(ks[15], (N_ODD, RET_HEADS * RET_DV)),
        'gdn_conv_w': nrm(ks[16], (N_ODD, CONV_WIDTH, GDN_QKV), CONV_WIDTH),
        'gdn_a_log': jnp.log(jax.random.uniform(ks[17], (N_ODD, 2, GDN_HEADS), f32, 1.0, 16.0)),
        'gdn_dt_bias': dt + jnp.log(-jnp.expm1(-dt)),
        'gdn_norm_w': gain(ks[18], (N_ODD, GDN_HEADS * GDN_DV)),
        'odd_w_out': nrm(ks[19], (N_ODD, MIX_WIDTH, D_MODEL), MIX_WIDTH),
        'ffn_w_up': nrm(ks[20], (DEPTH, D_MODEL, D_FF), D_MODEL),
        'ffn_w_down': nrm(ks[21], (DEPTH, D_FF, D_MODEL), D_FF),
    }


def reference(x, positions, norm_mix_w, norm_ffn_w, final_norm_w, hg_lb_logits, even_w_in,
              hg_norm_w, mla_q_norm_w, mla_w_q_b, mla_kv_norm_w, mla_w_kv_b, even_w_out,
              odd_w_in, ret_norm_w, gdn_conv_w, gdn_a_log, gdn_dt_bias, gdn_norm_w, odd_w_out,
              ffn_w_up, ffn_w_down):
    lb_table = jnp.cumsum(jax.nn.softmax(hg_lb_logits.astype(jnp.float32), axis=0), axis=0)
    for layer in range(DEPTH):
        j = layer // 2
        h = rmsnorm(x, norm_mix_w[layer])
        if layer % 2 == 0:
            hq, hf_f, hf_b, hi, hg, c_q, kv_a = split_cols(h @ even_w_in[j], EVEN_COLS)
            o_a = hgrn2_mixer(hq, hf_f, hf_b, hi, hg, lb_table[layer], hg_norm_w[j])
            o_b = mla_mixer(c_q, kv_a, positions, mla_q_norm_w[j], mla_w_q_b[j],
                            mla_kv_norm_w[j], mla_w_kv_b[j])
            mix = jnp.concatenate([o_a, o_b], axis=-1) @ even_w_out[j]
        else:
            rq, rk, rv, rg, gqkv, ga_f, ga_b, gb_f, gb_b, gg = split_cols(h @ odd_w_in[j], ODD_COLS)
            o_c = retention_mixer(rq, rk, rv, rg, positions, ret_norm_w[j])
            o_d = gdn_mixer(gqkv, ga_f, ga_b, gb_f, gb_b, gg, gdn_conv_w[j], gdn_a_log[j],
                            gdn_dt_bias[j], gdn_norm_w[j])
            mix = jnp.concatenate([o_c, o_d], axis=-1) @ odd_w_out[j]
        x = x + mix
        h = rmsnorm(x, norm_ffn_w[layer])
        x = x + jnp.square(jax.nn.relu(h @ ffn_w_up[layer])) @ ffn_w_down[layer]
    return rmsnorm(x, final_norm_w)
```

```python
import functools
import math

import jax
import jax.numpy as jnp
from jax import lax
from jax.experimental import pallas as pl
from jax.experimental.pallas import tpu as pltpu

F32 = jnp.float32
BF16 = jnp.bfloat16
HIGHEST = lax.Precision.HIGHEST

EPS = 1e-6
CHUNK = 64
ROPE_THETA = 10000.0
LANES = 128
SUBLANES = 8

HG_HEADS, HG_DK, HG_DV = 8, 128, 128
MLA_HEADS, MLA_NOPE, MLA_ROPE, MLA_V = 8, 128, 64, 128
MLA_Q_RANK, MLA_KV_RANK = 512, 256
MLA_QK_PAD = 256
RET_HEADS, RET_DK, RET_DV = 4, 128, 256
GDN_HEADS, GDN_DK, GDN_DV = 8, 128, 128
CONV_WIDTH = 5

NT_DIMS = (((1,), (1,)), ((), ()))
TN_DIMS = (((0,), (0,)), ((), ()))


def _params(semantics, vmem_mib):
    return pltpu.CompilerParams(dimension_semantics=semantics, vmem_limit_bytes=vmem_mib * 1024 * 1024)


def _rms_rows(x, w):
    ms = jnp.mean(x * x, axis=-1, keepdims=True)
    return x * lax.rsqrt(ms + EPS) * w


def _silu(x):
    return x * jax.nn.sigmoid(x)


def _bdot(a, b):
    return jnp.dot(a.astype(BF16), b.astype(BF16), preferred_element_type=F32)


def _bdot_nt(a, b):
    return lax.dot_general(a.astype(BF16), b.astype(BF16), NT_DIMS, preferred_element_type=F32)


def _bdot_tn(a, b):
    return lax.dot_general(a.astype(BF16), b.astype(BF16), TN_DIMS, preferred_element_type=F32)


def _fdot(a, b):
    return jnp.dot(a, b, precision=HIGHEST, preferred_element_type=F32)


def _rope_table_kernel(pos_ref, inv_ref, cmul_ref, smul_ref, cos_ref, sin_ref):
    ang = pos_ref[...].astype(F32) * inv_ref[...]
    cos_ref[...] = jnp.cos(ang) * cmul_ref[...]
    sin_ref[...] = jnp.sin(ang) * smul_ref[...]


def _rope_tables(pos_col, inv_row, cmul_row, smul_row):
    s = pos_col.shape[0]
    tm = min(512, s)
    row = pl.BlockSpec((1, LANES), lambda i: (0, 0))
    tab = pl.BlockSpec((tm, LANES), lambda i: (i, 0))
    return pl.pallas_call(
        _rope_table_kernel, grid=(s // tm,),
        in_specs=[pl.BlockSpec((tm, 1), lambda i: (i, 0)), row, row, row],
        out_specs=[tab, tab],
        out_shape=[jax.ShapeDtypeStruct((s, LANES), F32)] * 2,
        compiler_params=_params(("parallel",), 16), name="rope_tables",
    )(pos_col, inv_row, cmul_row, smul_row)


NORM_ROWS = 128


def _norm_rows_to(x_ref, nw_ref, xn_ref, tm):
    def body(r, carry):
        rows = pl.ds(pl.multiple_of(r * NORM_ROWS, NORM_ROWS), NORM_ROWS)
        xn_ref[rows, :] = _rms_rows(x_ref[rows, :], nw_ref[...]).astype(BF16)
        return carry
    lax.fori_loop(0, tm // NORM_ROWS, body, 0)


def _norm_mm_kernel(x_ref, nw_ref, w_ref, ws_ref, o_ref, os_ref, xn_ref, *, tm):
    @pl.when(pl.program_id(1) == 0)
    def _():
        _norm_rows_to(x_ref, nw_ref, xn_ref, tm)
        os_ref[...] = jnp.dot(xn_ref[...], ws_ref[...], preferred_element_type=F32)
    o_ref[...] = jnp.dot(xn_ref[...], w_ref[...], preferred_element_type=F32)


def _norm_matmul(x, nw, w_main, w_side, *, tm=1024, tn=512):
    s, d = x.shape
    n, ns = w_main.shape[1], w_side.shape[1]
    tm = min(tm, s)
    return pl.pallas_call(
        functools.partial(_norm_mm_kernel, tm=tm), grid=(s // tm, n // tn),
        in_specs=[pl.BlockSpec((tm, d), lambda i, j: (i, 0)),
                  pl.BlockSpec((1, d), lambda i, j: (0, 0)),
                  pl.BlockSpec((d, tn), lambda i, j: (0, j)),
                  pl.BlockSpec((d, ns), lambda i, j: (0, 0))],
        out_specs=[pl.BlockSpec((tm, tn), lambda i, j: (i, j)),
                   pl.BlockSpec((tm, ns), lambda i, j: (i, 0))],
        out_shape=[jax.ShapeDtypeStruct((s, n), F32), jax.ShapeDtypeStruct((s, ns), F32)],
        scratch_shapes=[pltpu.VMEM((tm, d), BF16)],
        compiler_params=_params(("parallel", "arbitrary"), 48), name="norm_matmul",
    )(x, nw, w_main, w_side)


def _gla_chunk(q, k, v, b, b_mid, b_last, st, mask):
    qe = q * jnp.exp(b - b_mid)
    ke = k * jnp.exp(b_mid - b)
    scores = jnp.where(mask, _bdot_nt(qe, ke), 0.0)
    o = _bdot(scores, v) + _bdot_nt(q * jnp.exp(b), st)
    k_dec = k * jnp.exp(b_last - b)
    st_new = st * jnp.exp(b_last) + _bdot_tn(v, k_dec)
    return o, st_new


def _chunk_masks():
    row = lax.broadcasted_iota(jnp.int32, (CHUNK, CHUNK), 0)
    col = lax.broadcasted_iota(jnp.int32, (CHUNK, CHUNK), 1)
    return row, col


def _hgrn_kernel(logit_ref, qf_ref, ff_ref, vf_ref, qb_ref, fb_ref, vb_ref, of_ref, ob_ref, st_ref,
                 *, tb, layer):
    @pl.when(pl.program_id(0) == 0)
    def _():
        st_ref[...] = jnp.zeros_like(st_ref)

    lg = logit_ref[...]
    e = jnp.exp(lg - jnp.max(lg, axis=0, keepdims=True))
    lb = jnp.sum(e[0:layer + 1], axis=0, keepdims=True) / jnp.sum(e, axis=0, keepdims=True)
    row, col = _chunk_masks()
    causal, anti = row >= col, row <= col
    tri_f, tri_b = causal.astype(F32), anti.astype(F32)
    n = tb // CHUNK
    mid = CHUNK // 2
    scale = HG_DK ** -0.5

    def one(q_ref, f_ref, v_ref, o_ref, rows, h, slot, rev):
        cs = slice(h * HG_DK, (h + 1) * HG_DK)
        lbh = lb[:, cs]
        z = f_ref[rows, cs]
        f = lbh + (1.0 - lbh) * jax.nn.sigmoid(z)
        k = (1.0 - lbh) * jax.nn.sigmoid(-z)
        g = jnp.log(f)
        q = _silu(q_ref[rows, cs]) * scale
        v = v_ref[rows, cs]
        b = _fdot(tri_b if rev else tri_f, g)
        if rev:
            b_last, b_mid = b[0:1], b[CHUNK - 1 - mid:CHUNK - mid]
        else:
            b_last, b_mid = b[CHUNK - 1:CHUNK], b[mid:mid + 1]
        o, st_new = _gla_chunk(q, k, v, b, b_mid, b_last, st_ref[slot], anti if rev else causal)
        st_ref[slot] = st_new
        o_ref[rows, cs] = o

    def body(c, carry):
        rf = pl.ds(pl.multiple_of(c * CHUNK, CHUNK), CHUNK)
        rb = pl.ds(pl.multiple_of((n - 1 - c) * CHUNK, CHUNK), CHUNK)
        for h in range(HG_HEADS):
            one(qf_ref, ff_ref, vf_ref, of_ref, rf, h, h, False)
            one(qb_ref, fb_ref, vb_ref, ob_ref, rb, h, HG_HEADS + h, True)
        return carry

    lax.fori_loop(0, n, body, 0)


def _hgrn2(h_main, lb_logits, *, layer, tb=256):
    s = h_main.shape[0]
    tb = min(tb, s)
    nb = s // tb
    w = HG_HEADS * HG_DK
    fwd = lambda cb: pl.BlockSpec((tb, w), lambda i: (i, cb))
    bwd = lambda cb: pl.BlockSpec((tb, w), lambda i: (nb - 1 - i, cb))
    return pl.pallas_call(
        functools.partial(_hgrn_kernel, tb=tb, layer=layer), grid=(nb,),
        in_specs=[pl.BlockSpec(lb_logits.shape, lambda i: (0, 0)),
                  fwd(0), fwd(1), fwd(3), bwd(0), bwd(2), bwd(3)],
        out_specs=[fwd(0), bwd(0)],
        out_shape=[jax.ShapeDtypeStruct((s, w), F32)] * 2,
        scratch_shapes=[pltpu.VMEM((2 * HG_HEADS, HG_DV, HG_DK), F32)],
        compiler_params=_params(("arbitrary",), 40), name="hgrn2_scan",
    )(lb_logits, h_main, h_main, h_main, h_main, h_main, h_main)


def _ret_kernel(qf_ref, kf_ref, vf_ref, cf_ref, sf_ref, qb_ref, kb_ref, vb_ref, cb_ref, sb_ref,
                of_ref, ob_ref, st_ref, *, tb):
    @pl.when(pl.program_id(0) == 0)
    def _():
        st_ref[...] = jnp.zeros_like(st_ref)

    row, col = _chunk_masks()
    causal, anti = row >= col, row <= col
    n = tb // CHUNK
    mid = CHUNK // 2
    scale = RET_DK ** -0.5
    t_idx = lax.broadcasted_iota(jnp.int32, (CHUNK, RET_DK), 0).astype(F32)
    log_gamma = [math.log1p(-2.0 ** (-5 - h)) for h in range(RET_HEADS)]

    def rope(x, cos, sin):
        return x * cos + pltpu.roll(x, RET_DK // 2, 1) * sin

    def one(q_ref, k_ref, v_ref, c_ref, s_ref, o_ref, rows, h, slot, rev):
        cs = slice(h * RET_DK, (h + 1) * RET_DK)
        vs = slice(h * RET_DV, (h + 1) * RET_DV)
        cos, sin = c_ref[rows, :], s_ref[rows, :]
        q = rope(q_ref[rows, cs], cos, sin) * scale
        k = rope(k_ref[rows, cs], cos, sin)
        v = v_ref[rows, vs]
        lgam = log_gamma[RET_HEADS - 1 - h] if rev else log_gamma[h]
        b = (CHUNK - t_idx) * lgam if rev else (t_idx + 1.0) * lgam
        ones = jnp.ones((1, RET_DK), F32)
        b_mid = ones * ((mid + 1) * lgam)
        b_last = ones * (CHUNK * lgam)
        o, st_new = _gla_chunk(q, k, v, b, b_mid, b_last, st_ref[slot], anti if rev else causal)
        st_ref[slot] = st_new
        o_ref[rows, vs] = o

    def body(c, carry):
        rf = pl.ds(pl.multiple_of(c * CHUNK, CHUNK), CHUNK)
        rb = pl.ds(pl.multiple_of((n - 1 - c) * CHUNK, CHUNK), CHUNK)
        for h in range(RET_HEADS):
            one(qf_ref, kf_ref, vf_ref, cf_ref, sf_ref, of_ref, rf, h, h, False)
            one(qb_ref, kb_ref, vb_ref, cb_ref, sb_ref, ob_ref, rb, h, RET_HEADS + h, True)
        return carry

    lax.fori_loop(0, n, body, 0)


def _retention(h_main, cos_tab, sin_tab, *, tb=256):
    s = h_main.shape[0]
    tb = min(tb, s)
    nb = s // tb
    wk, wv = RET_HEADS * RET_DK, RET_HEADS * RET_DV
    fwd = lambda w, cb: pl.BlockSpec((tb, w), lambda i: (i, cb))
    bwd = lambda w, cb: pl.BlockSpec((tb, w), lambda i: (nb - 1 - i, cb))
    return pl.pallas_call(
        functools.partial(_ret_kernel, tb=tb), grid=(nb,),
        in_specs=[fwd(wk, 0), fwd(wk, 1), fwd(wv, 1), fwd(LANES, 0), fwd(LANES, 0),
                  bwd(wk, 0), bwd(wk, 1), bwd(wv, 1), bwd(LANES, 0), bwd(LANES, 0)],
        out_specs=[fwd(wv, 0), bwd(wv, 0)],
        out_shape=[jax.ShapeDtypeStruct((s, wv), F32)] * 2,
        scratch_shapes=[pltpu.VMEM((2 * RET_HEADS, RET_DV, RET_DK), F32)],
        compiler_params=_params(("arbitrary",), 40), name="retention_scan",
    )(h_main, h_main, h_main, cos_tab, sin_tab, h_main, h_main, h_main, cos_tab, sin_tab)


def _gated_head_norm(of_ref, ob_ref, gate_ref, nw_ref, rows, heads, hd):
    o = of_ref[rows, :] + ob_ref[rows, :]
    parts = []
    for h in range(heads):
        y = o[:, h * hd:(h + 1) * hd]
        parts.append(y * lax.rsqrt(jnp.mean(y * y, axis=-1, keepdims=True) + EPS))
    y = jnp.concatenate(parts, axis=-1)
    return y * nw_ref[...] * _silu(gate_ref[rows, :])


def _mixout_kernel(*refs, tm, groups):
    x_ref, w_ref = refs[0], refs[1]
    out_ref, lhs_ref = refs[-2], refs[-1]
    grefs = refs[2:-2]

    @pl.when(pl.program_id(1) == 0)
    def _():
        def body(r, carry):
            rows = pl.ds(pl.multiple_of(r * NORM_ROWS, NORM_ROWS), NORM_ROWS)
            pos, col = 0, 0
            for g in groups:
                if g is None:
                    val, width = grefs[pos][rows, :], grefs[pos].shape[1]
                    pos += 1
                else:
                    heads, hd = g
                    val, width = _gated_head_norm(*grefs[pos:pos + 4], rows, heads, hd), heads * hd
                    pos += 4
                lhs_ref[rows, col:col + width] = val.astype(BF16)
                col += width
            return carry
        lax.fori_loop(0, tm // NORM_ROWS, body, 0)

    out_ref[...] = x_ref[...] + jnp.dot(lhs_ref[...], w_ref[...], preferred_element_type=F32)


def _mix_out(x, w_out, group_args, groups, *, tm=512, tn=512):
    s, d = x.shape
    tm = min(tm, s)
    specs = [pl.BlockSpec((tm, tn), lambda i, j: (i, j)),
             pl.BlockSpec((w_out.shape[0], tn), lambda i, j: (0, j))]
    arrays = [x, w_out]
    for arr, cb, width, rowvec in group_args:
        arrays.append(arr)
        if rowvec:
            specs.append(pl.BlockSpec((1, width), lambda i, j: (0, 0)))
        else:
            specs.append(pl.BlockSpec((tm, width), lambda i, j, cb=cb: (i, cb)))
    return pl.pallas_call(
        functools.partial(_mixout_kernel, tm=tm, groups=groups), grid=(s // tm, d // tn),
        in_specs=specs,
        out_specs=pl.BlockSpec((tm, tn), lambda i, j: (i, j)),
        out_shape=jax.ShapeDtypeStruct((s, d), F32),
        scratch_shapes=[pltpu.VMEM((tm, w_out.shape[0]), BF16)],
        compiler_params=_params(("parallel", "arbitrary"), 48), name="mix_out",
    )(*arrays)


def _rot_pair(pr, cos, sin):
    return pr * cos + pltpu.roll(pr, MLA_ROPE, 1) * sin


def _mla_q_kernel(cq_ref, nw_ref, w_ref, cos_ref, sin_ref, q_ref, cn_ref, *, scale):
    @pl.when(pl.program_id(1) == 0)
    def _():
        cn_ref[...] = _rms_rows(cq_ref[...], nw_ref[...]).astype(BF16)
    y = jnp.dot(cn_ref[...], w_ref[...], preferred_element_type=F32)
    pe = _rot_pair(y[:, MLA_NOPE:], cos_ref[...], sin_ref[...])
    q_ref[0] = (jnp.concatenate([y[:, :MLA_NOPE], pe], axis=1) * scale).astype(BF16)


def _mla_q(h_main, q_norm_w, wq, cos_tab, sin_tab, *, tm=512):
    s = h_main.shape[0]
    tm = min(tm, s)
    cq_block = (HG_HEADS * (3 * HG_DK + 2 * HG_DV)) // MLA_Q_RANK
    scale = (MLA_NOPE + MLA_ROPE) ** -0.5
    return pl.pallas_call(
        functools.partial(_mla_q_kernel, scale=scale), grid=(s // tm, MLA_HEADS),
        in_specs=[pl.BlockSpec((tm, MLA_Q_RANK), lambda i, h: (i, cq_block)),
                  pl.BlockSpec((1, MLA_Q_RANK), lambda i, h: (0, 0)),
                  pl.BlockSpec((MLA_Q_RANK, MLA_QK_PAD), lambda i, h: (0, h)),
                  pl.BlockSpec((tm, LANES), lambda i, h: (i, 0)),
                  pl.BlockSpec((tm, LANES), lambda i, h: (i, 0))],
        out_specs=pl.BlockSpec((1, tm, MLA_QK_PAD), lambda i, h: (h, i, 0)),
        out_shape=jax.ShapeDtypeStruct((MLA_HEADS, s, MLA_QK_PAD), BF16),
        scratch_shapes=[pltpu.VMEM((tm, MLA_Q_RANK), BF16)],
        compiler_params=_params(("parallel", "arbitrary"), 32), name="mla_q_proj",
    )(h_main, q_norm_w, wq, cos_tab, sin_tab)


def _mla_kv_kernel(ckv_ref, kpe_ref, nw_ref, w_ref, cos_ref, sin_ref, k_ref, v_ref, cn_ref, pe_ref):
    @pl.when(pl.program_id(1) == 0)
    def _():
        cn_ref[...] = _rms_rows(ckv_ref[...], nw_ref[...]).astype(BF16)
        pe_ref[...] = _rot_pair(kpe_ref[...], cos_ref[...], sin_ref[...]).astype(BF16)
    y = jnp.dot(cn_ref[...], w_ref[...], preferred_element_type=F32)
    k_ref[0] = jnp.concatenate([y[:, :MLA_NOPE].astype(BF16), pe_ref[...]], axis=1)
    v_ref[0] = y[:, MLA_NOPE:].astype(BF16)


def _mla_kv(h_side, kv_norm_w, wkv, cos_tab, sin_tab, *, tm=512):
    s = h_side.shape[0]
    tm = min(tm, s)
    return pl.pallas_call(
        _mla_kv_kernel, grid=(s // tm, MLA_HEADS),
        in_specs=[pl.BlockSpec((tm, MLA_KV_RANK), lambda i, h: (i, 0)),
                  pl.BlockSpec((tm, LANES), lambda i, h: (i, MLA_KV_RANK // LANES)),
                  pl.BlockSpec((1, MLA_KV_RANK), lambda i, h: (0, 0)),
                  pl.BlockSpec((MLA_KV_RANK, MLA_NOPE + MLA_V), lambda i, h: (0, h)),
                  pl.BlockSpec((tm, LANES), lambda i, h: (i, 0)),
                  pl.BlockSpec((tm, LANES), lambda i, h: (i, 0))],
        out_specs=[pl.BlockSpec((1, tm, MLA_QK_PAD), lambda i, h: (h, i, 0)),
                   pl.BlockSpec((1, tm, MLA_V), lambda i, h: (h, i, 0))],
        out_shape=[jax.ShapeDtypeStruct((MLA_HEADS, s, MLA_QK_PAD), BF16),
                   jax.ShapeDtypeStruct((MLA_HEADS, s, MLA_V), BF16)],
        scratch_shapes=[pltpu.VMEM((tm, MLA_KV_RANK), BF16), pltpu.VMEM((tm, LANES), BF16)],
        compiler_params=_params(("parallel", "arbitrary"), 32), name="mla_kv_proj",
    )(h_side, h_side, kv_norm_w, wkv, cos_tab, sin_tab)


def _attn_kernel(q_ref, k_ref, v_ref, o_ref, *, tk, nk):
    q = q_ref[0]
    tq = q.shape[0]

    def body(j, carry):
        m, l, acc = carry
        rows = pl.ds(pl.multiple_of(j * tk, tk), tk)
        sc = lax.dot_general(q, k_ref[0, rows, :], NT_DIMS, preferred_element_type=F32)
        m_new = jnp.maximum(m, jnp.max(sc, axis=-1, keepdims=True))
        p = jnp.exp(sc - m_new)
        alpha = jnp.exp(m - m_new)
        l = alpha * l + jnp.sum(p, axis=-1, keepdims=True)
        acc = alpha * acc + jnp.dot(p.astype(BF16), v_ref[0, rows, :], preferred_element_type=F32)
        return m_new, l, acc

    init = (jnp.full((tq, 1), -jnp.inf, F32), jnp.zeros((tq, 1), F32), jnp.zeros((tq, MLA_V), F32))
    _, l, acc = lax.fori_loop(0, nk, body, init)
    o_ref[...] = (acc / l).astype(BF16)


def _attention(q, k, v, *, tq=512, tk=512):
    heads, s, _ = q.shape
    tq, tk = min(tq, s), min(tk, s)
    return pl.pallas_call(
        functools.partial(_attn_kernel, tk=tk, nk=s // tk), grid=(heads, s // tq),
        in_specs=[pl.BlockSpec((1, tq, MLA_QK_PAD), lambda h, i: (h, i, 0)),
                  pl.BlockSpec((1, s, MLA_QK_PAD), lambda h, i: (h, 0, 0)),
                  pl.BlockSpec((1, s, MLA_V), lambda h, i: (h, 0, 0))],
        out_specs=pl.BlockSpec((tq, MLA_V), lambda h, i: (i, h)),
        out_shape=jax.ShapeDtypeStruct((s, heads * MLA_V), BF16),
        compiler_params=_params(("parallel", "arbitrary"), 40), name="mla_attention",
    )(q, k, v)


def _ffn_kernel(x_ref, nw_ref, wu_ref, wd_ref, fw_ref, o_ref, xn_ref, *, tm, nf, final_norm):
    f = pl.program_id(1)

    @pl.when(f == 0)
    def _():
        _norm_rows_to(x_ref, nw_ref, xn_ref, tm)
        o_ref[...] = x_ref[...]

    a = jnp.maximum(jnp.dot(xn_ref[...], wu_ref[...], preferred_element_type=F32), 0.0)
    o_ref[...] += jnp.dot((a * a).astype(BF16), wd_ref[...], preferred_element_type=F32)

    if final_norm:
        @pl.when(f == nf - 1)
        def _():
            def body(r, carry):
                rows = pl.ds(pl.multiple_of(r * NORM_ROWS, NORM_ROWS), NORM_ROWS)
                o_ref[rows, :] = _rms_rows(o_ref[rows, :], fw_ref[...])
                return carry
            lax.fori_loop(0, tm // NORM_ROWS, body, 0)


def _ffn(x, nw, w_up, w_down, final_w, *, final_norm, tm=512, tf=1024):
    s, d = x.shape
    dff = w_up.shape[1]
    tm = min(tm, s)
    nf = dff // tf
    return pl.pallas_call(
        functools.partial(_ffn_kernel, tm=tm, nf=nf, final_norm=final_norm), grid=(s // tm, nf),
        in_specs=[pl.BlockSpec((tm, d), lambda i, f: (i, 0)),
                  pl.BlockSpec((1, d), lambda i, f: (0, 0)),
                  pl.BlockSpec((d, tf), lambda i, f: (0, f)),
                  pl.BlockSpec((tf, d), lambda i, f: (f, 0)),
                  pl.BlockSpec((1, d), lambda i, f: (0, 0))],
        out_specs=pl.BlockSpec((tm, d), lambda i, f: (i, 0)),
        out_shape=jax.ShapeDtypeStruct((s, d), F32),
        scratch_shapes=[pltpu.VMEM((tm, d), BF16)],
        compiler_params=_params(("parallel", "arbitrary"), 48), name="ffn",
    )(x, nw, w_up, w_down, final_w)


def _gdn_prep_kernel(x_ref, xp_ref, xn_ref, cw_ref, o_ref, *, tm, nblk):
    i, sec = pl.program_id(0), pl.program_id(1)
    x = x_ref[...]
    prev = jnp.where(i > 0, xp_ref[...], 0.0)
    nxt = jnp.where(i < nblk - 1, xn_ref[...], 0.0)
    rows = lax.broadcasted_iota(jnp.int32, x.shape, 0)
    half = CONV_WIDTH // 2

    def shifted(d):
        if d == 0:
            return x
        r = pltpu.roll(x, (-d) % tm, 0)
        if d < 0:
            for t in range(-d):
                r = jnp.where(rows == t, prev[SUBLANES + t + d:SUBLANES + t + d + 1], r)
        else:
            for t in range(tm - d, tm):
                r = jnp.where(rows == t, nxt[t + d - tm:t + d - tm + 1], r)
        return r

    y = shifted(-half) * cw_ref[0:1]
    for j in range(1, CONV_WIDTH):
        y = y + shifted(j - half) * cw_ref[j:j + 1]
    y = _silu(y)
    parts = []
    for h in range(GDN_HEADS):
        a = y[:, h * GDN_DK:(h + 1) * GDN_DK]
        parts.append(a * lax.rsqrt(jnp.sum(a * a, axis=-1, keepdims=True) + EPS))
    nrm = jnp.concatenate(parts, axis=-1) * jnp.where(sec == 0, GDN_DK ** -0.5, 1.0)
    o_ref[...] = jnp.where(sec == 2, y, nrm)


def _gdn_prep(h_main, conv_w, *, tm=256):
    s = h_main.shape[0]
    tm = min(tm, s)
    nblk = s // tm
    w = GDN_HEADS * GDN_DK
    base = (2 * RET_HEADS * RET_DK + 2 * RET_HEADS * RET_DV) // w
    per = tm // SUBLANES
    return pl.pallas_call(
        functools.partial(_gdn_prep_kernel, tm=tm, nblk=nblk), grid=(nblk, 3),
        in_specs=[pl.BlockSpec((tm, w), lambda i, c: (i, base + c)),
                  pl.BlockSpec((SUBLANES, w), lambda i, c: (jnp.maximum(i * per - 1, 0), base + c)),
                  pl.BlockSpec((SUBLANES, w), lambda i, c: (jnp.minimum((i + 1) * per, s // SUBLANES - 1), base + c)),
                  pl.BlockSpec((CONV_WIDTH, w), lambda i, c: (0, c))],
        out_specs=pl.BlockSpec((tm, w), lambda i, c: (i, c)),
        out_shape=jax.ShapeDtypeStruct((s, 3 * w), F32),
        compiler_params=_params(("parallel", "arbitrary"), 32), name="gdn_prep",
    )(h_main, h_main, h_main, conv_w)


def _unit_tri_inverse(a, eye, m16, m32):
    n1 = -jnp.where(m16, a, 0.0)
    n2 = _fdot(n1, n1)
    n4 = _fdot(n2, n2)
    n8 = _fdot(n4, n4)
    t = eye + n1
    t = t + _fdot(t, n2)
    t = t + _fdot(t, n4)
    t = t + _fdot(t, n8)
    e32 = jnp.where(jnp.logical_and(m32, jnp.logical_not(m16)), a, 0.0)
    t = t - _fdot(_fdot(t, e32), t)
    e64 = jnp.where(m32, 0.0, a)
    t = t - _fdot(_fdot(t, e64), t)
    return t


def _softplus(x):
    return jnp.maximum(x, 0.0) + jnp.log1p(jnp.exp(-jnp.abs(x)))


def _gdn_kernel(alog_ref, dt_ref, qf_ref, kf_ref, vf_ref, gf_ref, qb_ref, kb_ref, vb_ref, gb_ref,
                of_ref, ob_ref, st_ref, *, tb):
    @pl.when(pl.program_id(0) == 0)
    def _():
        st_ref[...] = jnp.zeros_like(st_ref)

    row, col = _chunk_masks()
    causal, anti = row >= col, row <= col
    strict_c, strict_a = row > col, row < col
    tri_f, tri_b = causal.astype(F32), anti.astype(F32)
    eye = (row == col).astype(F32)
    m16 = (row >> 4) == (col >> 4)
    m32 = (row >> 5) == (col >> 5)
    n = tb // CHUNK
    neg_a = -jnp.exp(alog_ref[...])
    dt = dt_ref[...]

    def gate_prep(g_ref, rows, rev):
        raw = g_ref[rows, :]
        la = neg_a * _softplus(raw + dt)
        beta = jax.nn.sigmoid(raw)
        g = _fdot(tri_b if rev else tri_f, la)
        return g, g.T, jnp.exp(g), beta

    def one(q_ref, k_ref, v_ref, o_ref, rows, gates, h, slot, rev):
        g, g_t, eg, beta = gates
        la_lane = (GDN_HEADS if rev else 0) + h
        b_lane = (3 * GDN_HEADS if rev else 2 * GDN_HEADS) + h
        cs = slice(h * GDN_DK, (h + 1) * GDN_DK)
        q, k, v = q_ref[rows, cs], k_ref[rows, cs], v_ref[rows, cs]
        g_col = g[:, la_lane:la_lane + 1]
        g_row = g_t[la_lane:la_lane + 1, :]
        eg_col = eg[:, la_lane:la_lane + 1]
        b_col = beta[:, b_lane:b_lane + 1]
        last = 0 if rev else CHUNK - 1
        g_last = g[last:last + 1, la_lane:la_lane + 1]
        incl, strict = (anti, strict_a) if rev else (causal, strict_c)
        decay = jnp.exp(jnp.where(incl, g_col - g_row, -jnp.inf))
        kb = k * b_col
        a_mat = jnp.where(strict, _bdot_nt(kb, k) * decay, 0.0)
        t_inv = _unit_tri_inverse(a_mat, eye, m16, m32)
        u = _fdot(t_inv, v * b_col)
        w = _fdot(t_inv, kb * eg_col)
        qk = _bdot_nt(q, k) * decay
        st = st_ref[slot]
        v_new = u - _bdot(w, st)
        o = _bdot(q * eg_col, st) + _bdot(qk, v_new)
        st_ref[slot] = st * jnp.exp(g_last) + _bdot_tn(k * jnp.exp(g_last - g_col), v_new)
        o_ref[rows, cs] = o

    def body(c, carry):
        rf = pl.ds(pl.multiple_of(c * CHUNK, CHUNK), CHUNK)
        rb = pl.ds(pl.multiple_of((n - 1 - c) * CHUNK, CHUNK), CHUNK)
        gates_f = gate_prep(gf_ref, rf, False)
        gates_b = gate_prep(gb_ref, rb, True)
        for h in range(GDN_HEADS):
            one(qf_ref, kf_ref, vf_ref, of_ref, rf, gates_f, h, h, False)
            one(qb_ref, kb_ref, vb_ref, ob_ref, rb, gates_b, h, GDN_HEADS + h, True)
        return carry

    lax.fori_loop(0, n, body, 0)


def _gdn(qkv, gates, alog_row, dt_row, *, tb=256):
    s = qkv.shape[0]
    tb = min(tb, s)
    nb = s // tb
    w = GDN_HEADS * GDN_DK
    fwd = lambda width, cb: pl.BlockSpec((tb, width), lambda i: (i, cb))
    bwd = lambda width, cb: pl.BlockSpec((tb, width), lambda i: (nb - 1 - i, cb))
    row = pl.BlockSpec((1, LANES), lambda i: (0, 0))
    return pl.pallas_call(
        functools.partial(_gdn_kernel, tb=tb), grid=(nb,),
        in_specs=[row, row,
                  fwd(w, 0), fwd(w, 1), fwd(w, 2), fwd(LANES, 0),
                  bwd(w, 0), bwd(w, 1), bwd(w, 2), bwd(LANES, 0)],
        out_specs=[fwd(w, 0), bwd(w, 0)],
        out_shape=[jax.ShapeDtypeStruct((s, w), F32)] * 2,
        scratch_shapes=[pltpu.VMEM((2 * GDN_HEADS, GDN_DK, GDN_DV), F32)],
        compiler_params=_params(("arbitrary",), 40), name="gdn_scan",
    )(alog_row, dt_row, qkv, qkv, qkv, gates, qkv, qkv, qkv, gates)


def _rot_cols(w_pe):
    half = w_pe.shape[-1] // 2
    return jnp.concatenate([-w_pe[..., half:], w_pe[..., :half]], axis=-1)


def _pad_lanes(v, fill=0.0):
    v = v.reshape(1, -1).astype(F32)
    return jnp.pad(v, ((0, 0), (0, LANES - v.shape[1])), constant_values=fill)


def kernel(x, positions, norm_mix_w, norm_ffn_w, final_norm_w, hg_lb_logits, even_w_in, hg_norm_w,
           mla_q_norm_w, mla_w_q_b, mla_kv_norm_w, mla_w_kv_b, even_w_out, odd_w_in, ret_norm_w,
           gdn_conv_w, gdn_a_log, gdn_dt_bias, gdn_norm_w, odd_w_out, ffn_w_up, ffn_w_down):
    b, s, d = x.shape
    assert b == 1 and s % CHUNK == 0
    xs = x.reshape(s, d)
    row = lambda v: v.reshape(1, -1).astype(F32)

    pos_col = positions.reshape(s, 1)
    inv_ret = ROPE_THETA ** (-jnp.arange(RET_DK // 2, dtype=F32) / (RET_DK // 2))
    inv_mla = ROPE_THETA ** (-jnp.arange(MLA_ROPE // 2, dtype=F32) / (MLA_ROPE // 2))
    ones64, zeros64 = jnp.ones((64,), F32), jnp.zeros((64,), F32)
    cos_ret, sin_ret = _rope_tables(pos_col, row(jnp.concatenate([inv_ret, inv_ret])),
                                    row(jnp.concatenate([ones64, ones64])),
                                    row(jnp.concatenate([-ones64, ones64])))
    cos_mla, sin_mla = _rope_tables(pos_col, row(jnp.concatenate([inv_mla, inv_mla, zeros64])),
                                    row(jnp.concatenate([ones64, zeros64])),
                                    row(jnp.concatenate([ones64, zeros64])))

    w_in = even_w_in[0]
    n_main = HG_HEADS * (3 * HG_DK + 2 * HG_DV) + MLA_Q_RANK
    w_kpe = w_in[:, n_main + MLA_KV_RANK:]
    w_side = jnp.concatenate([w_in[:, n_main:n_main + MLA_KV_RANK], w_kpe, _rot_cols(w_kpe)], axis=1)
    h_main, h_side = _norm_matmul(xs, row(norm_mix_w[0]), w_in[:, :n_main].astype(BF16), w_side.astype(BF16))

    o_f, o_b = _hgrn2(h_main, hg_lb_logits.astype(F32), layer=0)

    wq = mla_w_q_b[0].reshape(MLA_Q_RANK, MLA_HEADS, MLA_NOPE + MLA_ROPE)
    wq_pe = wq[..., MLA_NOPE:]
    wq = jnp.concatenate([wq[..., :MLA_NOPE], wq_pe, _rot_cols(wq_pe)], axis=-1)
    wq = wq.reshape(MLA_Q_RANK, MLA_HEADS * MLA_QK_PAD).astype(BF16)
    q = _mla_q(h_main, row(mla_q_norm_w[0]), wq, cos_mla, sin_mla)
    k, v = _mla_kv(h_side, row(mla_kv_norm_w[0]), mla_w_kv_b[0].astype(BF16), cos_mla, sin_mla)
    o_attn = _attention(q, k, v)

    wa = HG_HEADS * HG_DV
    xs = _mix_out(xs, even_w_out[0].astype(BF16),
                  [(o_f, 0, wa, False), (o_b, 0, wa, False), (h_main, 4, wa, False), (row(hg_norm_w[0]), 0, wa, True),
                   (o_attn, 0, MLA_HEADS * MLA_V, False)],
                  ((HG_HEADS, HG_DV), None))
    xs = _ffn(xs, row(norm_ffn_w[0]), ffn_w_up[0].astype(BF16), ffn_w_down[0].astype(BF16),
              row(final_norm_w), final_norm=False)

    w_in = odd_w_in[0]
    n_ret = 2 * RET_HEADS * RET_DK + 2 * RET_HEADS * RET_DV
    n_qkv = GDN_HEADS * (2 * GDN_DK + GDN_DV)
    n_gate = 4 * GDN_HEADS
    w_main = jnp.concatenate([w_in[:, :n_ret + n_qkv], w_in[:, n_ret + n_qkv + n_gate:]], axis=1)
    w_side = jnp.pad(w_in[:, n_ret + n_qkv:n_ret + n_qkv + n_gate], ((0, 0), (0, LANES - n_gate)))
    h_main, h_gates = _norm_matmul(xs, row(norm_mix_w[1]), w_main.astype(BF16), w_side.astype(BF16))

    r_f, r_b = _retention(h_main, cos_ret, sin_ret)
    qkv = _gdn_prep(h_main, gdn_conv_w[0].astype(F32))
    g_f, g_b = _gdn(qkv, h_gates, _pad_lanes(gdn_a_log[0]), _pad_lanes(gdn_dt_bias[0]))

    wr, wg = RET_HEADS * RET_DV, GDN_HEADS * GDN_DV
    xs = _mix_out(xs, odd_w_out[0].astype(BF16),
                  [(r_f, 0, wr, False), (r_b, 0, wr, False), (h_main, 2, wr, False), (row(ret_norm_w[0]), 0, wr, True),
                   (g_f, 0, wg, False), (g_b, 0, wg, False), (h_main, 6, wg, False), (row(gdn_norm_w[0]), 0, wg, True)],
                  ((RET_HEADS, RET_DV), (GDN_HEADS, GDN_DV)))
    xs = _ffn(xs, row(norm_ffn_w[1]), ffn_w_up[1].astype(BF16), ffn_w_down[1].astype(BF16),
              row(final_norm_w), final_norm=True)
    return xs.reshape(b, s, d)
```

```python
import functools
import math

import jax
import jax.numpy as jnp
from jax import lax
from jax.experimental import pallas as pl
from jax.experimental.pallas import tpu as pltpu

F32 = jnp.float32
BF16 = jnp.bfloat16

EPS = 1e-6
CHUNK = 64
ROPE_THETA = 10000.0
LANES = 128
SUBLANES = 8

HG_HEADS, HG_DK, HG_DV = 8, 128, 128
MLA_HEADS, MLA_NOPE, MLA_ROPE, MLA_V = 8, 128, 64, 128
MLA_Q_RANK, MLA_KV_RANK = 512, 256
MLA_QK_PAD = 256
RET_HEADS, RET_DK, RET_DV = 4, 128, 256
GDN_HEADS, GDN_DK, GDN_DV = 8, 128, 128
CONV_WIDTH = 5

NT_DIMS = (((1,), (1,)), ((), ()))
TN_DIMS = (((0,), (0,)), ((), ()))


def _params(semantics, vmem_mib):
    return pltpu.CompilerParams(dimension_semantics=semantics, vmem_limit_bytes=vmem_mib * 1024 * 1024)


def _rms_rows(x, w):
    ms = jnp.mean(x * x, axis=-1, keepdims=True)
    return x * lax.rsqrt(ms + EPS) * w


def _silu(x):
    return x * jax.nn.sigmoid(x)


def _bdot(a, b):
    return jnp.dot(a.astype(BF16), b.astype(BF16), preferred_element_type=F32)


def _bdot_nt(a, b):
    return lax.dot_general(a.astype(BF16), b.astype(BF16), NT_DIMS, preferred_element_type=F32)


def _bdot_tn(a, b):
    return lax.dot_general(a.astype(BF16), b.astype(BF16), TN_DIMS, preferred_element_type=F32)


def _lhs_split(a):
    hi = a.astype(BF16).astype(F32)
    return jnp.concatenate([hi, a - hi, hi], axis=1).astype(BF16)


def _rhs_split(b):
    hi = b.astype(BF16)
    lo = (b - hi.astype(F32)).astype(BF16)
    return jnp.concatenate([hi, hi, lo], axis=0)


def _rhs_split3(b):
    b1 = b.astype(BF16)
    r = b - b1.astype(F32)
    b2 = r.astype(BF16)
    b3 = (r - b2.astype(F32)).astype(BF16)
    return jnp.concatenate([b1, b2, b3], axis=0)


def _sdot(lhs_split, rhs_split):
    return jnp.dot(lhs_split, rhs_split, preferred_element_type=F32)


def _tri3(rev):
    row = lax.broadcasted_iota(jnp.int32, (CHUNK, 3 * CHUNK), 0)
    col = lax.broadcasted_iota(jnp.int32, (CHUNK, 3 * CHUNK), 1) & (CHUNK - 1)
    return jnp.where(row <= col if rev else row >= col, 1.0, 0.0).astype(BF16)


def _rope_table_kernel(pos_ref, inv_ref, cmul_ref, smul_ref, cos_ref, sin_ref):
    ang = pos_ref[...].astype(F32) * inv_ref[...]
    cos_ref[...] = jnp.cos(ang) * cmul_ref[...]
    sin_ref[...] = jnp.sin(ang) * smul_ref[...]


def _rope_tables(pos_col, inv_row, cmul_row, smul_row):
    s = pos_col.shape[0]
    tm = min(512, s)
    row = pl.BlockSpec((1, LANES), lambda i: (0, 0))
    tab = pl.BlockSpec((tm, LANES), lambda i: (i, 0))
    return pl.pallas_call(
        _rope_table_kernel, grid=(s // tm,),
        in_specs=[pl.BlockSpec((tm, 1), lambda i: (i, 0)), row, row, row],
        out_specs=[tab, tab],
        out_shape=[jax.ShapeDtypeStruct((s, LANES), F32)] * 2,
        compiler_params=_params(("parallel",), 16), name="rope_tables",
    )(pos_col, inv_row, cmul_row, smul_row)


NORM_ROWS = 128


def _norm_rows_to(x_ref, nw_ref, xn_ref, tm):
    def body(r, carry):
        rows = pl.ds(pl.multiple_of(r * NORM_ROWS, NORM_ROWS), NORM_ROWS)
        xn_ref[rows, :] = _rms_rows(x_ref[rows, :], nw_ref[...]).astype(BF16)
        return carry
    lax.fori_loop(0, tm // NORM_ROWS, body, 0)


def _norm_mm_kernel(x_ref, nw_ref, w_ref, ws_ref, o_ref, os_ref, xn_ref, *, tm):
    @pl.when(pl.program_id(1) == 0)
    def _():
        _norm_rows_to(x_ref, nw_ref, xn_ref, tm)
        os_ref[...] = jnp.dot(xn_ref[...], ws_ref[...], preferred_element_type=F32)
    o_ref[...] = jnp.dot(xn_ref[...], w_ref[...], preferred_element_type=F32)


def _norm_matmul(x, nw, w_main, w_side, *, tm=1024, tn=512):
    s, d = x.shape
    n, ns = w_main.shape[1], w_side.shape[1]
    tm = min(tm, s)
    return pl.pallas_call(
        functools.partial(_norm_mm_kernel, tm=tm), grid=(s // tm, n // tn),
        in_specs=[pl.BlockSpec((tm, d), lambda i, j: (i, 0)),
                  pl.BlockSpec((1, d), lambda i, j: (0, 0)),
                  pl.BlockSpec((d, tn), lambda i, j: (0, j)),
                  pl.BlockSpec((d, ns), lambda i, j: (0, 0))],
        out_specs=[pl.BlockSpec((tm, tn), lambda i, j: (i, j)),
                   pl.BlockSpec((tm, ns), lambda i, j: (i, 0))],
        out_shape=[jax.ShapeDtypeStruct((s, n), F32), jax.ShapeDtypeStruct((s, ns), F32)],
        scratch_shapes=[pltpu.VMEM((tm, d), BF16)],
        compiler_params=_params(("parallel", "arbitrary"), 48), name="norm_matmul",
    )(x, nw, w_main, w_side)


def _gla_chunks(ps, st_ref):
    for p in ps:
        qe = p["q"] * jnp.exp(p["b"] - p["b_mid"])
        ke = p["k"] * jnp.exp(p["b_mid"] - p["b"])
        p["scores"] = jnp.where(p["mask"], _bdot_nt(qe, ke), 0.0)
    for p in ps:
        p["o"] = _bdot(p["scores"], p["v"]) + _bdot_nt(p["q"] * jnp.exp(p["b"]), st_ref[p["slot"]])
    for p in ps:
        k_dec = p["k"] * jnp.exp(p["b_last"] - p["b"])
        st_ref[p["slot"]] = st_ref[p["slot"]] * jnp.exp(p["b_last"]) + _bdot_tn(p["v"], k_dec)
        p["o_ref"][p["rows"], p["cols"]] = p["o"]


def _chunk_masks():
    row = lax.broadcasted_iota(jnp.int32, (CHUNK, CHUNK), 0)
    col = lax.broadcasted_iota(jnp.int32, (CHUNK, CHUNK), 1)
    return row, col


def _hgrn_kernel(logit_ref, qf_ref, ff_ref, vf_ref, qb_ref, fb_ref, vb_ref, of_ref, ob_ref, st_ref,
                 *, tb, layer):
    @pl.when(pl.program_id(0) == 0)
    def _():
        st_ref[...] = jnp.zeros_like(st_ref)

    lg = logit_ref[...]
    e = jnp.exp(lg - jnp.max(lg, axis=0, keepdims=True))
    lb = jnp.sum(e[0:layer + 1], axis=0, keepdims=True) / jnp.sum(e, axis=0, keepdims=True)
    row, col = _chunk_masks()
    causal, anti = row >= col, row <= col
    tri3_f, tri3_b = _tri3(False), _tri3(True)
    n = tb // CHUNK
    mid = CHUNK // 2
    scale = HG_DK ** -0.5

    def problem(q_ref, f_ref, v_ref, o_ref, rows, h, rev):
        cs = slice(h * HG_DK, (h + 1) * HG_DK)
        lbh = lb[:, cs]
        z = f_ref[rows, cs]
        f = lbh + (1.0 - lbh) * jax.nn.sigmoid(z)
        b = _sdot(tri3_b if rev else tri3_f, _rhs_split3(jnp.log(f)))
        if rev:
            b_last, b_mid = b[0:1], b[CHUNK - 1 - mid:CHUNK - mid]
        else:
            b_last, b_mid = b[CHUNK - 1:CHUNK], b[mid:mid + 1]
        return dict(q=_silu(q_ref[rows, cs]) * scale, k=(1.0 - lbh) * jax.nn.sigmoid(-z), v=v_ref[rows, cs],
                    b=b, b_mid=b_mid, b_last=b_last, mask=anti if rev else causal,
                    slot=(HG_HEADS if rev else 0) + h, o_ref=o_ref, rows=rows, cols=cs)

    def body(c, carry):
        rf = pl.ds(pl.multiple_of(c * CHUNK, CHUNK), CHUNK)
        rb = pl.ds(pl.multiple_of((n - 1 - c) * CHUNK, CHUNK), CHUNK)
        ps = [problem(qf_ref, ff_ref, vf_ref, of_ref, rf, h, False) for h in range(HG_HEADS)]
        ps += [problem(qb_ref, fb_ref, vb_ref, ob_ref, rb, h, True) for h in range(HG_HEADS)]
        _gla_chunks(ps, st_ref)
        return carry

    lax.fori_loop(0, n, body, 0)


def _hgrn2(h_main, lb_logits, *, layer, tb=256):
    s = h_main.shape[0]
    tb = min(tb, s)
    nb = s // tb
    w = HG_HEADS * HG_DK
    fwd = lambda cb: pl.BlockSpec((tb, w), lambda i: (i, cb))
    bwd = lambda cb: pl.BlockSpec((tb, w), lambda i: (nb - 1 - i, cb))
    return pl.pallas_call(
        functools.partial(_hgrn_kernel, tb=tb, layer=layer), grid=(nb,),
        in_specs=[pl.BlockSpec(lb_logits.shape, lambda i: (0, 0)),
                  fwd(0), fwd(1), fwd(3), bwd(0), bwd(2), bwd(3)],
        out_specs=[fwd(0), bwd(0)],
        out_shape=[jax.ShapeDtypeStruct((s, w), F32)] * 2,
        scratch_shapes=[pltpu.VMEM((2 * HG_HEADS, HG_DV, HG_DK), F32)],
        compiler_params=_params(("arbitrary",), 40), name="hgrn2_scan",
    )(lb_logits, h_main, h_main, h_main, h_main, h_main, h_main)


def _ret_kernel(qf_ref, kf_ref, vf_ref, cf_ref, sf_ref, qb_ref, kb_ref, vb_ref, cb_ref, sb_ref,
                of_ref, ob_ref, st_ref, *, tb):
    @pl.when(pl.program_id(0) == 0)
    def _():
        st_ref[...] = jnp.zeros_like(st_ref)

    row, col = _chunk_masks()
    causal, anti = row >= col, row <= col
    n = tb // CHUNK
    mid = CHUNK // 2
    scale = RET_DK ** -0.5
    t_idx = lax.broadcasted_iota(jnp.int32, (CHUNK, RET_DK), 0).astype(F32)
    log_gamma = [math.log1p(-2.0 ** (-5 - h)) for h in range(RET_HEADS)]

    def rope(x, cos, sin):
        return x * cos + pltpu.roll(x, RET_DK // 2, 1) * sin

    def problem(q_ref, k_ref, v_ref, c_ref, s_ref, o_ref, rows, h, rev):
        cs = slice(h * RET_DK, (h + 1) * RET_DK)
        vs = slice(h * RET_DV, (h + 1) * RET_DV)
        cos, sin = c_ref[rows, :], s_ref[rows, :]
        lgam = log_gamma[RET_HEADS - 1 - h] if rev else log_gamma[h]
        ones = jnp.ones((1, RET_DK), F32)
        return dict(q=rope(q_ref[rows, cs], cos, sin) * scale, k=rope(k_ref[rows, cs], cos, sin),
                    v=v_ref[rows, vs], b=(CHUNK - t_idx) * lgam if rev else (t_idx + 1.0) * lgam,
                    b_mid=ones * ((mid + 1) * lgam), b_last=ones * (CHUNK * lgam),
                    mask=anti if rev else causal, slot=(RET_HEADS if rev else 0) + h,
                    o_ref=o_ref, rows=rows, cols=vs)

    def body(c, carry):
        rf = pl.ds(pl.multiple_of(c * CHUNK, CHUNK), CHUNK)
        rb = pl.ds(pl.multiple_of((n - 1 - c) * CHUNK, CHUNK), CHUNK)
        ps = [problem(qf_ref, kf_ref, vf_ref, cf_ref, sf_ref, of_ref, rf, h, False) for h in range(RET_HEADS)]
        ps += [problem(qb_ref, kb_ref, vb_ref, cb_ref, sb_ref, ob_ref, rb, h, True) for h in range(RET_HEADS)]
        _gla_chunks(ps, st_ref)
        return carry

    lax.fori_loop(0, n, body, 0)


def _retention(h_main, cos_tab, sin_tab, *, tb=256):
    s = h_main.shape[0]
    tb = min(tb, s)
    nb = s // tb
    wk, wv = RET_HEADS * RET_DK, RET_HEADS * RET_DV
    fwd = lambda w, cb: pl.BlockSpec((tb, w), lambda i: (i, cb))
    bwd = lambda w, cb: pl.BlockSpec((tb, w), lambda i: (nb - 1 - i, cb))
    return pl.pallas_call(
        functools.partial(_ret_kernel, tb=tb), grid=(nb,),
        in_specs=[fwd(wk, 0), fwd(wk, 1), fwd(wv, 1), fwd(LANES, 0), fwd(LANES, 0),
                  bwd(wk, 0), bwd(wk, 1), bwd(wv, 1), bwd(LANES, 0), bwd(LANES, 0)],
        out_specs=[fwd(wv, 0), bwd(wv, 0)],
        out_shape=[jax.ShapeDtypeStruct((s, wv), F32)] * 2,
        scratch_shapes=[pltpu.VMEM((2 * RET_HEADS, RET_DV, RET_DK), F32)],
        compiler_params=_params(("arbitrary",), 40), name="retention_scan",
    )(h_main, h_main, h_main, cos_tab, sin_tab, h_main, h_main, h_main, cos_tab, sin_tab)


def _gated_head_norm(of_ref, ob_ref, gate_ref, nw_ref, rows, heads, hd):
    o = of_ref[rows, :] + ob_ref[rows, :]
    parts = []
    for h in range(heads):
        y = o[:, h * hd:(h + 1) * hd]
        parts.append(y * lax.rsqrt(jnp.mean(y * y, axis=-1, keepdims=True) + EPS))
    y = jnp.concatenate(parts, axis=-1)
    return y * nw_ref[...] * _silu(gate_ref[rows, :])


def _mixout_kernel(*refs, tm, groups):
    x_ref, w_ref = refs[0], refs[1]
    out_ref, lhs_ref = refs[-2], refs[-1]
    grefs = refs[2:-2]

    @pl.when(pl.program_id(1) == 0)
    def _():
        def body(r, carry):
            rows = pl.ds(pl.multiple_of(r * NORM_ROWS, NORM_ROWS), NORM_ROWS)
            pos, col = 0, 0
            for g in groups:
                if g is None:
                    val, width = grefs[pos][rows, :], grefs[pos].shape[1]
                    pos += 1
                else:
                    heads, hd = g
                    val, width = _gated_head_norm(*grefs[pos:pos + 4], rows, heads, hd), heads * hd
                    pos += 4
                lhs_ref[rows, col:col + width] = val.astype(BF16)
                col += width
            return carry
        lax.fori_loop(0, tm // NORM_ROWS, body, 0)

    out_ref[...] = x_ref[...] + jnp.dot(lhs_ref[...], w_ref[...], preferred_element_type=F32)


def _mix_out(x, w_out, group_args, groups, *, tm=512, tn=512):
    s, d = x.shape
    tm = min(tm, s)
    specs = [pl.BlockSpec((tm, tn), lambda i, j: (i, j)),
             pl.BlockSpec((w_out.shape[0], tn), lambda i, j: (0, j))]
    arrays = [x, w_out]
    for arr, cb, width, rowvec in group_args:
        arrays.append(arr)
        if rowvec:
            specs.append(pl.BlockSpec((1, width), lambda i, j: (0, 0)))
        else:
            specs.append(pl.BlockSpec((tm, width), lambda i, j, cb=cb: (i, cb)))
    return pl.pallas_call(
        functools.partial(_mixout_kernel, tm=tm, groups=groups), grid=(s // tm, d // tn),
        in_specs=specs,
        out_specs=pl.BlockSpec((tm, tn), lambda i, j: (i, j)),
        out_shape=jax.ShapeDtypeStruct((s, d), F32),
        scratch_shapes=[pltpu.VMEM((tm, w_out.shape[0]), BF16)],
        compiler_params=_params(("parallel", "arbitrary"), 48), name="mix_out",
    )(*arrays)


def _rot_pair(pr, cos, sin):
    return pr * cos + pltpu.roll(pr, MLA_ROPE, 1) * sin


def _mla_q_kernel(cq_ref, nw_ref, w_ref, cos_ref, sin_ref, q_ref, cn_ref, *, scale):
    @pl.when(pl.program_id(1) == 0)
    def _():
        cn_ref[...] = _rms_rows(cq_ref[...], nw_ref[...]).astype(BF16)
    y = jnp.dot(cn_ref[...], w_ref[...], preferred_element_type=F32)
    pe = _rot_pair(y[:, MLA_NOPE:], cos_ref[...], sin_ref[...])
    q_ref[0] = (jnp.concatenate([y[:, :MLA_NOPE], pe], axis=1) * scale).astype(BF16)


def _mla_q(h_main, q_norm_w, wq, cos_tab, sin_tab, *, tm=512):
    s = h_main.shape[0]
    tm = min(tm, s)
    cq_block = (HG_HEADS * (3 * HG_DK + 2 * HG_DV)) // MLA_Q_RANK
    scale = (MLA_NOPE + MLA_ROPE) ** -0.5 * math.log2(math.e)
    return pl.pallas_call(
        functools.partial(_mla_q_kernel, scale=scale), grid=(s // tm, MLA_HEADS),
        in_specs=[pl.BlockSpec((tm, MLA_Q_RANK), lambda i, h: (i, cq_block)),
                  pl.BlockSpec((1, MLA_Q_RANK), lambda i, h: (0, 0)),
                  pl.BlockSpec((MLA_Q_RANK, MLA_QK_PAD), lambda i, h: (0, h)),
                  pl.BlockSpec((tm, LANES), lambda i, h: (i, 0)),
                  pl.BlockSpec((tm, LANES), lambda i, h: (i, 0))],
        out_specs=pl.BlockSpec((1, tm, MLA_QK_PAD), lambda i, h: (h, i, 0)),
        out_shape=jax.ShapeDtypeStruct((MLA_HEADS, s, MLA_QK_PAD), BF16),
        scratch_shapes=[pltpu.VMEM((tm, MLA_Q_RANK), BF16)],
        compiler_params=_params(("parallel", "arbitrary"), 32), name="mla_q_proj",
    )(h_main, q_norm_w, wq, cos_tab, sin_tab)


def _mla_kv_kernel(ckv_ref, kpe_ref, nw_ref, w_ref, cos_ref, sin_ref, k_ref, v_ref, cn_ref, pe_ref):
    @pl.when(pl.program_id(1) == 0)
    def _():
        cn_ref[...] = _rms_rows(ckv_ref[...], nw_ref[...]).astype(BF16)
        pe_ref[...] = _rot_pair(kpe_ref[...], cos_ref[...], sin_ref[...]).astype(BF16)
    y = jnp.dot(cn_ref[...], w_ref[...], preferred_element_type=F32)
    k_ref[0] = jnp.concatenate([y[:, :MLA_NOPE].astype(BF16), pe_ref[...]], axis=1)
    lane = lax.broadcasted_iota(jnp.int32, (y.shape[0], LANES), 1)
    ones_col = jnp.where(lane == 0, 1.0, 0.0).astype(BF16)
    v_ref[0] = jnp.concatenate([y[:, MLA_NOPE:].astype(BF16), ones_col], axis=1)


def _mla_kv(h_side, kv_norm_w, wkv, cos_tab, sin_tab, *, tm=512):
    s = h_side.shape[0]
    tm = min(tm, s)
    return pl.pallas_call(
        _mla_kv_kernel, grid=(s // tm, MLA_HEADS),
        in_specs=[pl.BlockSpec((tm, MLA_KV_RANK), lambda i, h: (i, 0)),
                  pl.BlockSpec((tm, LANES), lambda i, h: (i, MLA_KV_RANK // LANES)),
                  pl.BlockSpec((1, MLA_KV_RANK), lambda i, h: (0, 0)),
                  pl.BlockSpec((MLA_KV_RANK, MLA_NOPE + MLA_V), lambda i, h: (0, h)),
                  pl.BlockSpec((tm, LANES), lambda i, h: (i, 0)),
                  pl.BlockSpec((tm, LANES), lambda i, h: (i, 0))],
        out_specs=[pl.BlockSpec((1, tm, MLA_QK_PAD), lambda i, h: (h, i, 0)),
                   pl.BlockSpec((1, tm, 2 * MLA_V), lambda i, h: (h, i, 0))],
        out_shape=[jax.ShapeDtypeStruct((MLA_HEADS, s, MLA_QK_PAD), BF16),
                   jax.ShapeDtypeStruct((MLA_HEADS, s, 2 * MLA_V), BF16)],
        scratch_shapes=[pltpu.VMEM((tm, MLA_KV_RANK), BF16), pltpu.VMEM((tm, LANES), BF16)],
        compiler_params=_params(("parallel", "arbitrary"), 32), name="mla_kv_proj",
    )(h_side, h_side, kv_norm_w, wkv, cos_tab, sin_tab)


ATTN_SPLIT = 2
ATTN_ROWS = 32


def _attn_kernel(q_ref, k_ref, v_ref, o_ref, s_ref, p_ref, m_ref, alpha_ref, acc_ref, *, tk, nk):
    tq = q_ref.shape[1]
    sub = tq // ATTN_SPLIT
    m_ref[...] = jnp.full(m_ref.shape, -jnp.inf, F32)
    acc_ref[...] = jnp.zeros(acc_ref.shape, F32)

    def softmax_rows(i):
        for rb in range(sub // ATTN_ROWS):
            loc = slice(rb * ATTN_ROWS, (rb + 1) * ATTN_ROWS)
            glob = slice(i * sub + rb * ATTN_ROWS, i * sub + (rb + 1) * ATTN_ROWS)
            sc = s_ref[i, loc, :]
            cols = [sc[:, c * LANES:(c + 1) * LANES] for c in range(tk // LANES)]
            col_max = functools.reduce(jnp.maximum, cols)
            m_old = m_ref[glob, :]
            m_new = jnp.maximum(m_old, jnp.max(col_max, axis=-1, keepdims=True))
            m_ref[glob, :] = m_new
            alpha_ref[glob, :] = jnp.exp2(m_old - m_new)
            p_ref[i, loc, :] = jnp.concatenate([jnp.exp2(c - m_new) for c in cols], axis=1).astype(BF16)

    def body(j, carry):
        rows = pl.ds(pl.multiple_of(j * tk, tk), tk)
        k, v = k_ref[0, rows, :], v_ref[0, rows, :]
        for i in range(ATTN_SPLIT):
            s_ref[i] = lax.dot_general(q_ref[0, i * sub:(i + 1) * sub, :], k, NT_DIMS, preferred_element_type=F32)
        for i in range(ATTN_SPLIT):
            softmax_rows(i)
            rs = slice(i * sub, (i + 1) * sub)
            alpha = alpha_ref[rs, :]
            acc_ref[rs, :] = (jnp.concatenate([alpha, alpha], axis=1) * acc_ref[rs, :]
                              + jnp.dot(p_ref[i], v, preferred_element_type=F32))
        return carry

    lax.fori_loop(0, nk, body, 0)
    acc = acc_ref[...]
    o_ref[...] = (acc[:, :MLA_V] / acc[:, MLA_V:MLA_V + 1]).astype(BF16)


def _attention(q, k, v, *, tq=1024, tk=1024):
    heads, s, _ = q.shape
    tq, tk = min(tq, s), min(tk, s)
    sub = tq // ATTN_SPLIT
    return pl.pallas_call(
        functools.partial(_attn_kernel, tk=tk, nk=s // tk), grid=(heads, s // tq),
        in_specs=[pl.BlockSpec((1, tq, MLA_QK_PAD), lambda h, i: (h, i, 0)),
                  pl.BlockSpec((1, s, MLA_QK_PAD), lambda h, i: (h, 0, 0)),
                  pl.BlockSpec((1, s, 2 * MLA_V), lambda h, i: (h, 0, 0))],
        out_specs=pl.BlockSpec((tq, MLA_V), lambda h, i: (i, h)),
        out_shape=jax.ShapeDtypeStruct((s, heads * MLA_V), BF16),
        scratch_shapes=[pltpu.VMEM((ATTN_SPLIT, sub, tk), F32), pltpu.VMEM((ATTN_SPLIT, sub, tk), BF16),
                        pltpu.VMEM((tq, LANES), F32), pltpu.VMEM((tq, LANES), F32),
                        pltpu.VMEM((tq, 2 * MLA_V), F32)],
        compiler_params=_params(("parallel", "arbitrary"), 48), name="mla_attention",
    )(q, k, v)


def _ffn_kernel(x_ref, nw_ref, wu_ref, wd_ref, fw_ref, o_ref, xn_ref, *, tm, nf, final_norm):
    f = pl.program_id(1)

    @pl.when(f == 0)
    def _():
        _norm_rows_to(x_ref, nw_ref, xn_ref, tm)
        o_ref[...] = x_ref[...]

    a = jnp.maximum(jnp.dot(xn_ref[...], wu_ref[...], preferred_element_type=F32), 0.0)
    o_ref[...] += jnp.dot((a * a).astype(BF16), wd_ref[...], preferred_element_type=F32)

    if final_norm:
        @pl.when(f == nf - 1)
        def _():
            def body(r, carry):
                rows = pl.ds(pl.multiple_of(r * NORM_ROWS, NORM_ROWS), NORM_ROWS)
                o_ref[rows, :] = _rms_rows(o_ref[rows, :], fw_ref[...])
                return carry
            lax.fori_loop(0, tm // NORM_ROWS, body, 0)


def _ffn(x, nw, w_up, w_down, final_w, *, final_norm, tm=512, tf=1024):
    s, d = x.shape
    dff = w_up.shape[1]
    tm = min(tm, s)
    nf = dff // tf
    return pl.pallas_call(
        functools.partial(_ffn_kernel, tm=tm, nf=nf, final_norm=final_norm), grid=(s // tm, nf),
        in_specs=[pl.BlockSpec((tm, d), lambda i, f: (i, 0)),
                  pl.BlockSpec((1, d), lambda i, f: (0, 0)),
                  pl.BlockSpec((d, tf), lambda i, f: (0, f)),
                  pl.BlockSpec((tf, d), lambda i, f: (f, 0)),
                  pl.BlockSpec((1, d), lambda i, f: (0, 0))],
        out_specs=pl.BlockSpec((tm, d), lambda i, f: (i, 0)),
        out_shape=jax.ShapeDtypeStruct((s, d), F32),
        scratch_shapes=[pltpu.VMEM((tm, d), BF16)],
        compiler_params=_params(("parallel", "arbitrary"), 48), name="ffn",
    )(x, nw, w_up, w_down, final_w)


def _gdn_prep_kernel(x_ref, xp_ref, xn_ref, cw_ref, o_ref, *, tm, nblk):
    i, sec = pl.program_id(0), pl.program_id(1)
    x = x_ref[...]
    prev = jnp.where(i > 0, xp_ref[...], 0.0)
    nxt = jnp.where(i < nblk - 1, xn_ref[...], 0.0)
    rows = lax.broadcasted_iota(jnp.int32, x.shape, 0)
    half = CONV_WIDTH // 2

    def shifted(d):
        if d == 0:
            return x
        r = pltpu.roll(x, (-d) % tm, 0)
        if d < 0:
            for t in range(-d):
                r = jnp.where(rows == t, prev[SUBLANES + t + d:SUBLANES + t + d + 1], r)
        else:
            for t in range(tm - d, tm):
                r = jnp.where(rows == t, nxt[t + d - tm:t + d - tm + 1], r)
        return r

    y = shifted(-half) * cw_ref[0:1]
    for j in range(1, CONV_WIDTH):
        y = y + shifted(j - half) * cw_ref[j:j + 1]
    y = _silu(y)
    parts = []
    for h in range(GDN_HEADS):
        a = y[:, h * GDN_DK:(h + 1) * GDN_DK]
        parts.append(a * lax.rsqrt(jnp.sum(a * a, axis=-1, keepdims=True) + EPS))
    nrm = jnp.concatenate(parts, axis=-1) * jnp.where(sec == 0, GDN_DK ** -0.5, 1.0)
    o_ref[...] = jnp.where(sec == 2, y, nrm)


def _gdn_prep(h_main, conv_w, *, tm=256):
    s = h_main.shape[0]
    tm = min(tm, s)
    nblk = s // tm
    w = GDN_HEADS * GDN_DK
    base = (2 * RET_HEADS * RET_DK + 2 * RET_HEADS * RET_DV) // w
    per = tm // SUBLANES
    return pl.pallas_call(
        functools.partial(_gdn_prep_kernel, tm=tm, nblk=nblk), grid=(nblk, 3),
        in_specs=[pl.BlockSpec((tm, w), lambda i, c: (i, base + c)),
                  pl.BlockSpec((SUBLANES, w), lambda i, c: (jnp.maximum(i * per - 1, 0), base + c)),
                  pl.BlockSpec((SUBLANES, w), lambda i, c: (jnp.minimum((i + 1) * per, s // SUBLANES - 1), base + c)),
                  pl.BlockSpec((CONV_WIDTH, w), lambda i, c: (0, c))],
        out_specs=pl.BlockSpec((tm, w), lambda i, c: (i, c)),
        out_shape=jax.ShapeDtypeStruct((s, 3 * w), F32),
        compiler_params=_params(("parallel", "arbitrary"), 32), name="gdn_prep",
    )(h_main, h_main, h_main, conv_w)


def _softplus(x):
    return jnp.maximum(x, 0.0) + jnp.log1p(jnp.exp(-jnp.abs(x)))


def _gdn_kernel(alog_ref, dt_ref, qf_ref, kf_ref, vf_ref, gf_ref, qb_ref, kb_ref, vb_ref, gb_ref,
                of_ref, ob_ref, st_ref, *, tb):
    @pl.when(pl.program_id(0) == 0)
    def _():
        st_ref[...] = jnp.zeros_like(st_ref)

    row, col = _chunk_masks()
    causal, anti = row >= col, row <= col
    strict_c, strict_a = row > col, row < col
    tri3_f, tri3_b = _tri3(False), _tri3(True)
    eye = (row == col).astype(F32)
    same16 = (row >> 4) == (col >> 4)
    same32 = (row >> 5) == (col >> 5)
    in32 = jnp.logical_and(same32, jnp.logical_not(same16))
    n = tb // CHUNK
    neg_a = -jnp.exp(alog_ref[...])
    dt = dt_ref[...]
    heads = range(GDN_HEADS)

    def problems(q_ref, k_ref, v_ref, g_ref, o_ref, rows, rev):
        raw = g_ref[rows, :]
        la = neg_a * _softplus(raw + dt)
        beta = jax.nn.sigmoid(raw)
        g = _sdot(tri3_b if rev else tri3_f, _rhs_split3(la))
        g_t, eg = g.T, jnp.exp(g)
        la0 = GDN_HEADS if rev else 0
        b0 = (3 if rev else 2) * GDN_HEADS
        last = 0 if rev else CHUNK - 1
        out = []
        for h in heads:
            cs = slice(h * GDN_DK, (h + 1) * GDN_DK)
            out.append(dict(
                q_ref=q_ref, k_ref=k_ref, v_ref=v_ref, o_ref=o_ref, rows=rows, cs=cs,
                slot=(GDN_HEADS if rev else 0) + h,
                incl=anti if rev else causal, strict=strict_a if rev else strict_c,
                g_col=g[:, la0 + h:la0 + h + 1], g_row=g_t[la0 + h:la0 + h + 1, :],
                eg_col=eg[:, la0 + h:la0 + h + 1], b_col=beta[:, b0 + h:b0 + h + 1],
                g_last=g[last:last + 1, la0 + h:la0 + h + 1]))
        return out

    def body(c, carry):
        rf = pl.ds(pl.multiple_of(c * CHUNK, CHUNK), CHUNK)
        rb = pl.ds(pl.multiple_of((n - 1 - c) * CHUNK, CHUNK), CHUNK)
        ps = (problems(qf_ref, kf_ref, vf_ref, gf_ref, of_ref, rf, False)
              + problems(qb_ref, kb_ref, vb_ref, gb_ref, ob_ref, rb, True))
        for p in ps:
            k = p["k_ref"][p["rows"], p["cs"]]
            p["decay"] = jnp.exp(jnp.where(p["incl"], p["g_col"] - p["g_row"], -jnp.inf))
            p["kb"] = k * p["b_col"]
            kq = _bdot_nt(jnp.concatenate([p["kb"], p["q_ref"][p["rows"], p["cs"]]], axis=0), k)
            a = jnp.where(p["strict"], kq[:CHUNK] * p["decay"], 0.0)
            p["qk"] = kq[CHUNK:] * p["decay"]
            n1 = -jnp.where(same16, a, 0.0)
            p["n1r"], p["t"] = _rhs_split(n1), eye + n1
            p["n_l"] = _lhs_split(n1)
            p["e32"] = _rhs_split(jnp.where(in32, a, 0.0))
            p["e64"] = _rhs_split(jnp.where(same32, 0.0, a))
        for p in ps:
            n2 = _sdot(p["n_l"], p["n1r"])
            p["n_l"], p["n_r"] = _lhs_split(n2), _rhs_split(n2)
        for p in ps:
            p["t"] = p["t"] + _sdot(_lhs_split(p["t"]), p["n_r"])
        for p in ps:
            n4 = _sdot(p["n_l"], p["n_r"])
            p["n_l"], p["n_r"] = _lhs_split(n4), _rhs_split(n4)
        for p in ps:
            p["t"] = p["t"] + _sdot(_lhs_split(p["t"]), p["n_r"])
        for p in ps:
            p["n_r"] = _rhs_split(_sdot(p["n_l"], p["n_r"]))
        for p in ps:
            p["t"] = p["t"] + _sdot(_lhs_split(p["t"]), p["n_r"])
        for e in ("e32", "e64"):
            for p in ps:
                p["t_l"] = _lhs_split(p["t"])
                p["x"] = _sdot(p["t_l"], p[e])
            for p in ps:
                p["t"] = p["t"] - _sdot(_lhs_split(p["x"]), _rhs_split(p["t"]))
        for p in ps:
            v = p["v_ref"][p["rows"], p["cs"]]
            rhs = jnp.concatenate([v * p["b_col"], p["kb"] * p["eg_col"]], axis=1)
            p["sol"] = _sdot(_lhs_split(p["t"]), _rhs_split(rhs))
        for p in ps:
            st = st_ref[p["slot"]]
            q = p["q_ref"][p["rows"], p["cs"]]
            ws = _bdot(jnp.concatenate([p["sol"][:, GDN_DV:], q * p["eg_col"]], axis=0), st)
            p["v_new"] = p["sol"][:, :GDN_DV] - ws[:CHUNK]
            p["o"] = ws[CHUNK:]
        for p in ps:
            k = p["k_ref"][p["rows"], p["cs"]]
            p["o_ref"][p["rows"], p["cs"]] = p["o"] + _bdot(p["qk"], p["v_new"])
            st_ref[p["slot"]] = (st_ref[p["slot"]] * jnp.exp(p["g_last"])
                                 + _bdot_tn(k * jnp.exp(p["g_last"] - p["g_col"]), p["v_new"]))
        return carry

    lax.fori_loop(0, n, body, 0)


def _gdn(qkv, gates, alog_row, dt_row, *, tb=256):
    s = qkv.shape[0]
    tb = min(tb, s)
    nb = s // tb
    w = GDN_HEADS * GDN_DK
    fwd = lambda width, cb: pl.BlockSpec((tb, width), lambda i: (i, cb))
    bwd = lambda width, cb: pl.BlockSpec((tb, width), lambda i: (nb - 1 - i, cb))
    row = pl.BlockSpec((1, LANES), lambda i: (0, 0))
    return pl.pallas_call(
        functools.partial(_gdn_kernel, tb=tb), grid=(nb,),
        in_specs=[row, row,
                  fwd(w, 0), fwd(w, 1), fwd(w, 2), fwd(LANES, 0),
                  bwd(w, 0), bwd(w, 1), bwd(w, 2), bwd(LANES, 0)],
        out_specs=[fwd(w, 0), bwd(w, 0)],
        out_shape=[jax.ShapeDtypeStruct((s, w), F32)] * 2,
        scratch_shapes=[pltpu.VMEM((2 * GDN_HEADS, GDN_DK, GDN_DV), F32)],
        compiler_params=_params(("arbitrary",), 40), name="gdn_scan",
    )(alog_row, dt_row, qkv, qkv, qkv, gates, qkv, qkv, qkv, gates)


def _rot_cols(w_pe):
    half = w_pe.shape[-1] // 2
    return jnp.concatenate([-w_pe[..., half:], w_pe[..., :half]], axis=-1)


def _pad_lanes(v, fill=0.0):
    v = v.reshape(1, -1).astype(F32)
    return jnp.pad(v, ((0, 0), (0, LANES - v.shape[1])), constant_values=fill)


def kernel(x, positions, norm_mix_w, norm_ffn_w, final_norm_w, hg_lb_logits, even_w_in, hg_norm_w,
           mla_q_norm_w, mla_w_q_b, mla_kv_norm_w, mla_w_kv_b, even_w_out, odd_w_in, ret_norm_w,
           gdn_conv_w, gdn_a_log, gdn_dt_bias, gdn_norm_w, odd_w_out, ffn_w_up, ffn_w_down):
    b, s, d = x.shape
    assert b == 1 and s % CHUNK == 0
    xs = x.reshape(s, d)
    row = lambda v: v.reshape(1, -1).astype(F32)

    pos_col = positions.reshape(s, 1)
    inv_ret = ROPE_THETA ** (-jnp.arange(RET_DK // 2, dtype=F32) / (RET_DK // 2))
    inv_mla = ROPE_THETA ** (-jnp.arange(MLA_ROPE // 2, dtype=F32) / (MLA_ROPE // 2))
    ones64, zeros64 = jnp.ones((64,), F32), jnp.zeros((64,), F32)
    cos_ret, sin_ret = _rope_tables(pos_col, row(jnp.concatenate([inv_ret, inv_ret])),
                                    row(jnp.concatenate([ones64, ones64])),
                                    row(jnp.concatenate([-ones64, ones64])))
    cos_mla, sin_mla = _rope_tables(pos_col, row(jnp.concatenate([inv_mla, inv_mla, zeros64])),
                                    row(jnp.concatenate([ones64, zeros64])),
                                    row(jnp.concatenate([ones64, zeros64])))

    w_in = even_w_in[0]
    n_main = HG_HEADS * (3 * HG_DK + 2 * HG_DV) + MLA_Q_RANK
    w_kpe = w_in[:, n_main + MLA_KV_RANK:]
    w_side = jnp.concatenate([w_in[:, n_main:n_main + MLA_KV_RANK], w_kpe, _rot_cols(w_kpe)], axis=1)
    h_main, h_side = _norm_matmul(xs, row(norm_mix_w[0]), w_in[:, :n_main].astype(BF16), w_side.astype(BF16))

    o_f, o_b = _hgrn2(h_main, hg_lb_logits.astype(F32), layer=0)

    wq = mla_w_q_b[0].reshape(MLA_Q_RANK, MLA_HEADS, MLA_NOPE + MLA_ROPE)
    wq_pe = wq[..., MLA_NOPE:]
    wq = jnp.concatenate([wq[..., :MLA_NOPE], wq_pe, _rot_cols(wq_pe)], axis=-1)
    wq = wq.reshape(MLA_Q_RANK, MLA_HEADS * MLA_QK_PAD).astype(BF16)
    q = _mla_q(h_main, row(mla_q_norm_w[0]), wq, cos_mla, sin_mla)
    k, v = _mla_kv(h_side, row(mla_kv_norm_w[0]), mla_w_kv_b[0].astype(BF16), cos_mla, sin_mla)
    o_attn = _attention(q, k, v)

    wa = HG_HEADS * HG_DV
    xs = _mix_out(xs, even_w_out[0].astype(BF16),
                  [(o_f, 0, wa, False), (o_b, 0, wa, False), (h_main, 4, wa, False), (row(hg_norm_w[0]), 0, wa, True),
                   (o_attn, 0, MLA_HEADS * MLA_V, False)],
                  ((HG_HEADS, HG_DV), None))
    xs = _ffn(xs, row(norm_ffn_w[0]), ffn_w_up[0].astype(BF16), ffn_w_down[0].astype(BF16),
              row(final_norm_w), final_norm=False)

    w_in = odd_w_in[0]
    n_ret = 2 * RET_HEADS * RET_DK + 2 * RET_HEADS * RET_DV
    n_qkv = GDN_HEADS * (2 * GDN_DK + GDN_DV)
    n_gate = 4 * GDN_HEADS
    w_main = jnp.concatenate([w_in[:, :n_ret + n_qkv], w_in[:, n_ret + n_qkv + n_gate:]], axis=1)
    w_side = jnp.pad(w_in[:, n_ret + n_qkv:n_ret + n_qkv + n_gate], ((0, 0), (0, LANES - n_gate)))
    h_main, h_gates = _norm_matmul(xs, row(norm_mix_w[1]), w_main.astype(BF16), w_side.astype(BF16))

    r_f, r_b = _retention(h_main, cos_ret, sin_ret)
    qkv = _gdn_prep(h_main, gdn_conv_w[0].astype(F32))
    g_f, g_b = _gdn(qkv, h_gates, _pad_lanes(gdn_a_log[0]), _pad_lanes(gdn_dt_bias[0]))

    wr, wg = RET_HEADS * RET_DV, GDN_HEADS * GDN_DV
    xs = _mix_out(xs, odd_w_out[0].astype(BF16),
                  [(r_f, 0, wr, False), (r_b, 0, wr, False), (h_main, 2, wr, False), (row(ret_norm_w[0]), 0, wr, True),
                   (g_f, 0, wg, False), (g_b, 0, wg, False), (h_main, 6, wg, False), (row(gdn_norm_w[0]), 0, wg, True)],
                  ((RET_HEADS, RET_DV), (GDN_HEADS, GDN_DV)))
    xs = _ffn(xs, row(norm_ffn_w[1]), ffn_w_up[1].astype(BF16), ffn_w_down[1].astype(BF16),
              row(final_norm_w), final_norm=True)
    return xs.reshape(b, s, d)
```

```python
import functools
import math

import jax
import jax.numpy as jnp
from jax import lax
from jax.experimental import pallas as pl
from jax.experimental.pallas import tpu as pltpu

F32 = jnp.float32
BF16 = jnp.bfloat16

EPS = 1e-6
CHUNK = 64
ROPE_THETA = 10000.0
LANES = 128
SUBLANES = 8

HG_HEADS, HG_DK, HG_DV = 8, 128, 128
MLA_HEADS, MLA_NOPE, MLA_ROPE, MLA_V = 8, 128, 64, 128
MLA_Q_RANK, MLA_KV_RANK = 512, 256
MLA_QK_PAD = 256
RET_HEADS, RET_DK, RET_DV = 4, 128, 256
GDN_HEADS, GDN_DK, GDN_DV = 8, 128, 128
CONV_WIDTH = 5

NT_DIMS = (((1,), (1,)), ((), ()))
TN_DIMS = (((0,), (0,)), ((), ()))


def _params(semantics, vmem_mib):
    return pltpu.CompilerParams(dimension_semantics=semantics, vmem_limit_bytes=vmem_mib * 1024 * 1024)


def _rms_rows(x, w):
    ms = jnp.mean(x * x, axis=-1, keepdims=True)
    return x * lax.rsqrt(ms + EPS) * w


def _silu(x):
    return x * jax.nn.sigmoid(x)


def _bdot(a, b):
    return jnp.dot(a.astype(BF16), b.astype(BF16), preferred_element_type=F32)


def _bdot_nt(a, b):
    return lax.dot_general(a.astype(BF16), b.astype(BF16), NT_DIMS, preferred_element_type=F32)


def _bdot_tn(a, b):
    return lax.dot_general(a.astype(BF16), b.astype(BF16), TN_DIMS, preferred_element_type=F32)


def _lhs_split(a):
    hi = a.astype(BF16).astype(F32)
    return jnp.concatenate([hi, a - hi, hi], axis=1).astype(BF16)


def _rhs_split(b):
    hi = b.astype(BF16)
    lo = (b - hi.astype(F32)).astype(BF16)
    return jnp.concatenate([hi, hi, lo], axis=0)


def _rhs_split3(b):
    b1 = b.astype(BF16)
    r = b - b1.astype(F32)
    b2 = r.astype(BF16)
    b3 = (r - b2.astype(F32)).astype(BF16)
    return jnp.concatenate([b1, b2, b3], axis=0)


def _sdot(lhs_split, rhs_split):
    return jnp.dot(lhs_split, rhs_split, preferred_element_type=F32)


def _tri3(rev):
    row = lax.broadcasted_iota(jnp.int32, (CHUNK, 3 * CHUNK), 0)
    col = lax.broadcasted_iota(jnp.int32, (CHUNK, 3 * CHUNK), 1) & (CHUNK - 1)
    return jnp.where(row <= col if rev else row >= col, 1.0, 0.0).astype(BF16)


def _rope_table_kernel(pos_ref, inv_ref, cmul_ref, smul_ref, cos_ref, sin_ref):
    ang = pos_ref[...].astype(F32) * inv_ref[...]
    cos_ref[...] = jnp.cos(ang) * cmul_ref[...]
    sin_ref[...] = jnp.sin(ang) * smul_ref[...]


def _rope_tables(pos_col, inv_row, cmul_row, smul_row):
    s = pos_col.shape[0]
    tm = min(512, s)
    row = pl.BlockSpec((1, LANES), lambda i: (0, 0))
    tab = pl.BlockSpec((tm, LANES), lambda i: (i, 0))
    return pl.pallas_call(
        _rope_table_kernel, grid=(s // tm,),
        in_specs=[pl.BlockSpec((tm, 1), lambda i: (i, 0)), row, row, row],
        out_specs=[tab, tab],
        out_shape=[jax.ShapeDtypeStruct((s, LANES), F32)] * 2,
        compiler_params=_params(("parallel",), 16), name="rope_tables",
    )(pos_col, inv_row, cmul_row, smul_row)


NORM_ROWS = 128


def _norm_rows_to(x_ref, nw_ref, xn_ref, tm):
    def body(r, carry):
        rows = pl.ds(pl.multiple_of(r * NORM_ROWS, NORM_ROWS), NORM_ROWS)
        xn_ref[rows, :] = _rms_rows(x_ref[rows, :], nw_ref[...]).astype(BF16)
        return carry
    lax.fori_loop(0, tm // NORM_ROWS, body, 0)


def _norm_mm_kernel(x_ref, nw_ref, w_ref, ws_ref, o_ref, os_ref, xn_ref, *, tm):
    @pl.when(pl.program_id(1) == 0)
    def _():
        _norm_rows_to(x_ref, nw_ref, xn_ref, tm)
        os_ref[...] = jnp.dot(xn_ref[...], ws_ref[...], preferred_element_type=F32)
    o_ref[...] = jnp.dot(xn_ref[...], w_ref[...], preferred_element_type=F32)


def _norm_matmul(x, nw, w_main, w_side, *, n, tm=1024, tn=512):
    s, d = x.shape
    ns = w_side.shape[1]
    tm = min(tm, s)
    return pl.pallas_call(
        functools.partial(_norm_mm_kernel, tm=tm), grid=(s // tm, n // tn),
        in_specs=[pl.BlockSpec((tm, d), lambda i, j: (i, 0)),
                  pl.BlockSpec((1, d), lambda i, j: (0, 0)),
                  pl.BlockSpec((d, tn), lambda i, j: (0, j)),
                  pl.BlockSpec((d, ns), lambda i, j: (0, 0))],
        out_specs=[pl.BlockSpec((tm, tn), lambda i, j: (i, j)),
                   pl.BlockSpec((tm, ns), lambda i, j: (i, 0))],
        out_shape=[jax.ShapeDtypeStruct((s, n), F32), jax.ShapeDtypeStruct((s, ns), F32)],
        scratch_shapes=[pltpu.VMEM((tm, d), BF16)],
        compiler_params=_params(("parallel", "arbitrary"), 48), name="norm_matmul",
    )(x, nw, w_main, w_side)


def _gla_chunks(ps, st_ref):
    for p in ps:
        p["qe"] = p["q"] * jnp.exp(p["b"] - p["b_mid"])
        p["ke"] = p["k"] * jnp.exp(p["b_mid"] - p["b"])
        p["scores"] = jnp.where(p["mask"], _bdot_nt(p["qe"], p["ke"]), 0.0)
    for p in ps:
        q_dec = p["qe"] * jnp.exp(p["b_mid"])
        p["o"] = _bdot(p["scores"], p["v"]) + _bdot_nt(q_dec, st_ref[p["slot"]])
    for p in ps:
        k_dec = p["ke"] * jnp.exp(p["b_last"] - p["b_mid"])
        st_ref[p["slot"]] = st_ref[p["slot"]] * jnp.exp(p["b_last"]) + _bdot_tn(p["v"], k_dec)
        p["o_ref"][p["rows"], p["cols"]] = p["o"]


def _chunk_masks():
    row = lax.broadcasted_iota(jnp.int32, (CHUNK, CHUNK), 0)
    col = lax.broadcasted_iota(jnp.int32, (CHUNK, CHUNK), 1)
    return row, col


def _hgrn_kernel(logit_ref, qf_ref, ff_ref, vf_ref, qb_ref, fb_ref, vb_ref, of_ref, ob_ref, st_ref,
                 *, tb, layer):
    @pl.when(pl.program_id(0) == 0)
    def _():
        st_ref[...] = jnp.zeros_like(st_ref)

    lg = logit_ref[...]
    e = jnp.exp(lg - jnp.max(lg, axis=0, keepdims=True))
    lb = jnp.sum(e[0:layer + 1], axis=0, keepdims=True) / jnp.sum(e, axis=0, keepdims=True)
    row, col = _chunk_masks()
    causal, anti = row >= col, row <= col
    tri3_f, tri3_b = _tri3(False), _tri3(True)
    n = tb // CHUNK
    mid = CHUNK // 2
    scale = HG_DK ** -0.5

    def problem(q_ref, f_ref, v_ref, o_ref, rows, h, rev):
        cs = slice(h * HG_DK, (h + 1) * HG_DK)
        lbh = lb[:, cs]
        sig = jax.nn.sigmoid(f_ref[rows, cs])
        f = lbh + (1.0 - lbh) * sig
        b = _sdot(tri3_b if rev else tri3_f, _rhs_split3(jnp.log(f)))
        if rev:
            b_last, b_mid = b[0:1], b[CHUNK - 1 - mid:CHUNK - mid]
        else:
            b_last, b_mid = b[CHUNK - 1:CHUNK], b[mid:mid + 1]
        return dict(q=_silu(q_ref[rows, cs]) * scale, k=(1.0 - lbh) * (1.0 - sig), v=v_ref[rows, cs],
                    b=b, b_mid=b_mid, b_last=b_last, mask=anti if rev else causal,
                    slot=(HG_HEADS if rev else 0) + h, o_ref=o_ref, rows=rows, cols=cs)

    def body(c, carry):
        rf = pl.ds(pl.multiple_of(c * CHUNK, CHUNK), CHUNK)
        rb = pl.ds(pl.multiple_of((n - 1 - c) * CHUNK, CHUNK), CHUNK)
        ps = [problem(qf_ref, ff_ref, vf_ref, of_ref, rf, h, False) for h in range(HG_HEADS)]
        ps += [problem(qb_ref, fb_ref, vb_ref, ob_ref, rb, h, True) for h in range(HG_HEADS)]
        _gla_chunks(ps, st_ref)
        return carry

    lax.fori_loop(0, n, body, 0)


def _hgrn2(h_main, lb_logits, *, layer, tb=256):
    s = h_main.shape[0]
    tb = min(tb, s)
    nb = s // tb
    w = HG_HEADS * HG_DK
    fwd = lambda cb: pl.BlockSpec((tb, w), lambda i: (i, cb))
    bwd = lambda cb: pl.BlockSpec((tb, w), lambda i: (nb - 1 - i, cb))
    return pl.pallas_call(
        functools.partial(_hgrn_kernel, tb=tb, layer=layer), grid=(nb,),
        in_specs=[pl.BlockSpec(lb_logits.shape, lambda i: (0, 0)),
                  fwd(0), fwd(1), fwd(3), bwd(0), bwd(2), bwd(3)],
        out_specs=[fwd(0), bwd(0)],
        out_shape=[jax.ShapeDtypeStruct((s, w), F32)] * 2,
        scratch_shapes=[pltpu.VMEM((2 * HG_HEADS, HG_DV, HG_DK), F32)],
        compiler_params=_params(("arbitrary",), 40), name="hgrn2_scan",
    )(lb_logits, h_main, h_main, h_main, h_main, h_main, h_main)


def _ret_kernel(qf_ref, kf_ref, vf_ref, cf_ref, sf_ref, qb_ref, kb_ref, vb_ref, cb_ref, sb_ref,
                of_ref, ob_ref, st_ref, *, tb):
    @pl.when(pl.program_id(0) == 0)
    def _():
        st_ref[...] = jnp.zeros_like(st_ref)

    row, col = _chunk_masks()
    causal, anti = row >= col, row <= col
    n = tb // CHUNK
    mid = CHUNK // 2
    scale = RET_DK ** -0.5
    t_idx = lax.broadcasted_iota(jnp.int32, (CHUNK, RET_DK), 0).astype(F32)
    log_gamma = [math.log1p(-2.0 ** (-5 - h)) for h in range(RET_HEADS)]

    def rope(x, cos, sin):
        return x * cos + pltpu.roll(x, RET_DK // 2, 1) * sin

    def problem(q_ref, k_ref, v_ref, c_ref, s_ref, o_ref, rows, h, rev):
        cs = slice(h * RET_DK, (h + 1) * RET_DK)
        vs = slice(h * RET_DV, (h + 1) * RET_DV)
        cos, sin = c_ref[rows, :], s_ref[rows, :]
        lgam = log_gamma[RET_HEADS - 1 - h] if rev else log_gamma[h]
        ones = jnp.ones((1, RET_DK), F32)
        return dict(q=rope(q_ref[rows, cs], cos, sin) * scale, k=rope(k_ref[rows, cs], cos, sin),
                    v=v_ref[rows, vs], b=(CHUNK - t_idx) * lgam if rev else (t_idx + 1.0) * lgam,
                    b_mid=ones * ((mid + 1) * lgam), b_last=ones * (CHUNK * lgam),
                    mask=anti if rev else causal, slot=(RET_HEADS if rev else 0) + h,
                    o_ref=o_ref, rows=rows, cols=vs)

    def body(c, carry):
        rf = pl.ds(pl.multiple_of(c * CHUNK, CHUNK), CHUNK)
        rb = pl.ds(pl.multiple_of((n - 1 - c) * CHUNK, CHUNK), CHUNK)
        ps = [problem(qf_ref, kf_ref, vf_ref, cf_ref, sf_ref, of_ref, rf, h, False) for h in range(RET_HEADS)]
        ps += [problem(qb_ref, kb_ref, vb_ref, cb_ref, sb_ref, ob_ref, rb, h, True) for h in range(RET_HEADS)]
        _gla_chunks(ps, st_ref)
        return carry

    lax.fori_loop(0, n, body, 0)


def _retention(h_main, cos_tab, sin_tab, *, tb=256):
    s = h_main.shape[0]
    tb = min(tb, s)
    nb = s // tb
    wk, wv = RET_HEADS * RET_DK, RET_HEADS * RET_DV
    fwd = lambda w, cb: pl.BlockSpec((tb, w), lambda i: (i, cb))
    bwd = lambda w, cb: pl.BlockSpec((tb, w), lambda i: (nb - 1 - i, cb))
    return pl.pallas_call(
        functools.partial(_ret_kernel, tb=tb), grid=(nb,),
        in_specs=[fwd(wk, 0), fwd(wk, 1), fwd(wv, 1), fwd(LANES, 0), fwd(LANES, 0),
                  bwd(wk, 0), bwd(wk, 1), bwd(wv, 1), bwd(LANES, 0), bwd(LANES, 0)],
        out_specs=[fwd(wv, 0), bwd(wv, 0)],
        out_shape=[jax.ShapeDtypeStruct((s, wv), F32)] * 2,
        scratch_shapes=[pltpu.VMEM((2 * RET_HEADS, RET_DV, RET_DK), F32)],
        compiler_params=_params(("arbitrary",), 40), name="retention_scan",
    )(h_main, h_main, h_main, cos_tab, sin_tab, h_main, h_main, h_main, cos_tab, sin_tab)


def _gated_head_norm(of_ref, ob_ref, gate_ref, nw_ref, rows, heads, hd):
    o = of_ref[rows, :] + ob_ref[rows, :]
    parts = []
    for h in range(heads):
        y = o[:, h * hd:(h + 1) * hd]
        parts.append(y * lax.rsqrt(jnp.mean(y * y, axis=-1, keepdims=True) + EPS))
    y = jnp.concatenate(parts, axis=-1)
    return y * nw_ref[...] * _silu(gate_ref[rows, :])


def _mixout_kernel(*refs, tm, groups):
    x_ref, w_ref = refs[0], refs[1]
    out_ref, lhs_ref = refs[-2], refs[-1]
    grefs = refs[2:-2]

    def body(r, carry):
        rows = pl.ds(pl.multiple_of(r * NORM_ROWS, NORM_ROWS), NORM_ROWS)
        pos, col = 0, 0
        for g in groups:
            if g is None:
                val, width = grefs[pos][rows, :], grefs[pos].shape[1]
                pos += 1
            else:
                heads, hd = g
                val, width = _gated_head_norm(*grefs[pos:pos + 4], rows, heads, hd), heads * hd
                pos += 4
            lhs_ref[rows, col:col + width] = val.astype(BF16)
            col += width
        return carry
    lax.fori_loop(0, tm // NORM_ROWS, body, 0)

    out_ref[...] = x_ref[...] + jnp.dot(lhs_ref[...], w_ref[...], preferred_element_type=F32)


def _mix_out(x, w_out, group_args, groups, *, tm=256):
    s, d = x.shape
    tm = min(tm, s)
    specs = [pl.BlockSpec((tm, d), lambda i: (i, 0)),
             pl.BlockSpec(w_out.shape, lambda i: (0, 0), pipeline_mode=pl.Buffered(1))]
    arrays = [x, w_out]
    for arr, cb, width, rowvec in group_args:
        arrays.append(arr)
        if rowvec:
            specs.append(pl.BlockSpec((1, width), lambda i: (0, 0)))
        else:
            specs.append(pl.BlockSpec((tm, width), lambda i, cb=cb: (i, cb)))
    return pl.pallas_call(
        functools.partial(_mixout_kernel, tm=tm, groups=groups), grid=(s // tm,),
        in_specs=specs,
        out_specs=pl.BlockSpec((tm, d), lambda i: (i, 0)),
        out_shape=jax.ShapeDtypeStruct((s, d), F32),
        scratch_shapes=[pltpu.VMEM((tm, w_out.shape[0]), BF16)],
        compiler_params=_params(("parallel",), 40), name="mix_out",
    )(*arrays)


def _rot_pair(pr, cos, sin):
    return pr * cos + pltpu.roll(pr, MLA_ROPE, 1) * sin


def _mla_proj_kernel(cq_ref, ckv_ref, kpe_ref, qnw_ref, kvnw_ref, wq_ref, wkv_ref, cos_ref, sin_ref,
                     q_ref, k_ref, v_ref, *, scale):
    cos, sin = cos_ref[...], sin_ref[...]
    cqn = _rms_rows(cq_ref[...], qnw_ref[...]).astype(BF16)
    ckvn = _rms_rows(ckv_ref[...], kvnw_ref[...]).astype(BF16)
    pe = _rot_pair(kpe_ref[...], cos, sin).astype(BF16)
    lane = lax.broadcasted_iota(jnp.int32, pe.shape, 1)
    ones_col = jnp.where(lane == 0, 1.0, 0.0).astype(BF16)
    for h in range(MLA_HEADS):
        cols = slice(h * MLA_QK_PAD, (h + 1) * MLA_QK_PAD)
        yq = jnp.dot(cqn, wq_ref[:, cols], preferred_element_type=F32)
        q_rope = _rot_pair(yq[:, MLA_NOPE:], cos, sin)
        q_ref[h] = (jnp.concatenate([yq[:, :MLA_NOPE], q_rope], axis=1) * scale).astype(BF16)
        ykv = jnp.dot(ckvn, wkv_ref[:, cols], preferred_element_type=F32)
        k_ref[h] = jnp.concatenate([ykv[:, :MLA_NOPE].astype(BF16), pe], axis=1)
        v_ref[h] = jnp.concatenate([ykv[:, MLA_NOPE:].astype(BF16), ones_col], axis=1)


def _mla_proj(h_main, h_side, q_norm_w, kv_norm_w, wq, wkv, cos_tab, sin_tab, *, tm=512):
    s = h_main.shape[0]
    tm = min(tm, s)
    cq_block = (HG_HEADS * (3 * HG_DK + 2 * HG_DV)) // MLA_Q_RANK
    scale = (MLA_NOPE + MLA_ROPE) ** -0.5 * math.log2(math.e)
    full = lambda a: pl.BlockSpec(a.shape, lambda i: (0, 0))
    tab = pl.BlockSpec((tm, LANES), lambda i: (i, 0))
    head_out = pl.BlockSpec((MLA_HEADS, tm, MLA_QK_PAD), lambda i: (0, i, 0))
    return pl.pallas_call(
        functools.partial(_mla_proj_kernel, scale=scale), grid=(s // tm,),
        in_specs=[pl.BlockSpec((tm, MLA_Q_RANK), lambda i: (i, cq_block)),
                  pl.BlockSpec((tm, MLA_KV_RANK), lambda i: (i, 0)),
                  pl.BlockSpec((tm, LANES), lambda i: (i, MLA_KV_RANK // LANES)),
                  full(q_norm_w), full(kv_norm_w), full(wq), full(wkv), tab, tab],
        out_specs=[head_out, head_out, head_out],
        out_shape=[jax.ShapeDtypeStruct((MLA_HEADS, s, MLA_QK_PAD), BF16)] * 3,
        compiler_params=_params(("parallel",), 40), name="mla_proj",
    )(h_main, h_side, h_side, q_norm_w, kv_norm_w, wq, wkv, cos_tab, sin_tab)


ATTN_SPLIT = 2
ATTN_ROWS = 32


def _attn_kernel(q_ref, k_ref, v_ref, o_ref, s_ref, p_ref, m_ref, alpha_ref, acc_ref, *, tk, nk):
    tq = q_ref.shape[1]
    sub = tq // ATTN_SPLIT
    m_ref[...] = jnp.full(m_ref.shape, -jnp.inf, F32)
    acc_ref[...] = jnp.zeros(acc_ref.shape, F32)

    def softmax_rows(i):
        for rb in range(sub // ATTN_ROWS):
            loc = slice(rb * ATTN_ROWS, (rb + 1) * ATTN_ROWS)
            glob = slice(i * sub + rb * ATTN_ROWS, i * sub + (rb + 1) * ATTN_ROWS)
            sc = s_ref[i, loc, :]
            cols = [sc[:, c * LANES:(c + 1) * LANES] for c in range(tk // LANES)]
            col_max = functools.reduce(jnp.maximum, cols)
            m_old = m_ref[glob, :]
            m_new = jnp.maximum(m_old, jnp.max(col_max, axis=-1, keepdims=True))
            m_ref[glob, :] = m_new
            alpha_ref[glob, :] = jnp.exp2(m_old - m_new)
            p_ref[i, loc, :] = jnp.concatenate([jnp.exp2(c - m_new) for c in cols], axis=1).astype(BF16)

    def body(j, carry):
        rows = pl.ds(pl.multiple_of(j * tk, tk), tk)
        k, v = k_ref[0, rows, :], v_ref[0, rows, :]
        for i in range(ATTN_SPLIT):
            s_ref[i] = lax.dot_general(q_ref[0, i * sub:(i + 1) * sub, :], k, NT_DIMS, preferred_element_type=F32)
        for i in range(ATTN_SPLIT):
            softmax_rows(i)
            rs = slice(i * sub, (i + 1) * sub)
            alpha = alpha_ref[rs, :]
            acc_ref[rs, :] = (jnp.concatenate([alpha, alpha], axis=1) * acc_ref[rs, :]
                              + jnp.dot(p_ref[i], v, preferred_element_type=F32))
        return carry

    lax.fori_loop(0, nk, body, 0)
    acc = acc_ref[...]
    o_ref[...] = (acc[:, :MLA_V] / acc[:, MLA_V:MLA_V + 1]).astype(BF16)


def _attention(q, k, v, *, tq=1024, tk=2048):
    heads, s, _ = q.shape
    tq, tk = min(tq, s), min(tk, s)
    sub = tq // ATTN_SPLIT
    return pl.pallas_call(
        functools.partial(_attn_kernel, tk=tk, nk=s // tk), grid=(heads, s // tq),
        in_specs=[pl.BlockSpec((1, tq, MLA_QK_PAD), lambda h, i: (h, i, 0)),
                  pl.BlockSpec((1, s, MLA_QK_PAD), lambda h, i: (h, 0, 0)),
                  pl.BlockSpec((1, s, 2 * MLA_V), lambda h, i: (h, 0, 0))],
        out_specs=pl.BlockSpec((tq, MLA_V), lambda h, i: (i, h)),
        out_shape=jax.ShapeDtypeStruct((s, heads * MLA_V), BF16),
        scratch_shapes=[pltpu.VMEM((ATTN_SPLIT, sub, tk), F32), pltpu.VMEM((ATTN_SPLIT, sub, tk), BF16),
                        pltpu.VMEM((tq, LANES), F32), pltpu.VMEM((tq, LANES), F32),
                        pltpu.VMEM((tq, 2 * MLA_V), F32)],
        compiler_params=_params(("parallel", "arbitrary"), 48), name="mla_attention",
    )(q, k, v)


def _ffn_kernel(x_ref, nw_ref, wu_ref, wd_ref, fw_ref, o_ref, xn_ref, *, tm, nf, final_norm):
    f = pl.program_id(1)

    @pl.when(f == 0)
    def _():
        _norm_rows_to(x_ref, nw_ref, xn_ref, tm)
        o_ref[...] = x_ref[...]

    a = jnp.maximum(jnp.dot(xn_ref[...], wu_ref[...], preferred_element_type=F32), 0.0)
    o_ref[...] += jnp.dot((a * a).astype(BF16), wd_ref[...], preferred_element_type=F32)

    if final_norm:
        @pl.when(f == nf - 1)
        def _():
            def body(r, carry):
                rows = pl.ds(pl.multiple_of(r * NORM_ROWS, NORM_ROWS), NORM_ROWS)
                o_ref[rows, :] = _rms_rows(o_ref[rows, :], fw_ref[...])
                return carry
            lax.fori_loop(0, tm // NORM_ROWS, body, 0)


def _ffn(x, nw, w_up, w_down, final_w, *, final_norm, tm=512, tf=1024):
    s, d = x.shape
    dff = w_up.shape[1]
    tm = min(tm, s)
    nf = dff // tf
    return pl.pallas_call(
        functools.partial(_ffn_kernel, tm=tm, nf=nf, final_norm=final_norm), grid=(s // tm, nf),
        in_specs=[pl.BlockSpec((tm, d), lambda i, f: (i, 0)),
                  pl.BlockSpec((1, d), lambda i, f: (0, 0)),
                  pl.BlockSpec((d, tf), lambda i, f: (0, f)),
                  pl.BlockSpec((tf, d), lambda i, f: (f, 0)),
                  pl.BlockSpec((1, d), lambda i, f: (0, 0))],
        out_specs=pl.BlockSpec((tm, d), lambda i, f: (i, 0)),
        out_shape=jax.ShapeDtypeStruct((s, d), F32),
        scratch_shapes=[pltpu.VMEM((tm, d), BF16)],
        compiler_params=_params(("parallel", "arbitrary"), 48), name="ffn",
    )(x, nw, w_up, w_down, final_w)


def _gdn_prep_kernel(x_ref, xp_ref, xn_ref, cw_ref, o_ref, *, tm, nblk):
    i, sec = pl.program_id(0), pl.program_id(1)
    x = x_ref[...]
    prev = jnp.where(i > 0, xp_ref[...], 0.0)
    nxt = jnp.where(i < nblk - 1, xn_ref[...], 0.0)
    rows = lax.broadcasted_iota(jnp.int32, x.shape, 0)
    half = CONV_WIDTH // 2

    def shifted(d):
        if d == 0:
            return x
        r = pltpu.roll(x, (-d) % tm, 0)
        if d < 0:
            for t in range(-d):
                r = jnp.where(rows == t, prev[SUBLANES + t + d:SUBLANES + t + d + 1], r)
        else:
            for t in range(tm - d, tm):
                r = jnp.where(rows == t, nxt[t + d - tm:t + d - tm + 1], r)
        return r

    y = shifted(-half) * cw_ref[0:1]
    for j in range(1, CONV_WIDTH):
        y = y + shifted(j - half) * cw_ref[j:j + 1]
    y = _silu(y)
    parts = []
    for h in range(GDN_HEADS):
        a = y[:, h * GDN_DK:(h + 1) * GDN_DK]
        parts.append(a * lax.rsqrt(jnp.sum(a * a, axis=-1, keepdims=True) + EPS))
    nrm = jnp.concatenate(parts, axis=-1) * jnp.where(sec == 0, GDN_DK ** -0.5, 1.0)
    o_ref[...] = jnp.where(sec == 2, y, nrm)


def _gdn_prep(h_main, conv_w, *, tm=256):
    s = h_main.shape[0]
    tm = min(tm, s)
    nblk = s // tm
    w = GDN_HEADS * GDN_DK
    base = (2 * RET_HEADS * RET_DK + 2 * RET_HEADS * RET_DV) // w
    per = tm // SUBLANES
    return pl.pallas_call(
        functools.partial(_gdn_prep_kernel, tm=tm, nblk=nblk), grid=(nblk, 3),
        in_specs=[pl.BlockSpec((tm, w), lambda i, c: (i, base + c)),
                  pl.BlockSpec((SUBLANES, w), lambda i, c: (jnp.maximum(i * per - 1, 0), base + c)),
                  pl.BlockSpec((SUBLANES, w), lambda i, c: (jnp.minimum((i + 1) * per, s // SUBLANES - 1), base + c)),
                  pl.BlockSpec((CONV_WIDTH, w), lambda i, c: (0, c))],
        out_specs=pl.BlockSpec((tm, w), lambda i, c: (i, c)),
        out_shape=jax.ShapeDtypeStruct((s, 3 * w), F32),
        compiler_params=_params(("parallel", "arbitrary"), 32), name="gdn_prep",
    )(h_main, h_main, h_main, conv_w)


def _softplus(x):
    return jnp.maximum(x, 0.0) + jnp.log1p(jnp.exp(-jnp.abs(x)))


def _gdn_kernel(alog_ref, dt_ref, qf_ref, kf_ref, vf_ref, gf_ref, qb_ref, kb_ref, vb_ref, gb_ref,
                of_ref, ob_ref, st_ref, *, tb):
    @pl.when(pl.program_id(0) == 0)
    def _():
        st_ref[...] = jnp.zeros_like(st_ref)

    row, col = _chunk_masks()
    causal, anti = row >= col, row <= col
    strict_c, strict_a = row > col, row < col
    tri3_f, tri3_b = _tri3(False), _tri3(True)
    eye = (row == col).astype(F32)
    same16 = (row >> 4) == (col >> 4)
    same32 = (row >> 5) == (col >> 5)
    in32 = jnp.logical_and(same32, jnp.logical_not(same16))
    n = tb // CHUNK
    neg_a = -jnp.exp(alog_ref[...])
    dt = dt_ref[...]
    heads = range(GDN_HEADS)

    def problems(q_ref, k_ref, v_ref, g_ref, o_ref, rows, rev):
        raw = g_ref[rows, :]
        la = neg_a * _softplus(raw + dt)
        beta = jax.nn.sigmoid(raw)
        g = _sdot(tri3_b if rev else tri3_f, _rhs_split3(la))
        g_t, eg = g.T, jnp.exp(g)
        la0 = GDN_HEADS if rev else 0
        b0 = (3 if rev else 2) * GDN_HEADS
        last = 0 if rev else CHUNK - 1
        out = []
        for h in heads:
            cs = slice(h * GDN_DK, (h + 1) * GDN_DK)
            out.append(dict(
                q_ref=q_ref, k_ref=k_ref, v_ref=v_ref, o_ref=o_ref, rows=rows, cs=cs,
                slot=(GDN_HEADS if rev else 0) + h,
                incl=anti if rev else causal, strict=strict_a if rev else strict_c,
                g_col=g[:, la0 + h:la0 + h + 1], g_row=g_t[la0 + h:la0 + h + 1, :],
                eg_col=eg[:, la0 + h:la0 + h + 1], b_col=beta[:, b0 + h:b0 + h + 1],
                g_last=g[last:last + 1, la0 + h:la0 + h + 1]))
        return out

    def body(c, carry):
        rf = pl.ds(pl.multiple_of(c * CHUNK, CHUNK), CHUNK)
        rb = pl.ds(pl.multiple_of((n - 1 - c) * CHUNK, CHUNK), CHUNK)
        ps = (problems(qf_ref, kf_ref, vf_ref, gf_ref, of_ref, rf, False)
              + problems(qb_ref, kb_ref, vb_ref, gb_ref, ob_ref, rb, True))
        for p in ps:
            k = p["k_ref"][p["rows"], p["cs"]]
            p["decay"] = jnp.exp(jnp.where(p["incl"], p["g_col"] - p["g_row"], -jnp.inf))
            p["kb"] = k * p["b_col"]
            kq = _bdot_nt(jnp.concatenate([p["kb"], p["q_ref"][p["rows"], p["cs"]]], axis=0), k)
            a = jnp.where(p["strict"], kq[:CHUNK] * p["decay"], 0.0)
            p["qk"] = kq[CHUNK:] * p["decay"]
            n1 = -jnp.where(same16, a, 0.0)
            p["n1r"], p["t"] = _rhs_split(n1), eye + n1
            p["n_l"] = _lhs_split(n1)
            p["e32"] = _rhs_split(jnp.where(in32, a, 0.0))
            p["e64"] = _rhs_split(jnp.where(same32, 0.0, a))
        for p in ps:
            n2 = _sdot(p["n_l"], p["n1r"])
            p["n_l"], p["n_r"] = _lhs_split(n2), _rhs_split(n2)
        for p in ps:
            p["t"] = p["t"] + _sdot(_lhs_split(p["t"]), p["n_r"])
        for p in ps:
            n4 = _sdot(p["n_l"], p["n_r"])
            p["n_l"], p["n_r"] = _lhs_split(n4), _rhs_split(n4)
        for p in ps:
            p["t"] = p["t"] + _sdot(_lhs_split(p["t"]), p["n_r"])
        for p in ps:
            p["n_r"] = _rhs_split(_sdot(p["n_l"], p["n_r"]))
        for p in ps:
            p["t"] = p["t"] + _sdot(_lhs_split(p["t"]), p["n_r"])
        for e in ("e32", "e64"):
            for p in ps:
                p["t_l"] = _lhs_split(p["t"])
                p["x"] = _sdot(p["t_l"], p[e])
            for p in ps:
                p["t"] = p["t"] - _sdot(_lhs_split(p["x"]), _rhs_split(p["t"]))
        for p in ps:
            v = p["v_ref"][p["rows"], p["cs"]]
            rhs = jnp.concatenate([v * p["b_col"], p["kb"] * p["eg_col"]], axis=1)
            p["sol"] = _sdot(_lhs_split(p["t"]), _rhs_split(rhs))
        for p in ps:
            st = st_ref[p["slot"]]
            q = p["q_ref"][p["rows"], p["cs"]]
            ws = _bdot(jnp.concatenate([p["sol"][:, GDN_DV:], q * p["eg_col"]], axis=0), st)
            p["v_new"] = p["sol"][:, :GDN_DV] - ws[:CHUNK]
            p["o"] = ws[CHUNK:]
        for p in ps:
            k = p["k_ref"][p["rows"], p["cs"]]
            p["o_ref"][p["rows"], p["cs"]] = p["o"] + _bdot(p["qk"], p["v_new"])
            st_ref[p["slot"]] = (st_ref[p["slot"]] * jnp.exp(p["g_last"])
                                 + _bdot_tn(k * jnp.exp(p["g_last"] - p["g_col"]), p["v_new"]))
        return carry

    lax.fori_loop(0, n, body, 0)


def _gdn(qkv, gates, alog_row, dt_row, *, tb=256):
    s = qkv.shape[0]
    tb = min(tb, s)
    nb = s // tb
    w = GDN_HEADS * GDN_DK
    fwd = lambda width, cb: pl.BlockSpec((tb, width), lambda i: (i, cb))
    bwd = lambda width, cb: pl.BlockSpec((tb, width), lambda i: (nb - 1 - i, cb))
    row = pl.BlockSpec((1, LANES), lambda i: (0, 0))
    return pl.pallas_call(
        functools.partial(_gdn_kernel, tb=tb), grid=(nb,),
        in_specs=[row, row,
                  fwd(w, 0), fwd(w, 1), fwd(w, 2), fwd(LANES, 0),
                  bwd(w, 0), bwd(w, 1), bwd(w, 2), bwd(LANES, 0)],
        out_specs=[fwd(w, 0), bwd(w, 0)],
        out_shape=[jax.ShapeDtypeStruct((s, w), F32)] * 2,
        scratch_shapes=[pltpu.VMEM((2 * GDN_HEADS, GDN_DK, GDN_DV), F32)],
        compiler_params=_params(("arbitrary",), 40), name="gdn_scan",
    )(alog_row, dt_row, qkv, qkv, qkv, gates, qkv, qkv, qkv, gates)


def _rot_cols(w_pe):
    half = w_pe.shape[-1] // 2
    return jnp.concatenate([-w_pe[..., half:], w_pe[..., :half]], axis=-1)


def _pad_lanes(v, fill=0.0):
    v = v.reshape(1, -1).astype(F32)
    return jnp.pad(v, ((0, 0), (0, LANES - v.shape[1])), constant_values=fill)


def kernel(x, positions, norm_mix_w, norm_ffn_w, final_norm_w, hg_lb_logits, even_w_in, hg_norm_w,
           mla_q_norm_w, mla_w_q_b, mla_kv_norm_w, mla_w_kv_b, even_w_out, odd_w_in, ret_norm_w,
           gdn_conv_w, gdn_a_log, gdn_dt_bias, gdn_norm_w, odd_w_out, ffn_w_up, ffn_w_down):
    b, s, d = x.shape
    assert b == 1 and s % CHUNK == 0
    xs = x.reshape(s, d)
    row = lambda v: v.reshape(1, -1).astype(F32)

    pos_col = positions.reshape(s, 1)
    inv_ret = ROPE_THETA ** (-jnp.arange(RET_DK // 2, dtype=F32) / (RET_DK // 2))
    inv_mla = ROPE_THETA ** (-jnp.arange(MLA_ROPE // 2, dtype=F32) / (MLA_ROPE // 2))
    ones64, zeros64 = jnp.ones((64,), F32), jnp.zeros((64,), F32)
    cos_ret, sin_ret = _rope_tables(pos_col, row(jnp.concatenate([inv_ret, inv_ret])),
                                    row(jnp.concatenate([ones64, ones64])),
                                    row(jnp.concatenate([-ones64, ones64])))
    cos_mla, sin_mla = _rope_tables(pos_col, row(jnp.concatenate([inv_mla, inv_mla, zeros64])),
                                    row(jnp.concatenate([ones64, zeros64])),
                                    row(jnp.concatenate([ones64, zeros64])))

    w_in = even_w_in[0].astype(BF16)
    n_main = HG_HEADS * (3 * HG_DK + 2 * HG_DV) + MLA_Q_RANK
    w_kpe = w_in[:, n_main + MLA_KV_RANK:]
    w_side = jnp.concatenate([w_in[:, n_main:n_main + MLA_KV_RANK], w_kpe, _rot_cols(w_kpe)], axis=1)
    h_main, h_side = _norm_matmul(xs, row(norm_mix_w[0]), w_in, w_side, n=n_main, tn=512)

    o_f, o_b = _hgrn2(h_main, hg_lb_logits.astype(F32), layer=0)

    wq = mla_w_q_b[0].reshape(MLA_Q_RANK, MLA_HEADS, MLA_NOPE + MLA_ROPE)
    wq_pe = wq[..., MLA_NOPE:]
    wq = jnp.concatenate([wq[..., :MLA_NOPE], wq_pe, _rot_cols(wq_pe)], axis=-1)
    wq = wq.reshape(MLA_Q_RANK, MLA_HEADS * MLA_QK_PAD).astype(BF16)
    q, k, v = _mla_proj(h_main, h_side, row(mla_q_norm_w[0]), row(mla_kv_norm_w[0]), wq,
                        mla_w_kv_b[0].astype(BF16), cos_mla, sin_mla)
    o_attn = _attention(q, k, v)

    wa = HG_HEADS * HG_DV
    xs = _mix_out(xs, even_w_out[0].astype(BF16),
                  [(o_f, 0, wa, False), (o_b, 0, wa, False), (h_main, 4, wa, False), (row(hg_norm_w[0]), 0, wa, True),
                   (o_attn, 0, MLA_HEADS * MLA_V, False)],
                  ((HG_HEADS, HG_DV), None))
    xs = _ffn(xs, row(norm_ffn_w[0]), ffn_w_up[0].astype(BF16), ffn_w_down[0].astype(BF16),
              row(final_norm_w), final_norm=False)

    w_in = odd_w_in[0].astype(BF16)
    n_ret = 2 * RET_HEADS * RET_DK + 2 * RET_HEADS * RET_DV
    n_qkv = GDN_HEADS * (2 * GDN_DK + GDN_DV)
    n_gate = 4 * GDN_HEADS
    w_main = jnp.concatenate([w_in[:, :n_ret + n_qkv], w_in[:, n_ret + n_qkv + n_gate:]], axis=1)
    w_side = jnp.pad(w_in[:, n_ret + n_qkv:n_ret + n_qkv + n_gate], ((0, 0), (0, LANES - n_gate)))
    h_main, h_gates = _norm_matmul(xs, row(norm_mix_w[1]), w_main, w_side, n=w_main.shape[1], tn=1024)

    r_f, r_b = _retention(h_main, cos_ret, sin_ret)
    qkv = _gdn_prep(h_main, gdn_conv_w[0].astype(F32))
    g_f, g_b = _gdn(qkv, h_gates, _pad_lanes(gdn_a_log[0]), _pad_lanes(gdn_dt_bias[0]))

    wr, wg = RET_HEADS * RET_DV, GDN_HEADS * GDN_DV
    xs = _mix_out(xs, odd_w_out[0].astype(BF16),
                  [(r_f, 0, wr, False), (r_b, 0, wr, False), (h_main, 2, wr, False), (row(ret_norm_w[0]), 0, wr, True),
                   (g_f, 0, wg, False), (g_b, 0, wg, False), (h_main, 6, wg, False), (row(gdn_norm_w[0]), 0, wg, True)],
                  ((RET_HEADS, RET_DV), (GDN_HEADS, GDN_DV)))
    xs = _ffn(xs, row(norm_ffn_w[1]), ffn_w_up[1].astype(BF16), ffn_w_down[1].astype(BF16),
              row(final_norm_w), final_norm=True)
    return xs.reshape(b, s, d)
```

```python
import functools
import math

import jax
import jax.numpy as jnp
from jax import lax
from jax.experimental import pallas as pl
from jax.experimental.pallas import tpu as pltpu

F32 = jnp.float32
BF16 = jnp.bfloat16

EPS = 1e-6
CHUNK = 64
ROPE_THETA = 10000.0
LANES = 128
SUBLANES = 8

HG_HEADS, HG_DK, HG_DV = 8, 128, 128
MLA_HEADS, MLA_NOPE, MLA_ROPE, MLA_V = 8, 128, 64, 128
MLA_Q_RANK, MLA_KV_RANK = 512, 256
MLA_QK_PAD = 256
RET_HEADS, RET_DK, RET_DV = 4, 128, 256
GDN_HEADS, GDN_DK, GDN_DV = 8, 128, 128
CONV_WIDTH = 5

NT_DIMS = (((1,), (1,)), ((), ()))
TN_DIMS = (((0,), (0,)), ((), ()))


def _params(semantics, vmem_mib):
    return pltpu.CompilerParams(dimension_semantics=semantics, vmem_limit_bytes=vmem_mib * 1024 * 1024)


def _rms_rows(x, w):
    ms = jnp.mean(x * x, axis=-1, keepdims=True)
    return x * lax.rsqrt(ms + EPS) * w


def _silu(x):
    return x * jax.nn.sigmoid(x)


def _bdot(a, b):
    return jnp.dot(a.astype(BF16), b.astype(BF16), preferred_element_type=F32)


def _bdot_nt(a, b):
    return lax.dot_general(a.astype(BF16), b.astype(BF16), NT_DIMS, preferred_element_type=F32)


def _bdot_tn(a, b):
    return lax.dot_general(a.astype(BF16), b.astype(BF16), TN_DIMS, preferred_element_type=F32)


def _rhs_split(b):
    hi = b.astype(BF16)
    lo = (b - hi.astype(F32)).astype(BF16)
    return jnp.concatenate([hi, hi, lo], axis=0)


def _rhs_split3(b):
    b1 = b.astype(BF16)
    r = b - b1.astype(F32)
    b2 = r.astype(BF16)
    b3 = (r - b2.astype(F32)).astype(BF16)
    return jnp.concatenate([b1, b2, b3], axis=0)


def _sdot(lhs_split, rhs_split):
    return jnp.dot(lhs_split, rhs_split, preferred_element_type=F32)


def _tri3(rev):
    row = lax.broadcasted_iota(jnp.int32, (CHUNK, 3 * CHUNK), 0)
    col = lax.broadcasted_iota(jnp.int32, (CHUNK, 3 * CHUNK), 1) & (CHUNK - 1)
    return jnp.where(row <= col if rev else row >= col, 1.0, 0.0).astype(BF16)


def _rope_table_kernel(pos_ref, inv_ref, cmul_ref, smul_ref, cos_ref, sin_ref):
    ang = pos_ref[...].astype(F32) * inv_ref[...]
    cos_ref[...] = jnp.cos(ang) * cmul_ref[...]
    sin_ref[...] = jnp.sin(ang) * smul_ref[...]


def _rope_tables(pos_col, inv_row, cmul_row, smul_row):
    s = pos_col.shape[0]
    tm = min(512, s)
    row = pl.BlockSpec((1, LANES), lambda i: (0, 0))
    tab = pl.BlockSpec((tm, LANES), lambda i: (i, 0))
    return pl.pallas_call(
        _rope_table_kernel, grid=(s // tm,),
        in_specs=[pl.BlockSpec((tm, 1), lambda i: (i, 0)), row, row, row],
        out_specs=[tab, tab],
        out_shape=[jax.ShapeDtypeStruct((s, LANES), F32)] * 2,
        compiler_params=_params(("parallel",), 16), name="rope_tables",
    )(pos_col, inv_row, cmul_row, smul_row)


NORM_ROWS = 128


def _norm_rows_to(x_ref, nw_ref, xn_ref, tm):
    def body(r, carry):
        rows = pl.ds(pl.multiple_of(r * NORM_ROWS, NORM_ROWS), NORM_ROWS)
        xn_ref[rows, :] = _rms_rows(x_ref[rows, :], nw_ref[...]).astype(BF16)
        return carry
    lax.fori_loop(0, tm // NORM_ROWS, body, 0)


def _norm_mm_kernel(*refs, tm, n_a):
    x_ref, nw_ref, wa_ref = refs[:3]
    wb_ref = refs[3] if len(refs) == 8 else None
    ws_ref, o_ref, os_ref, xn_ref = refs[-4:]
    j = pl.program_id(1)

    @pl.when(j == 0)
    def _():
        _norm_rows_to(x_ref, nw_ref, xn_ref, tm)
        os_ref[...] = jnp.dot(xn_ref[...], ws_ref[...], preferred_element_type=F32)

    def project(w_ref):
        o_ref[...] = jnp.dot(xn_ref[...], w_ref[0].astype(BF16), preferred_element_type=F32)

    if wb_ref is None:
        project(wa_ref)
    else:
        pl.when(j < n_a)(lambda: project(wa_ref))
        pl.when(j >= n_a)(lambda: project(wb_ref))


def _norm_matmul(x, nw, w_a, n_a_cols, w_b, w_side, *, tm=1024, tn=512):
    s, d = x.shape
    ns = w_side.shape[1]
    tm = min(tm, s)
    n_a = n_a_cols // tn
    n = n_a_cols + (0 if w_b is None else w_b.shape[2])
    w_specs = [pl.BlockSpec((1, d, tn), lambda i, j: (0, 0, jnp.minimum(j, n_a - 1)))]
    weights = [w_a]
    if w_b is not None:
        w_specs.append(pl.BlockSpec((1, d, tn), lambda i, j: (0, 0, jnp.maximum(j - n_a, 0))))
        weights.append(w_b)
    return pl.pallas_call(
        functools.partial(_norm_mm_kernel, tm=tm, n_a=n_a), grid=(s // tm, n // tn),
        in_specs=[pl.BlockSpec((tm, d), lambda i, j: (i, 0)),
                  pl.BlockSpec((1, d), lambda i, j: (0, 0)),
                  *w_specs,
                  pl.BlockSpec((d, ns), lambda i, j: (0, 0))],
        out_specs=[pl.BlockSpec((tm, tn), lambda i, j: (i, j)),
                   pl.BlockSpec((tm, ns), lambda i, j: (i, 0))],
        out_shape=[jax.ShapeDtypeStruct((s, n), F32), jax.ShapeDtypeStruct((s, ns), F32)],
        scratch_shapes=[pltpu.VMEM((tm, d), BF16)],
        compiler_params=_params(("parallel", "arbitrary"), 56), name="norm_matmul",
    )(x, nw, *weights, w_side)


def _gla_chunks(ps, st_ref):
    for p in ps:
        p["qe"] = p["q"] * jnp.exp(p["b"] - p["b_mid"])
        p["ke"] = p["k"] * jnp.exp(p["b_mid"] - p["b"])
        p["scores"] = jnp.where(p["mask"], _bdot_nt(p["qe"], p["ke"]), 0.0)
    for p in ps:
        q_dec = p["qe"] * jnp.exp(p["b_mid"])
        p["o"] = _bdot(p["scores"], p["v"]) + _bdot_nt(q_dec, st_ref[p["slot"]])
    for p in ps:
        k_dec = p["ke"] * jnp.exp(p["b_last"] - p["b_mid"])
        st_ref[p["slot"]] = st_ref[p["slot"]] * jnp.exp(p["b_last"]) + _bdot_tn(p["v"], k_dec)
        p["o_ref"][p["rows"], p["cols"]] = p["o"]


def _chunk_masks():
    row = lax.broadcasted_iota(jnp.int32, (CHUNK, CHUNK), 0)
    col = lax.broadcasted_iota(jnp.int32, (CHUNK, CHUNK), 1)
    return row, col


def _hgrn_kernel(logit_ref, qf_ref, ff_ref, vf_ref, qb_ref, fb_ref, vb_ref, of_ref, ob_ref, st_ref,
                 *, tb, layer):
    @pl.when(pl.program_id(0) == 0)
    def _():
        st_ref[...] = jnp.zeros_like(st_ref)

    lg = logit_ref[...]
    e = jnp.exp(lg - jnp.max(lg, axis=0, keepdims=True))
    lb = jnp.sum(e[0:layer + 1], axis=0, keepdims=True) / jnp.sum(e, axis=0, keepdims=True)
    row, col = _chunk_masks()
    causal, anti = row >= col, row <= col
    tri3_f, tri3_b = _tri3(False), _tri3(True)
    n = tb // CHUNK
    mid = CHUNK // 2
    scale = HG_DK ** -0.5

    def problem(q_ref, f_ref, v_ref, o_ref, rows, h, rev):
        cs = slice(h * HG_DK, (h + 1) * HG_DK)
        lbh = lb[:, cs]
        sig = jax.nn.sigmoid(f_ref[rows, cs])
        f = lbh + (1.0 - lbh) * sig
        b = _sdot(tri3_b if rev else tri3_f, _rhs_split3(jnp.log(f)))
        if rev:
            b_last, b_mid = b[0:1], b[CHUNK - 1 - mid:CHUNK - mid]
        else:
            b_last, b_mid = b[CHUNK - 1:CHUNK], b[mid:mid + 1]
        return dict(q=_silu(q_ref[rows, cs]) * scale, k=(1.0 - lbh) * (1.0 - sig), v=v_ref[rows, cs],
                    b=b, b_mid=b_mid, b_last=b_last, mask=anti if rev else causal,
                    slot=(HG_HEADS if rev else 0) + h, o_ref=o_ref, rows=rows, cols=cs)

    def body(c, carry):
        rf = pl.ds(pl.multiple_of(c * CHUNK, CHUNK), CHUNK)
        rb = pl.ds(pl.multiple_of((n - 1 - c) * CHUNK, CHUNK), CHUNK)
        ps = [problem(qf_ref, ff_ref, vf_ref, of_ref, rf, h, False) for h in range(HG_HEADS)]
        ps += [problem(qb_ref, fb_ref, vb_ref, ob_ref, rb, h, True) for h in range(HG_HEADS)]
        _gla_chunks(ps, st_ref)
        return carry

    lax.fori_loop(0, n, body, 0)


def _hgrn2(h_main, lb_logits, *, layer, tb=256):
    s = h_main.shape[0]
    tb = min(tb, s)
    nb = s // tb
    w = HG_HEADS * HG_DK
    fwd = lambda cb: pl.BlockSpec((tb, w), lambda i: (i, cb))
    bwd = lambda cb: pl.BlockSpec((tb, w), lambda i: (nb - 1 - i, cb))
    return pl.pallas_call(
        functools.partial(_hgrn_kernel, tb=tb, layer=layer), grid=(nb,),
        in_specs=[pl.BlockSpec(lb_logits.shape, lambda i: (0, 0)),
                  fwd(0), fwd(1), fwd(3), bwd(0), bwd(2), bwd(3)],
        out_specs=[fwd(0), bwd(0)],
        out_shape=[jax.ShapeDtypeStruct((s, w), F32)] * 2,
        scratch_shapes=[pltpu.VMEM((2 * HG_HEADS, HG_DV, HG_DK), F32)],
        compiler_params=_params(("arbitrary",), 40), name="hgrn2_scan",
    )(lb_logits, h_main, h_main, h_main, h_main, h_main, h_main)


def _ret_kernel(qf_ref, kf_ref, vf_ref, cf_ref, sf_ref, qb_ref, kb_ref, vb_ref, cb_ref, sb_ref,
                of_ref, ob_ref, st_ref, *, tb):
    @pl.when(pl.program_id(0) == 0)
    def _():
        st_ref[...] = jnp.zeros_like(st_ref)

    row, col = _chunk_masks()
    causal, anti = row >= col, row <= col
    n = tb // CHUNK
    mid = CHUNK // 2
    scale = RET_DK ** -0.5
    t_idx = lax.broadcasted_iota(jnp.int32, (CHUNK, RET_DK), 0).astype(F32)
    log_gamma = [math.log1p(-2.0 ** (-5 - h)) for h in range(RET_HEADS)]

    def rope(x, cos, sin):
        return x * cos + pltpu.roll(x, RET_DK // 2, 1) * sin

    def problem(q_ref, k_ref, v_ref, c_ref, s_ref, o_ref, rows, h, rev):
        cs = slice(h * RET_DK, (h + 1) * RET_DK)
        vs = slice(h * RET_DV, (h + 1) * RET_DV)
        cos, sin = c_ref[rows, :], s_ref[rows, :]
        lgam = log_gamma[RET_HEADS - 1 - h] if rev else log_gamma[h]
        ones = jnp.ones((1, RET_DK), F32)
        return dict(q=rope(q_ref[rows, cs], cos, sin) * scale, k=rope(k_ref[rows, cs], cos, sin),
                    v=v_ref[rows, vs], b=(CHUNK - t_idx) * lgam if rev else (t_idx + 1.0) * lgam,
                    b_mid=ones * ((mid + 1) * lgam), b_last=ones * (CHUNK * lgam),
                    mask=anti if rev else causal, slot=(RET_HEADS if rev else 0) + h,
                    o_ref=o_ref, rows=rows, cols=vs)

    def body(c, carry):
        rf = pl.ds(pl.multiple_of(c * CHUNK, CHUNK), CHUNK)
        rb = pl.ds(pl.multiple_of((n - 1 - c) * CHUNK, CHUNK), CHUNK)
        ps = [problem(qf_ref, kf_ref, vf_ref, cf_ref, sf_ref, of_ref, rf, h, False) for h in range(RET_HEADS)]
        ps += [problem(qb_ref, kb_ref, vb_ref, cb_ref, sb_ref, ob_ref, rb, h, True) for h in range(RET_HEADS)]
        _gla_chunks(ps, st_ref)
        return carry

    lax.fori_loop(0, n, body, 0)


def _retention(h_main, cos_tab, sin_tab, *, tb=256):
    s = h_main.shape[0]
    tb = min(tb, s)
    nb = s // tb
    wk, wv = RET_HEADS * RET_DK, RET_HEADS * RET_DV
    fwd = lambda w, cb: pl.BlockSpec((tb, w), lambda i: (i, cb))
    bwd = lambda w, cb: pl.BlockSpec((tb, w), lambda i: (nb - 1 - i, cb))
    return pl.pallas_call(
        functools.partial(_ret_kernel, tb=tb), grid=(nb,),
        in_specs=[fwd(wk, 0), fwd(wk, 1), fwd(wv, 1), fwd(LANES, 0), fwd(LANES, 0),
                  bwd(wk, 0), bwd(wk, 1), bwd(wv, 1), bwd(LANES, 0), bwd(LANES, 0)],
        out_specs=[fwd(wv, 0), bwd(wv, 0)],
        out_shape=[jax.ShapeDtypeStruct((s, wv), F32)] * 2,
        scratch_shapes=[pltpu.VMEM((2 * RET_HEADS, RET_DV, RET_DK), F32)],
        compiler_params=_params(("arbitrary",), 40), name="retention_scan",
    )(h_main, h_main, h_main, cos_tab, sin_tab, h_main, h_main, h_main, cos_tab, sin_tab)


def _gated_head_norm(of_ref, ob_ref, gate_ref, nw_ref, rows, heads, hd):
    o = of_ref[rows, :] + ob_ref[rows, :]
    parts = []
    for h in range(heads):
        y = o[:, h * hd:(h + 1) * hd]
        parts.append(y * lax.rsqrt(jnp.mean(y * y, axis=-1, keepdims=True) + EPS))
    y = jnp.concatenate(parts, axis=-1)
    return y * nw_ref[...] * _silu(gate_ref[rows, :])


def _mixout_kernel(*refs, tm, groups):
    x_ref, w_ref = refs[0], refs[1]
    out_ref, lhs_ref = refs[-2], refs[-1]
    grefs = refs[2:-2]

    def body(r, carry):
        rows = pl.ds(pl.multiple_of(r * NORM_ROWS, NORM_ROWS), NORM_ROWS)
        pos, col = 0, 0
        for g in groups:
            if g is None:
                val, width = grefs[pos][rows, :], grefs[pos].shape[1]
                pos += 1
            else:
                heads, hd = g
                val, width = _gated_head_norm(*grefs[pos:pos + 4], rows, heads, hd), heads * hd
                pos += 4
            lhs_ref[rows, col:col + width] = val.astype(BF16)
            col += width
        return carry
    lax.fori_loop(0, tm // NORM_ROWS, body, 0)

    out_ref[...] = x_ref[...] + jnp.dot(lhs_ref[...], w_ref[...], preferred_element_type=F32)


def _mix_out(x, w_out, group_args, groups, *, tm=256):
    s, d = x.shape
    tm = min(tm, s)
    specs = [pl.BlockSpec((tm, d), lambda i: (i, 0)),
             pl.BlockSpec(w_out.shape, lambda i: (0, 0), pipeline_mode=pl.Buffered(1))]
    arrays = [x, w_out]
    for arr, cb, width, rowvec in group_args:
        arrays.append(arr)
        if rowvec:
            specs.append(pl.BlockSpec((1, width), lambda i: (0, 0)))
        else:
            specs.append(pl.BlockSpec((tm, width), lambda i, cb=cb: (i, cb)))
    return pl.pallas_call(
        functools.partial(_mixout_kernel, tm=tm, groups=groups), grid=(s // tm,),
        in_specs=specs,
        out_specs=pl.BlockSpec((tm, d), lambda i: (i, 0)),
        out_shape=jax.ShapeDtypeStruct((s, d), F32),
        scratch_shapes=[pltpu.VMEM((tm, w_out.shape[0]), BF16)],
        compiler_params=_params(("parallel",), 40), name="mix_out",
    )(*arrays)


def _rot_pair(pr, cos, sin):
    return pr * cos + pltpu.roll(pr, MLA_ROPE, 1) * sin


def _mla_proj_kernel(cq_ref, ckv_ref, kpe_ref, qnw_ref, kvnw_ref, wq_ref, wkv_ref, cos_ref, sin_ref,
                     q_ref, k_ref, v_ref, *, scale):
    cos, sin = cos_ref[...], sin_ref[...]
    cqn = _rms_rows(cq_ref[...], qnw_ref[...]).astype(BF16)
    ckvn = _rms_rows(ckv_ref[...], kvnw_ref[...]).astype(BF16)
    pe = _rot_pair(kpe_ref[...], cos, sin).astype(BF16)
    lane = lax.broadcasted_iota(jnp.int32, pe.shape, 1)
    ones_col = jnp.where(lane == 0, 1.0, 0.0).astype(BF16)
    for h in range(MLA_HEADS):
        cols = slice(h * MLA_QK_PAD, (h + 1) * MLA_QK_PAD)
        yq = jnp.dot(cqn, wq_ref[:, cols], preferred_element_type=F32)
        q_rope = _rot_pair(yq[:, MLA_NOPE:], cos, sin)
        q_ref[h] = (jnp.concatenate([yq[:, :MLA_NOPE], q_rope], axis=1) * scale).astype(BF16)
        ykv = jnp.dot(ckvn, wkv_ref[:, cols], preferred_element_type=F32)
        k_ref[h] = jnp.concatenate([ykv[:, :MLA_NOPE].astype(BF16), pe], axis=1)
        v_ref[h] = jnp.concatenate([ykv[:, MLA_NOPE:].astype(BF16), ones_col], axis=1)


def _mla_proj(h_main, h_side, q_norm_w, kv_norm_w, wq, wkv, cos_tab, sin_tab, *, tm=512):
    s = h_main.shape[0]
    tm = min(tm, s)
    cq_block = (HG_HEADS * (3 * HG_DK + 2 * HG_DV)) // MLA_Q_RANK
    scale = (MLA_NOPE + MLA_ROPE) ** -0.5 * math.log2(math.e)
    full = lambda a: pl.BlockSpec(a.shape, lambda i: (0, 0))
    tab = pl.BlockSpec((tm, LANES), lambda i: (i, 0))
    head_out = pl.BlockSpec((MLA_HEADS, tm, MLA_QK_PAD), lambda i: (0, i, 0))
    return pl.pallas_call(
        functools.partial(_mla_proj_kernel, scale=scale), grid=(s // tm,),
        in_specs=[pl.BlockSpec((tm, MLA_Q_RANK), lambda i: (i, cq_block)),
                  pl.BlockSpec((tm, MLA_KV_RANK), lambda i: (i, 0)),
                  pl.BlockSpec((tm, LANES), lambda i: (i, MLA_KV_RANK // LANES)),
                  full(q_norm_w), full(kv_norm_w), full(wq), full(wkv), tab, tab],
        out_specs=[head_out, head_out, head_out],
        out_shape=[jax.ShapeDtypeStruct((MLA_HEADS, s, MLA_QK_PAD), BF16)] * 3,
        compiler_params=_params(("parallel",), 40), name="mla_proj",
    )(h_main, h_side, h_side, q_norm_w, kv_norm_w, wq, wkv, cos_tab, sin_tab)


ATTN_SPLIT = 2
ATTN_ROWS = 32


def _attn_kernel(q_ref, k_ref, v_ref, o_ref, s_ref, p_ref, m_ref, alpha_ref, acc_ref, *, tk, nk):
    tq = q_ref.shape[1]
    sub = tq // ATTN_SPLIT
    m_ref[...] = jnp.full(m_ref.shape, -jnp.inf, F32)
    acc_ref[...] = jnp.zeros(acc_ref.shape, F32)

    def softmax_rows(i):
        for rb in range(sub // ATTN_ROWS):
            loc = slice(rb * ATTN_ROWS, (rb + 1) * ATTN_ROWS)
            glob = slice(i * sub + rb * ATTN_ROWS, i * sub + (rb + 1) * ATTN_ROWS)
            sc = s_ref[i, loc, :]
            cols = [sc[:, c * LANES:(c + 1) * LANES] for c in range(tk // LANES)]
            col_max = functools.reduce(jnp.maximum, cols)
            m_old = m_ref[glob, :]
            m_new = jnp.maximum(m_old, jnp.max(col_max, axis=-1, keepdims=True))
            m_ref[glob, :] = m_new
            alpha_ref[glob, :] = jnp.exp2(m_old - m_new)
            p_ref[i, loc, :] = jnp.concatenate([jnp.exp2(c - m_new) for c in cols], axis=1).astype(BF16)

    def body(j, carry):
        rows = pl.ds(pl.multiple_of(j * tk, tk), tk)
        k, v = k_ref[0, rows, :], v_ref[0, rows, :]
        for i in range(ATTN_SPLIT):
            s_ref[i] = lax.dot_general(q_ref[0, i * sub:(i + 1) * sub, :], k, NT_DIMS, preferred_element_type=F32)
        for i in range(ATTN_SPLIT):
            softmax_rows(i)
            rs = slice(i * sub, (i + 1) * sub)
            alpha = alpha_ref[rs, :]
            acc_ref[rs, :] = (jnp.concatenate([alpha, alpha], axis=1) * acc_ref[rs, :]
                              + jnp.dot(p_ref[i], v, preferred_element_type=F32))
        return carry

    lax.fori_loop(0, nk, body, 0)
    acc = acc_ref[...]
    o_ref[...] = (acc[:, :MLA_V] / acc[:, MLA_V:MLA_V + 1]).astype(BF16)


def _attention(q, k, v, *, tq=1024, tk=2048):
    heads, s, _ = q.shape
    tq, tk = min(tq, s), min(tk, s)
    sub = tq // ATTN_SPLIT
    return pl.pallas_call(
        functools.partial(_attn_kernel, tk=tk, nk=s // tk), grid=(heads, s // tq),
        in_specs=[pl.BlockSpec((1, tq, MLA_QK_PAD), lambda h, i: (h, i, 0)),
                  pl.BlockSpec((1, s, MLA_QK_PAD), lambda h, i: (h, 0, 0)),
                  pl.BlockSpec((1, s, 2 * MLA_V), lambda h, i: (h, 0, 0))],
        out_specs=pl.BlockSpec((tq, MLA_V), lambda h, i: (i, h)),
        out_shape=jax.ShapeDtypeStruct((s, heads * MLA_V), BF16),
        scratch_shapes=[pltpu.VMEM((ATTN_SPLIT, sub, tk), F32), pltpu.VMEM((ATTN_SPLIT, sub, tk), BF16),
                        pltpu.VMEM((tq, LANES), F32), pltpu.VMEM((tq, LANES), F32),
                        pltpu.VMEM((tq, 2 * MLA_V), F32)],
        compiler_params=_params(("parallel", "arbitrary"), 48), name="mla_attention",
    )(q, k, v)


def _ffn_kernel(x_ref, nw_ref, wu_ref, wd_ref, fw_ref, o_ref, xn_ref, *, tm, nf, final_norm):
    f = pl.program_id(1)

    @pl.when(f == 0)
    def _():
        _norm_rows_to(x_ref, nw_ref, xn_ref, tm)
        o_ref[...] = x_ref[...]

    a = jnp.maximum(jnp.dot(xn_ref[...], wu_ref[...], preferred_element_type=F32), 0.0)
    o_ref[...] += jnp.dot((a * a).astype(BF16), wd_ref[...], preferred_element_type=F32)

    if final_norm:
        @pl.when(f == nf - 1)
        def _():
            def body(r, carry):
                rows = pl.ds(pl.multiple_of(r * NORM_ROWS, NORM_ROWS), NORM_ROWS)
                o_ref[rows, :] = _rms_rows(o_ref[rows, :], fw_ref[...])
                return carry
            lax.fori_loop(0, tm // NORM_ROWS, body, 0)


def _ffn(x, nw, w_up, w_down, final_w, *, final_norm, tm=512, tf=1024):
    s, d = x.shape
    dff = w_up.shape[1]
    tm = min(tm, s)
    nf = dff // tf
    return pl.pallas_call(
        functools.partial(_ffn_kernel, tm=tm, nf=nf, final_norm=final_norm), grid=(s // tm, nf),
        in_specs=[pl.BlockSpec((tm, d), lambda i, f: (i, 0)),
                  pl.BlockSpec((1, d), lambda i, f: (0, 0)),
                  pl.BlockSpec((d, tf), lambda i, f: (0, f)),
                  pl.BlockSpec((tf, d), lambda i, f: (f, 0)),
                  pl.BlockSpec((1, d), lambda i, f: (0, 0))],
        out_specs=pl.BlockSpec((tm, d), lambda i, f: (i, 0)),
        out_shape=jax.ShapeDtypeStruct((s, d), F32),
        scratch_shapes=[pltpu.VMEM((tm, d), BF16)],
        compiler_params=_params(("parallel", "arbitrary"), 48), name="ffn",
    )(x, nw, w_up, w_down, final_w)


def _gdn_prep_kernel(x_ref, xp_ref, xn_ref, cw_ref, o_ref, *, tm, nblk):
    i, sec = pl.program_id(0), pl.program_id(1)
    x = x_ref[...]
    prev = jnp.where(i > 0, xp_ref[...], 0.0)
    nxt = jnp.where(i < nblk - 1, xn_ref[...], 0.0)
    rows = lax.broadcasted_iota(jnp.int32, x.shape, 0)
    half = CONV_WIDTH // 2

    def shifted(d):
        if d == 0:
            return x
        r = pltpu.roll(x, (-d) % tm, 0)
        if d < 0:
            for t in range(-d):
                r = jnp.where(rows == t, prev[SUBLANES + t + d:SUBLANES + t + d + 1], r)
        else:
            for t in range(tm - d, tm):
                r = jnp.where(rows == t, nxt[t + d - tm:t + d - tm + 1], r)
        return r

    y = shifted(-half) * cw_ref[0:1]
    for j in range(1, CONV_WIDTH):
        y = y + shifted(j - half) * cw_ref[j:j + 1]
    y = _silu(y)
    parts = []
    for h in range(GDN_HEADS):
        a = y[:, h * GDN_DK:(h + 1) * GDN_DK]
        parts.append(a * lax.rsqrt(jnp.sum(a * a, axis=-1, keepdims=True) + EPS))
    nrm = jnp.concatenate(parts, axis=-1) * jnp.where(sec == 0, GDN_DK ** -0.5, 1.0)
    o_ref[...] = jnp.where(sec == 2, y, nrm)


def _gdn_prep(h_main, conv_w, *, tm=256):
    s = h_main.shape[0]
    tm = min(tm, s)
    nblk = s // tm
    w = GDN_HEADS * GDN_DK
    base = (2 * RET_HEADS * RET_DK + 2 * RET_HEADS * RET_DV) // w
    per = tm // SUBLANES
    return pl.pallas_call(
        functools.partial(_gdn_prep_kernel, tm=tm, nblk=nblk), grid=(nblk, 3),
        in_specs=[pl.BlockSpec((tm, w), lambda i, c: (i, base + c)),
                  pl.BlockSpec((SUBLANES, w), lambda i, c: (jnp.maximum(i * per - 1, 0), base + c)),
                  pl.BlockSpec((SUBLANES, w), lambda i, c: (jnp.minimum((i + 1) * per, s // SUBLANES - 1), base + c)),
                  pl.BlockSpec((CONV_WIDTH, w), lambda i, c: (0, c))],
        out_specs=pl.BlockSpec((tm, w), lambda i, c: (i, c)),
        out_shape=jax.ShapeDtypeStruct((s, 3 * w), F32),
        compiler_params=_params(("parallel", "arbitrary"), 32), name="gdn_prep",
    )(h_main, h_main, h_main, conv_w)


def _softplus(x):
    return jnp.maximum(x, 0.0) + jnp.log1p(jnp.exp(-jnp.abs(x)))


def _pair_lhs(x):
    hi = x.astype(BF16)
    lo = (x - hi.astype(F32)).astype(BF16)
    return jnp.concatenate([hi, lo, hi], axis=1)


def _pair_rhs(y, low_lanes):
    hi = y.astype(BF16)
    lo = (y - hi.astype(F32)).astype(BF16)
    zero = jnp.zeros_like(hi)

    def bd(a):
        return jnp.concatenate([jnp.where(low_lanes, a, zero), jnp.where(low_lanes, zero, a)], axis=0)

    bd_hi = bd(hi)
    return jnp.concatenate([bd_hi, bd_hi, bd(lo)], axis=0)


def _pair_rows(x, low_lanes):
    return jnp.concatenate([jnp.where(low_lanes, x, 0.0), jnp.where(low_lanes, 0.0, x)], axis=0)


def _gdn_kernel(alog_ref, dt_ref, qf_ref, kf_ref, vf_ref, gf_ref, qb_ref, kb_ref, vb_ref, gb_ref,
                of_ref, ob_ref, st_ref, *, tb):
    @pl.when(pl.program_id(0) == 0)
    def _():
        st_ref[...] = jnp.zeros_like(st_ref)

    row = lax.broadcasted_iota(jnp.int32, (CHUNK, LANES), 0)
    lane = lax.broadcasted_iota(jnp.int32, (CHUNK, LANES), 1)
    col = lane & (CHUNK - 1)
    low_lanes = lane < CHUNK
    low_row = low_lanes[0:1]
    causal, anti = row >= col, row <= col
    strict_c, strict_a = row > col, row < col
    tri3_f, tri3_b = _tri3(False), _tri3(True)
    eye = (row == col).astype(F32)
    same16 = (row >> 4) == (col >> 4)
    same32 = (row >> 5) == (col >> 5)
    in32 = jnp.logical_and(same32, jnp.logical_not(same16))
    n = tb // CHUNK
    neg_a = -jnp.exp(alog_ref[...])
    dt = dt_ref[...]

    def pairs(q_ref, k_ref, v_ref, g_ref, o_ref, rows, rev):
        raw = g_ref[rows, :]
        la = neg_a * _softplus(raw + dt)
        beta = jax.nn.sigmoid(raw)
        g = _sdot(tri3_b if rev else tri3_f, _rhs_split3(la))
        g_rows = jnp.concatenate([g, g], axis=0).T
        la0 = GDN_HEADS if rev else 0
        b0 = (3 if rev else 2) * GDN_HEADS
        last = 0 if rev else CHUNK - 1
        out = []
        for j in range(GDN_HEADS // 2):
            hs = (2 * j, 2 * j + 1)
            g_cols = [jnp.broadcast_to(g[:, la0 + h:la0 + h + 1], (CHUNK, LANES)) for h in hs]
            b_cols = [jnp.broadcast_to(beta[:, b0 + h:b0 + h + 1], (CHUNK, LANES)) for h in hs]
            g_col = jnp.where(low_lanes, g_cols[0], g_cols[1])
            g_row = jnp.where(low_row, g_rows[la0 + hs[0]:la0 + hs[0] + 1], g_rows[la0 + hs[1]:la0 + hs[1] + 1])
            out.append(dict(
                q_ref=q_ref, k_ref=k_ref, v_ref=v_ref, o_ref=o_ref, rows=rows,
                cs=[slice(h * GDN_DK, (h + 1) * GDN_DK) for h in hs],
                slots=[(GDN_HEADS if rev else 0) + h for h in hs],
                strict=strict_a if rev else strict_c,
                decay=jnp.exp(jnp.where(anti if rev else causal, g_col - g_row, -jnp.inf)),
                g_cols=g_cols, b_cols=b_cols,
                g_last=[g[last:last + 1, la0 + h:la0 + h + 1] for h in hs]))
        return out

    def body(c, carry):
        rf = pl.ds(pl.multiple_of(c * CHUNK, CHUNK), CHUNK)
        rb = pl.ds(pl.multiple_of((n - 1 - c) * CHUNK, CHUNK), CHUNK)
        ps = (pairs(qf_ref, kf_ref, vf_ref, gf_ref, of_ref, rf, False)
              + pairs(qb_ref, kb_ref, vb_ref, gb_ref, ob_ref, rb, True))
        for p in ps:
            ks = [p["k_ref"][p["rows"], cs] for cs in p["cs"]]
            qs = [p["q_ref"][p["rows"], cs] for cs in p["cs"]]
            p["kb"] = [k * b for k, b in zip(ks, p["b_cols"])]
            lhs = jnp.concatenate([jnp.concatenate([kb, q], axis=0) for kb, q in zip(p["kb"], qs)], axis=1)
            zero = jnp.zeros_like(ks[0])
            k_bd = jnp.concatenate([jnp.concatenate([ks[0], zero], axis=1),
                                    jnp.concatenate([zero, ks[1]], axis=1)], axis=0)
            kq = _bdot_nt(lhs, k_bd)
            a = jnp.where(p["strict"], kq[:CHUNK] * p["decay"], 0.0)
            p["qk"] = kq[CHUNK:] * p["decay"]
            n1 = -jnp.where(same16, a, 0.0)
            p["n_l"], p["n_r"], p["t"] = _pair_lhs(n1), _pair_rhs(n1, low_lanes), eye + n1
            p["e32"] = _pair_rhs(jnp.where(in32, a, 0.0), low_lanes)
            p["e64"] = _pair_rhs(jnp.where(same32, 0.0, a), low_lanes)
        for step in range(3):
            for p in ps:
                nn = _sdot(p["n_l"], p["n_r"])
                p["n_r"] = _pair_rhs(nn, low_lanes)
                if step < 2:
                    p["n_l"] = _pair_lhs(nn)
            for p in ps:
                p["t"] = p["t"] + _sdot(_pair_lhs(p["t"]), p["n_r"])
        for e in ("e32", "e64"):
            for p in ps:
                p["x"] = _sdot(_pair_lhs(p["t"]), p[e])
            for p in ps:
                p["t"] = p["t"] - _sdot(_pair_lhs(p["x"]), _pair_rhs(p["t"], low_lanes))
        for p in ps:
            rhs = []
            for cs, kb, b, g in zip(p["cs"], p["kb"], p["b_cols"], p["g_cols"]):
                rhs.append(jnp.concatenate([p["v_ref"][p["rows"], cs] * b, kb * jnp.exp(g)], axis=1))
            p["sol"] = _sdot(_pair_lhs(_pair_rows(p["t"], low_lanes)), _rhs_split(jnp.concatenate(rhs, axis=0)))
        for p in ps:
            p["v_new"], p["o"] = [], []
            for i, (cs, slot, g) in enumerate(zip(p["cs"], p["slots"], p["g_cols"])):
                sol = p["sol"][i * CHUNK:(i + 1) * CHUNK]
                q = p["q_ref"][p["rows"], cs]
                ws = _bdot(jnp.concatenate([sol[:, GDN_DV:], q * jnp.exp(g)], axis=0), st_ref[slot])
                p["v_new"].append(sol[:, :GDN_DV] - ws[:CHUNK])
                p["o"].append(ws[CHUNK:])
        for p in ps:
            o_intra = _bdot(_pair_rows(p["qk"], low_lanes), jnp.concatenate(p["v_new"], axis=0))
            for i, (cs, slot, g, g_last) in enumerate(zip(p["cs"], p["slots"], p["g_cols"], p["g_last"])):
                k = p["k_ref"][p["rows"], cs]
                p["o_ref"][p["rows"], cs] = p["o"][i] + o_intra[i * CHUNK:(i + 1) * CHUNK]
                st_ref[slot] = (st_ref[slot] * jnp.exp(g_last)
                                + _bdot_tn(k * jnp.exp(g_last - g), p["v_new"][i]))
        return carry

    lax.fori_loop(0, n, body, 0)


def _gdn(qkv, gates, alog_row, dt_row, *, tb=256):
    s = qkv.shape[0]
    tb = min(tb, s)
    nb = s // tb
    w = GDN_HEADS * GDN_DK
    fwd = lambda width, cb: pl.BlockSpec((tb, width), lambda i: (i, cb))
    bwd = lambda width, cb: pl.BlockSpec((tb, width), lambda i: (nb - 1 - i, cb))
    row = pl.BlockSpec((1, LANES), lambda i: (0, 0))
    return pl.pallas_call(
        functools.partial(_gdn_kernel, tb=tb), grid=(nb,),
        in_specs=[row, row,
                  fwd(w, 0), fwd(w, 1), fwd(w, 2), fwd(LANES, 0),
                  bwd(w, 0), bwd(w, 1), bwd(w, 2), bwd(LANES, 0)],
        out_specs=[fwd(w, 0), bwd(w, 0)],
        out_shape=[jax.ShapeDtypeStruct((s, w), F32)] * 2,
        scratch_shapes=[pltpu.VMEM((2 * GDN_HEADS, GDN_DK, GDN_DV), F32)],
        compiler_params=_params(("arbitrary",), 40), name="gdn_scan",
    )(alog_row, dt_row, qkv, qkv, qkv, gates, qkv, qkv, qkv, gates)


def _rot_cols(w_pe):
    half = w_pe.shape[-1] // 2
    return jnp.concatenate([-w_pe[..., half:], w_pe[..., :half]], axis=-1)


def _pad_lanes(v, fill=0.0):
    v = v.reshape(1, -1).astype(F32)
    return jnp.pad(v, ((0, 0), (0, LANES - v.shape[1])), constant_values=fill)


def kernel(x, positions, norm_mix_w, norm_ffn_w, final_norm_w, hg_lb_logits, even_w_in, hg_norm_w,
           mla_q_norm_w, mla_w_q_b, mla_kv_norm_w, mla_w_kv_b, even_w_out, odd_w_in, ret_norm_w,
           gdn_conv_w, gdn_a_log, gdn_dt_bias, gdn_norm_w, odd_w_out, ffn_w_up, ffn_w_down):
    b, s, d = x.shape
    assert b == 1 and s % CHUNK == 0
    xs = x.reshape(s, d)
    row = lambda v: v.reshape(1, -1).astype(F32)

    pos_col = positions.reshape(s, 1)
    inv_ret = ROPE_THETA ** (-jnp.arange(RET_DK // 2, dtype=F32) / (RET_DK // 2))
    inv_mla = ROPE_THETA ** (-jnp.arange(MLA_ROPE // 2, dtype=F32) / (MLA_ROPE // 2))
    ones64, zeros64 = jnp.ones((64,), F32), jnp.zeros((64,), F32)
    cos_ret, sin_ret = _rope_tables(pos_col, row(jnp.concatenate([inv_ret, inv_ret])),
                                    row(jnp.concatenate([ones64, ones64])),
                                    row(jnp.concatenate([-ones64, ones64])))
    cos_mla, sin_mla = _rope_tables(pos_col, row(jnp.concatenate([inv_mla, inv_mla, zeros64])),
                                    row(jnp.concatenate([ones64, zeros64])),
                                    row(jnp.concatenate([ones64, zeros64])))

    n_main = HG_HEADS * (3 * HG_DK + 2 * HG_DV) + MLA_Q_RANK
    w_kva = even_w_in[0, :, n_main:].astype(BF16)
    w_kpe = w_kva[:, MLA_KV_RANK:]
    w_side = jnp.concatenate([w_kva, _rot_cols(w_kpe)], axis=1)
    h_main, h_side = _norm_matmul(xs, row(norm_mix_w[0]), even_w_in[0:1], n_main, None, w_side)

    o_f, o_b = _hgrn2(h_main, hg_lb_logits.astype(F32), layer=0)

    wq = mla_w_q_b[0].reshape(MLA_Q_RANK, MLA_HEADS, MLA_NOPE + MLA_ROPE)
    wq_pe = wq[..., MLA_NOPE:]
    wq = jnp.concatenate([wq[..., :MLA_NOPE], wq_pe, _rot_cols(wq_pe)], axis=-1)
    wq = wq.reshape(MLA_Q_RANK, MLA_HEADS * MLA_QK_PAD).astype(BF16)
    q, k, v = _mla_proj(h_main, h_side, row(mla_q_norm_w[0]), row(mla_kv_norm_w[0]), wq,
                        mla_w_kv_b[0].astype(BF16), cos_mla, sin_mla)
    o_attn = _attention(q, k, v)

    wa = HG_HEADS * HG_DV
    xs = _mix_out(xs, even_w_out[0].astype(BF16),
                  [(o_f, 0, wa, False), (o_b, 0, wa, False), (h_main, 4, wa, False), (row(hg_norm_w[0]), 0, wa, True),
                   (o_attn, 0, MLA_HEADS * MLA_V, False)],
                  ((HG_HEADS, HG_DV), None))
    xs = _ffn(xs, row(norm_ffn_w[0]), ffn_w_up[0].astype(BF16), ffn_w_down[0].astype(BF16),
              row(final_norm_w), final_norm=False)

    n_ret = 2 * RET_HEADS * RET_DK + 2 * RET_HEADS * RET_DV
    n_qkv = GDN_HEADS * (2 * GDN_DK + GDN_DV)
    n_gate = 4 * GDN_HEADS
    w_gdn_gate = odd_w_in[0:1, :, n_ret + n_qkv + n_gate:]
    w_side = jnp.pad(odd_w_in[0, :, n_ret + n_qkv:n_ret + n_qkv + n_gate].astype(BF16),
                     ((0, 0), (0, LANES - n_gate)))
    h_main, h_gates = _norm_matmul(xs, row(norm_mix_w[1]), odd_w_in[0:1], n_ret + n_qkv, w_gdn_gate, w_side)

    r_f, r_b = _retention(h_main, cos_ret, sin_ret)
    qkv = _gdn_prep(h_main, gdn_conv_w[0].astype(F32))
    g_f, g_b = _gdn(qkv, h_gates, _pad_lanes(gdn_a_log[0]), _pad_lanes(gdn_dt_bias[0]))

    wr, wg = RET_HEADS * RET_DV, GDN_HEADS * GDN_DV
    xs = _mix_out(xs, odd_w_out[0].astype(BF16),
                  [(r_f, 0, wr, False), (r_b, 0, wr, False), (h_main, 2, wr, False), (row(ret_norm_w[0]), 0, wr, True),
                   (g_f, 0, wg, False), (g_b, 0, wg, False), (h_main, 6, wg, False), (row(gdn_norm_w[0]), 0, wg, True)],
                  ((RET_HEADS, RET_DV), (GDN_HEADS, GDN_DV)))
    xs = _ffn(xs, row(norm_ffn_w[1]), ffn_w_up[1].astype(BF16), ffn_w_down[1].astype(BF16),
              row(final_norm_w), final_norm=True)
    return xs.reshape(b, s, d)
```

```python
import functools
import math

import jax
import jax.numpy as jnp
from jax import lax
from jax.experimental import pallas as pl
from jax.experimental.pallas import tpu as pltpu

F32 = jnp.float32
BF16 = jnp.bfloat16

EPS = 1e-6
CHUNK = 64
ROPE_THETA = 10000.0
LANES = 128
SUBLANES = 8

HG_HEADS, HG_DK, HG_DV = 8, 128, 128
MLA_HEADS, MLA_NOPE, MLA_ROPE, MLA_V = 8, 128, 64, 128
MLA_Q_RANK, MLA_KV_RANK = 512, 256
MLA_QK_PAD = 256
RET_HEADS, RET_DK, RET_DV = 4, 128, 256
GDN_HEADS, GDN_DK, GDN_DV = 8, 128, 128
CONV_WIDTH = 5

NT_DIMS = (((1,), (1,)), ((), ()))
TN_DIMS = (((0,), (0,)), ((), ()))


def _params(semantics, vmem_mib):
    return pltpu.CompilerParams(dimension_semantics=semantics, vmem_limit_bytes=vmem_mib * 1024 * 1024)


def _rms_rows(x, w):
    ms = jnp.mean(x * x, axis=-1, keepdims=True)
    return x * lax.rsqrt(ms + EPS) * w


def _silu(x):
    return x * jax.nn.sigmoid(x)


def _bdot(a, b):
    return jnp.dot(a.astype(BF16), b.astype(BF16), preferred_element_type=F32)


def _bdot_nt(a, b):
    return lax.dot_general(a.astype(BF16), b.astype(BF16), NT_DIMS, preferred_element_type=F32)


def _bdot_tn(a, b):
    return lax.dot_general(a.astype(BF16), b.astype(BF16), TN_DIMS, preferred_element_type=F32)


def _rhs_split(b):
    hi = b.astype(BF16)
    lo = (b - hi.astype(F32)).astype(BF16)
    return jnp.concatenate([hi, hi, lo], axis=0)


def _rhs_split3(b):
    b1 = b.astype(BF16)
    r = b - b1.astype(F32)
    b2 = r.astype(BF16)
    b3 = (r - b2.astype(F32)).astype(BF16)
    return jnp.concatenate([b1, b2, b3], axis=0)


def _sdot(lhs_split, rhs_split):
    return jnp.dot(lhs_split, rhs_split, preferred_element_type=F32)


def _tri3(rev):
    row = lax.broadcasted_iota(jnp.int32, (CHUNK, 3 * CHUNK), 0)
    col = lax.broadcasted_iota(jnp.int32, (CHUNK, 3 * CHUNK), 1) & (CHUNK - 1)
    return jnp.where(row <= col if rev else row >= col, 1.0, 0.0).astype(BF16)


def _rope_table_kernel(pos_ref, inv_ref, cmul_ref, smul_ref, cos_ref, sin_ref):
    ang = pos_ref[...].astype(F32) * inv_ref[...]
    cos_ref[...] = jnp.cos(ang) * cmul_ref[...]
    sin_ref[...] = jnp.sin(ang) * smul_ref[...]


def _rope_tables(pos_col, inv_row, cmul_row, smul_row):
    s = pos_col.shape[0]
    tm = min(512, s)
    row = pl.BlockSpec((1, LANES), lambda i: (0, 0))
    tab = pl.BlockSpec((tm, LANES), lambda i: (i, 0))
    return pl.pallas_call(
        _rope_table_kernel, grid=(s // tm,),
        in_specs=[pl.BlockSpec((tm, 1), lambda i: (i, 0)), row, row, row],
        out_specs=[tab, tab],
        out_shape=[jax.ShapeDtypeStruct((s, LANES), F32)] * 2,
        compiler_params=_params(("parallel",), 16), name="rope_tables",
    )(pos_col, inv_row, cmul_row, smul_row)


NORM_ROWS = 128


def _norm_rows_to(x_ref, nw_ref, xn_ref, tm):
    def body(r, carry):
        rows = pl.ds(pl.multiple_of(r * NORM_ROWS, NORM_ROWS), NORM_ROWS)
        xn_ref[rows, :] = _rms_rows(x_ref[rows, :], nw_ref[...]).astype(BF16)
        return carry
    lax.fori_loop(0, tm // NORM_ROWS, body, 0)


def _norm_mm_kernel(x_ref, nw_ref, wt_ref, wst_ref, o_ref, os_ref, xn_ref, *, tm):
    @pl.when(pl.program_id(1) == 0)
    def _():
        _norm_rows_to(x_ref, nw_ref, xn_ref, tm)
        os_ref[...] = lax.dot_general(xn_ref[...], wst_ref[...], NT_DIMS, preferred_element_type=F32)
    o_ref[...] = lax.dot_general(xn_ref[...], wt_ref[...], NT_DIMS, preferred_element_type=F32)


def _norm_matmul(x, nw, wt_main, wt_side, *, n, tm=1024, tn=512):
    s, d = x.shape
    ns = wt_side.shape[0]
    tm = min(tm, s)
    return pl.pallas_call(
        functools.partial(_norm_mm_kernel, tm=tm), grid=(s // tm, n // tn),
        in_specs=[pl.BlockSpec((tm, d), lambda i, j: (i, 0)),
                  pl.BlockSpec((1, d), lambda i, j: (0, 0)),
                  pl.BlockSpec((tn, d), lambda i, j: (j, 0)),
                  pl.BlockSpec((ns, d), lambda i, j: (0, 0))],
        out_specs=[pl.BlockSpec((tm, tn), lambda i, j: (i, j)),
                   pl.BlockSpec((tm, ns), lambda i, j: (i, 0))],
        out_shape=[jax.ShapeDtypeStruct((s, n), F32), jax.ShapeDtypeStruct((s, ns), F32)],
        scratch_shapes=[pltpu.VMEM((tm, d), BF16)],
        compiler_params=_params(("parallel", "arbitrary"), 48), name="norm_matmul",
    )(x, nw, wt_main, wt_side)


def _gla_chunks(ps, st_ref):
    for p in ps:
        p["qe"] = p["q"] * jnp.exp(p["b"] - p["b_mid"])
        p["ke"] = p["k"] * jnp.exp(p["b_mid"] - p["b"])
        p["scores"] = jnp.where(p["mask"], _bdot_nt(p["qe"], p["ke"]), 0.0)
    for p in ps:
        q_dec = p["qe"] * jnp.exp(p["b_mid"])
        p["o"] = _bdot(p["scores"], p["v"]) + _bdot_nt(q_dec, st_ref[p["slot"]])
    for p in ps:
        k_dec = p["ke"] * jnp.exp(p["b_last"] - p["b_mid"])
        st_ref[p["slot"]] = st_ref[p["slot"]] * jnp.exp(p["b_last"]) + _bdot_tn(p["v"], k_dec)
        p["o_ref"][p["rows"], p["cols"]] = p["o"]


def _chunk_masks():
    row = lax.broadcasted_iota(jnp.int32, (CHUNK, CHUNK), 0)
    col = lax.broadcasted_iota(jnp.int32, (CHUNK, CHUNK), 1)
    return row, col


def _hgrn_kernel(logit_ref, qf_ref, ff_ref, vf_ref, qb_ref, fb_ref, vb_ref, of_ref, ob_ref, st_ref,
                 *, tb, layer):
    @pl.when(pl.program_id(0) == 0)
    def _():
        st_ref[...] = jnp.zeros_like(st_ref)

    lg = logit_ref[...]
    e = jnp.exp(lg - jnp.max(lg, axis=0, keepdims=True))
    lb = jnp.sum(e[0:layer + 1], axis=0, keepdims=True) / jnp.sum(e, axis=0, keepdims=True)
    row, col = _chunk_masks()
    causal, anti = row >= col, row <= col
    tri3_f, tri3_b = _tri3(False), _tri3(True)
    n = tb // CHUNK
    mid = CHUNK // 2
    scale = HG_DK ** -0.5

    def problem(q_ref, f_ref, v_ref, o_ref, rows, h, rev):
        cs = slice(h * HG_DK, (h + 1) * HG_DK)
        lbh = lb[:, cs]
        sig = jax.nn.sigmoid(f_ref[rows, cs])
        f = lbh + (1.0 - lbh) * sig
        b = _sdot(tri3_b if rev else tri3_f, _rhs_split3(jnp.log(f)))
        if rev:
            b_last, b_mid = b[0:1], b[CHUNK - 1 - mid:CHUNK - mid]
        else:
            b_last, b_mid = b[CHUNK - 1:CHUNK], b[mid:mid + 1]
        return dict(q=_silu(q_ref[rows, cs]) * scale, k=(1.0 - lbh) * (1.0 - sig), v=v_ref[rows, cs],
                    b=b, b_mid=b_mid, b_last=b_last, mask=anti if rev else causal,
                    slot=(HG_HEADS if rev else 0) + h, o_ref=o_ref, rows=rows, cols=cs)

    def body(c, carry):
        rf = pl.ds(pl.multiple_of(c * CHUNK, CHUNK), CHUNK)
        rb = pl.ds(pl.multiple_of((n - 1 - c) * CHUNK, CHUNK), CHUNK)
        ps = [problem(qf_ref, ff_ref, vf_ref, of_ref, rf, h, False) for h in range(HG_HEADS)]
        ps += [problem(qb_ref, fb_ref, vb_ref, ob_ref, rb, h, True) for h in range(HG_HEADS)]
        _gla_chunks(ps, st_ref)
        return carry

    lax.fori_loop(0, n, body, 0)


def _hgrn2(h_main, lb_logits, *, layer, tb=256):
    s = h_main.shape[0]
    tb = min(tb, s)
    nb = s // tb
    w = HG_HEADS * HG_DK
    fwd = lambda cb: pl.BlockSpec((tb, w), lambda i: (i, cb))
    bwd = lambda cb: pl.BlockSpec((tb, w), lambda i: (nb - 1 - i, cb))
    return pl.pallas_call(
        functools.partial(_hgrn_kernel, tb=tb, layer=layer), grid=(nb,),
        in_specs=[pl.BlockSpec(lb_logits.shape, lambda i: (0, 0)),
                  fwd(0), fwd(1), fwd(3), bwd(0), bwd(2), bwd(3)],
        out_specs=[fwd(0), bwd(0)],
        out_shape=[jax.ShapeDtypeStruct((s, w), F32)] * 2,
        scratch_shapes=[pltpu.VMEM((2 * HG_HEADS, HG_DV, HG_DK), F32)],
        compiler_params=_params(("arbitrary",), 40), name="hgrn2_scan",
    )(lb_logits, h_main, h_main, h_main, h_main, h_main, h_main)


def _ret_kernel(qf_ref, kf_ref, vf_ref, cf_ref, sf_ref, qb_ref, kb_ref, vb_ref, cb_ref, sb_ref,
                of_ref, ob_ref, st_ref, *, tb):
    @pl.when(pl.program_id(0) == 0)
    def _():
        st_ref[...] = jnp.zeros_like(st_ref)

    row, col = _chunk_masks()
    causal, anti = row >= col, row <= col
    n = tb // CHUNK
    mid = CHUNK // 2
    scale = RET_DK ** -0.5
    t_idx = lax.broadcasted_iota(jnp.int32, (CHUNK, RET_DK), 0).astype(F32)
    log_gamma = [math.log1p(-2.0 ** (-5 - h)) for h in range(RET_HEADS)]

    def rope(x, cos, sin):
        return x * cos + pltpu.roll(x, RET_DK // 2, 1) * sin

    def problem(q_ref, k_ref, v_ref, c_ref, s_ref, o_ref, rows, h, rev):
        cs = slice(h * RET_DK, (h + 1) * RET_DK)
        vs = slice(h * RET_DV, (h + 1) * RET_DV)
        cos, sin = c_ref[rows, :], s_ref[rows, :]
        lgam = log_gamma[RET_HEADS - 1 - h] if rev else log_gamma[h]
        ones = jnp.ones((1, RET_DK), F32)
        return dict(q=rope(q_ref[rows, cs], cos, sin) * scale, k=rope(k_ref[rows, cs], cos, sin),
                    v=v_ref[rows, vs], b=(CHUNK - t_idx) * lgam if rev else (t_idx + 1.0) * lgam,
                    b_mid=ones * ((mid + 1) * lgam), b_last=ones * (CHUNK * lgam),
                    mask=anti if rev else causal, slot=(RET_HEADS if rev else 0) + h,
                    o_ref=o_ref, rows=rows, cols=vs)

    def body(c, carry):
        rf = pl.ds(pl.multiple_of(c * CHUNK, CHUNK), CHUNK)
        rb = pl.ds(pl.multiple_of((n - 1 - c) * CHUNK, CHUNK), CHUNK)
        ps = [problem(qf_ref, kf_ref, vf_ref, cf_ref, sf_ref, of_ref, rf, h, False) for h in range(RET_HEADS)]
        ps += [problem(qb_ref, kb_ref, vb_ref, cb_ref, sb_ref, ob_ref, rb, h, True) for h in range(RET_HEADS)]
        _gla_chunks(ps, st_ref)
        return carry

    lax.fori_loop(0, n, body, 0)


def _retention(h_main, cos_tab, sin_tab, *, tb=256):
    s = h_main.shape[0]
    tb = min(tb, s)
    nb = s // tb
    wk, wv = RET_HEADS * RET_DK, RET_HEADS * RET_DV
    fwd = lambda w, cb: pl.BlockSpec((tb, w), lambda i: (i, cb))
    bwd = lambda w, cb: pl.BlockSpec((tb, w), lambda i: (nb - 1 - i, cb))
    return pl.pallas_call(
        functools.partial(_ret_kernel, tb=tb), grid=(nb,),
        in_specs=[fwd(wk, 0), fwd(wk, 1), fwd(wv, 1), fwd(LANES, 0), fwd(LANES, 0),
                  bwd(wk, 0), bwd(wk, 1), bwd(wv, 1), bwd(LANES, 0), bwd(LANES, 0)],
        out_specs=[fwd(wv, 0), bwd(wv, 0)],
        out_shape=[jax.ShapeDtypeStruct((s, wv), F32)] * 2,
        scratch_shapes=[pltpu.VMEM((2 * RET_HEADS, RET_DV, RET_DK), F32)],
        compiler_params=_params(("arbitrary",), 40), name="retention_scan",
    )(h_main, h_main, h_main, cos_tab, sin_tab, h_main, h_main, h_main, cos_tab, sin_tab)


def _gated_head_norm(of_ref, ob_ref, gate_ref, nw_ref, rows, heads, hd):
    o = of_ref[rows, :] + ob_ref[rows, :]
    parts = []
    for h in range(heads):
        y = o[:, h * hd:(h + 1) * hd]
        parts.append(y * lax.rsqrt(jnp.mean(y * y, axis=-1, keepdims=True) + EPS))
    y = jnp.concatenate(parts, axis=-1)
    return y * nw_ref[...] * _silu(gate_ref[rows, :])


def _mixout_kernel(*refs, tm, groups):
    x_ref, w_ref = refs[0], refs[1]
    out_ref, lhs_ref = refs[-2], refs[-1]
    grefs = refs[2:-2]

    def body(r, carry):
        rows = pl.ds(pl.multiple_of(r * NORM_ROWS, NORM_ROWS), NORM_ROWS)
        pos, col = 0, 0
        for g in groups:
            if g is None:
                val, width = grefs[pos][rows, :], grefs[pos].shape[1]
                pos += 1
            else:
                heads, hd = g
                val, width = _gated_head_norm(*grefs[pos:pos + 4], rows, heads, hd), heads * hd
                pos += 4
            lhs_ref[rows, col:col + width] = val.astype(BF16)
            col += width
        return carry
    lax.fori_loop(0, tm // NORM_ROWS, body, 0)

    out_ref[...] = x_ref[...] + jnp.dot(lhs_ref[...], w_ref[...], preferred_element_type=F32)


def _mix_out(x, w_out, group_args, groups, *, tm=256):
    s, d = x.shape
    tm = min(tm, s)
    specs = [pl.BlockSpec((tm, d), lambda i: (i, 0)),
             pl.BlockSpec(w_out.shape, lambda i: (0, 0), pipeline_mode=pl.Buffered(1))]
    arrays = [x, w_out]
    for arr, cb, width, rowvec in group_args:
        arrays.append(arr)
        if rowvec:
            specs.append(pl.BlockSpec((1, width), lambda i: (0, 0)))
        else:
            specs.append(pl.BlockSpec((tm, width), lambda i, cb=cb: (i, cb)))
    return pl.pallas_call(
        functools.partial(_mixout_kernel, tm=tm, groups=groups), grid=(s // tm,),
        in_specs=specs,
        out_specs=pl.BlockSpec((tm, d), lambda i: (i, 0)),
        out_shape=jax.ShapeDtypeStruct((s, d), F32),
        scratch_shapes=[pltpu.VMEM((tm, w_out.shape[0]), BF16)],
        compiler_params=_params(("parallel",), 40), name="mix_out",
    )(*arrays)


def _rot_pair(pr, cos, sin):
    return pr * cos + pltpu.roll(pr, MLA_ROPE, 1) * sin


def _mla_proj_kernel(cq_ref, ckv_ref, kpe_ref, qnw_ref, kvnw_ref, wq_ref, wkv_ref, cos_ref, sin_ref,
                     q_ref, k_ref, v_ref, *, scale):
    cos, sin = cos_ref[...], sin_ref[...]
    cqn = _rms_rows(cq_ref[...], qnw_ref[...]).astype(BF16)
    ckvn = _rms_rows(ckv_ref[...], kvnw_ref[...]).astype(BF16)
    pe = _rot_pair(kpe_ref[...], cos, sin).astype(BF16)
    lane = lax.broadcasted_iota(jnp.int32, pe.shape, 1)
    ones_col = jnp.where(lane == 0, 1.0, 0.0).astype(BF16)
    for h in range(MLA_HEADS):
        cols = slice(h * MLA_QK_PAD, (h + 1) * MLA_QK_PAD)
        yq = jnp.dot(cqn, wq_ref[:, cols], preferred_element_type=F32)
        q_rope = _rot_pair(yq[:, MLA_NOPE:], cos, sin)
        q_ref[h] = (jnp.concatenate([yq[:, :MLA_NOPE], q_rope], axis=1) * scale).astype(BF16)
        ykv = jnp.dot(ckvn, wkv_ref[:, cols], preferred_element_type=F32)
        k_ref[h] = jnp.concatenate([ykv[:, :MLA_NOPE].astype(BF16), pe], axis=1)
        v_ref[h] = jnp.concatenate([ykv[:, MLA_NOPE:].astype(BF16), ones_col], axis=1)


def _mla_proj(h_main, h_side, q_norm_w, kv_norm_w, wq, wkv, cos_tab, sin_tab, *, tm=512):
    s = h_main.shape[0]
    tm = min(tm, s)
    cq_block = (HG_HEADS * (3 * HG_DK + 2 * HG_DV)) // MLA_Q_RANK
    scale = (MLA_NOPE + MLA_ROPE) ** -0.5 * math.log2(math.e)
    full = lambda a: pl.BlockSpec(a.shape, lambda i: (0, 0))
    tab = pl.BlockSpec((tm, LANES), lambda i: (i, 0))
    head_out = pl.BlockSpec((MLA_HEADS, tm, MLA_QK_PAD), lambda i: (0, i, 0))
    return pl.pallas_call(
        functools.partial(_mla_proj_kernel, scale=scale), grid=(s // tm,),
        in_specs=[pl.BlockSpec((tm, MLA_Q_RANK), lambda i: (i, cq_block)),
                  pl.BlockSpec((tm, MLA_KV_RANK), lambda i: (i, 0)),
                  pl.BlockSpec((tm, LANES), lambda i: (i, MLA_KV_RANK // LANES)),
                  full(q_norm_w), full(kv_norm_w), full(wq), full(wkv), tab, tab],
        out_specs=[head_out, head_out, head_out],
        out_shape=[jax.ShapeDtypeStruct((MLA_HEADS, s, MLA_QK_PAD), BF16)] * 3,
        compiler_params=_params(("parallel",), 40), name="mla_proj",
    )(h_main, h_side, h_side, q_norm_w, kv_norm_w, wq, wkv, cos_tab, sin_tab)


ATTN_SPLIT = 2
ATTN_ROWS = 32


def _attn_kernel(q_ref, k_ref, v_ref, o_ref, s_ref, p_ref, m_ref, alpha_ref, acc_ref, *, tk, nk):
    tq = q_ref.shape[1]
    sub = tq // ATTN_SPLIT
    m_ref[...] = jnp.full(m_ref.shape, -jnp.inf, F32)
    acc_ref[...] = jnp.zeros(acc_ref.shape, F32)

    def softmax_rows(i):
        for rb in range(sub // ATTN_ROWS):
            loc = slice(rb * ATTN_ROWS, (rb + 1) * ATTN_ROWS)
            glob = slice(i * sub + rb * ATTN_ROWS, i * sub + (rb + 1) * ATTN_ROWS)
            sc = s_ref[i, loc, :]
            cols = [sc[:, c * LANES:(c + 1) * LANES] for c in range(tk // LANES)]
            col_max = functools.reduce(jnp.maximum, cols)
            m_old = m_ref[glob, :]
            m_new = jnp.maximum(m_old, jnp.max(col_max, axis=-1, keepdims=True))
            m_ref[glob, :] = m_new
            alpha_ref[glob, :] = jnp.exp2(m_old - m_new)
            p_ref[i, loc, :] = jnp.concatenate([jnp.exp2(c - m_new) for c in cols], axis=1).astype(BF16)

    def body(j, carry):
        rows = pl.ds(pl.multiple_of(j * tk, tk), tk)
        k, v = k_ref[0, rows, :], v_ref[0, rows, :]
        for i in range(ATTN_SPLIT):
            s_ref[i] = lax.dot_general(q_ref[0, i * sub:(i + 1) * sub, :], k, NT_DIMS, preferred_element_type=F32)
        for i in range(ATTN_SPLIT):
            softmax_rows(i)
            rs = slice(i * sub, (i + 1) * sub)
            alpha = alpha_ref[rs, :]
            acc_ref[rs, :] = (jnp.concatenate([alpha, alpha], axis=1) * acc_ref[rs, :]
                              + jnp.dot(p_ref[i], v, preferred_element_type=F32))
        return carry

    lax.fori_loop(0, nk, body, 0)
    acc = acc_ref[...]
    o_ref[...] = (acc[:, :MLA_V] / acc[:, MLA_V:MLA_V + 1]).astype(BF16)


def _attention(q, k, v, *, tq=1024, tk=2048):
    heads, s, _ = q.shape
    tq, tk = min(tq, s), min(tk, s)
    sub = tq // ATTN_SPLIT
    return pl.pallas_call(
        functools.partial(_attn_kernel, tk=tk, nk=s // tk), grid=(heads, s // tq),
        in_specs=[pl.BlockSpec((1, tq, MLA_QK_PAD), lambda h, i: (h, i, 0)),
                  pl.BlockSpec((1, s, MLA_QK_PAD), lambda h, i: (h, 0, 0)),
                  pl.BlockSpec((1, s, 2 * MLA_V), lambda h, i: (h, 0, 0))],
        out_specs=pl.BlockSpec((tq, MLA_V), lambda h, i: (i, h)),
        out_shape=jax.ShapeDtypeStruct((s, heads * MLA_V), BF16),
        scratch_shapes=[pltpu.VMEM((ATTN_SPLIT, sub, tk), F32), pltpu.VMEM((ATTN_SPLIT, sub, tk), BF16),
                        pltpu.VMEM((tq, LANES), F32), pltpu.VMEM((tq, LANES), F32),
                        pltpu.VMEM((tq, 2 * MLA_V), F32)],
        compiler_params=_params(("parallel", "arbitrary"), 48), name="mla_attention",
    )(q, k, v)


def _ffn_kernel(x_ref, nw_ref, wu_ref, wd_ref, fw_ref, o_ref, xn_ref, *, tm, nf, final_norm):
    f = pl.program_id(1)

    @pl.when(f == 0)
    def _():
        _norm_rows_to(x_ref, nw_ref, xn_ref, tm)
        o_ref[...] = x_ref[...]

    a = jnp.maximum(jnp.dot(xn_ref[...], wu_ref[0], preferred_element_type=F32), 0.0)
    o_ref[...] += jnp.dot((a * a).astype(BF16), wd_ref[0], preferred_element_type=F32)

    if final_norm:
        @pl.when(f == nf - 1)
        def _():
            def body(r, carry):
                rows = pl.ds(pl.multiple_of(r * NORM_ROWS, NORM_ROWS), NORM_ROWS)
                o_ref[rows, :] = _rms_rows(o_ref[rows, :], fw_ref[...])
                return carry
            lax.fori_loop(0, tm // NORM_ROWS, body, 0)


def _ffn(x, nw, w_up, w_down, final_w, *, layer, final_norm, tm=512, tf=1024):
    s, d = x.shape
    dff = w_up.shape[2]
    tm = min(tm, s)
    nf = dff // tf
    return pl.pallas_call(
        functools.partial(_ffn_kernel, tm=tm, nf=nf, final_norm=final_norm), grid=(s // tm, nf),
        in_specs=[pl.BlockSpec((tm, d), lambda i, f: (i, 0)),
                  pl.BlockSpec((1, d), lambda i, f: (0, 0)),
                  pl.BlockSpec((1, d, tf), lambda i, f: (layer, 0, f)),
                  pl.BlockSpec((1, tf, d), lambda i, f: (layer, f, 0)),
                  pl.BlockSpec((1, d), lambda i, f: (0, 0))],
        out_specs=pl.BlockSpec((tm, d), lambda i, f: (i, 0)),
        out_shape=jax.ShapeDtypeStruct((s, d), F32),
        scratch_shapes=[pltpu.VMEM((tm, d), BF16)],
        compiler_params=_params(("parallel", "arbitrary"), 48), name="ffn",
    )(x, nw, w_up, w_down, final_w)


def _gdn_prep_kernel(x_ref, xp_ref, xn_ref, cw_ref, o_ref, *, tm, nblk):
    i, sec = pl.program_id(0), pl.program_id(1)
    x = x_ref[...]
    prev = jnp.where(i > 0, xp_ref[...], 0.0)
    nxt = jnp.where(i < nblk - 1, xn_ref[...], 0.0)
    rows = lax.broadcasted_iota(jnp.int32, prev.shape, 0)
    half = CONV_WIDTH // 2

    def shifted(d):
        if d == 0:
            return x
        r = pltpu.roll(x, (-d) % tm, 0)
        if d < 0:
            edge = r[:SUBLANES]
            for t in range(-d):
                edge = jnp.where(rows == t, prev[SUBLANES + t + d:SUBLANES + t + d + 1], edge)
            return jnp.concatenate([edge, r[SUBLANES:]], axis=0)
        edge = r[tm - SUBLANES:]
        for t in range(SUBLANES - d, SUBLANES):
            edge = jnp.where(rows == t, nxt[t + d - SUBLANES:t + d - SUBLANES + 1], edge)
        return jnp.concatenate([r[:tm - SUBLANES], edge], axis=0)

    y = shifted(-half) * cw_ref[0:1]
    for j in range(1, CONV_WIDTH):
        y = y + shifted(j - half) * cw_ref[j:j + 1]
    y = _silu(y)
    parts = []
    for h in range(GDN_HEADS):
        a = y[:, h * GDN_DK:(h + 1) * GDN_DK]
        parts.append(a * lax.rsqrt(jnp.sum(a * a, axis=-1, keepdims=True) + EPS))
    nrm = jnp.concatenate(parts, axis=-1) * jnp.where(sec == 0, GDN_DK ** -0.5, 1.0)
    o_ref[...] = jnp.where(sec == 2, y, nrm)


def _gdn_prep(h_main, conv_w, *, tm=256):
    s = h_main.shape[0]
    tm = min(tm, s)
    nblk = s // tm
    w = GDN_HEADS * GDN_DK
    base = (2 * RET_HEADS * RET_DK + 2 * RET_HEADS * RET_DV) // w
    per = tm // SUBLANES
    return pl.pallas_call(
        functools.partial(_gdn_prep_kernel, tm=tm, nblk=nblk), grid=(nblk, 3),
        in_specs=[pl.BlockSpec((tm, w), lambda i, c: (i, base + c)),
                  pl.BlockSpec((SUBLANES, w), lambda i, c: (jnp.maximum(i * per - 1, 0), base + c)),
                  pl.BlockSpec((SUBLANES, w), lambda i, c: (jnp.minimum((i + 1) * per, s // SUBLANES - 1), base + c)),
                  pl.BlockSpec((CONV_WIDTH, w), lambda i, c: (0, c))],
        out_specs=pl.BlockSpec((tm, w), lambda i, c: (i, c)),
        out_shape=jax.ShapeDtypeStruct((s, 3 * w), F32),
        compiler_params=_params(("parallel", "arbitrary"), 32), name="gdn_prep",
    )(h_main, h_main, h_main, conv_w)


def _softplus(x):
    return jnp.maximum(x, 0.0) + jnp.log1p(jnp.exp(-jnp.abs(x)))


def _pair_lhs(x):
    hi = x.astype(BF16)
    lo = (x - hi.astype(F32)).astype(BF16)
    return jnp.concatenate([hi, lo, hi], axis=1)


def _pair_rhs(y, low_lanes):
    hi = y.astype(BF16)
    lo = (y - hi.astype(F32)).astype(BF16)
    zero = jnp.zeros_like(hi)

    def bd(a):
        return jnp.concatenate([jnp.where(low_lanes, a, zero), jnp.where(low_lanes, zero, a)], axis=0)

    bd_hi = bd(hi)
    return jnp.concatenate([bd_hi, bd_hi, bd(lo)], axis=0)


def _pair_rows(x, low_lanes):
    return jnp.concatenate([jnp.where(low_lanes, x, 0.0), jnp.where(low_lanes, 0.0, x)], axis=0)


def _gdn_kernel(alog_ref, dt_ref, qf_ref, kf_ref, vf_ref, gf_ref, qb_ref, kb_ref, vb_ref, gb_ref,
                of_ref, ob_ref, st_ref, *, tb):
    @pl.when(pl.program_id(0) == 0)
    def _():
        st_ref[...] = jnp.zeros_like(st_ref)

    row = lax.broadcasted_iota(jnp.int32, (CHUNK, LANES), 0)
    lane = lax.broadcasted_iota(jnp.int32, (CHUNK, LANES), 1)
    col = lane & (CHUNK - 1)
    low_lanes = lane < CHUNK
    low_row = low_lanes[0:1]
    causal, anti = row >= col, row <= col
    strict_c, strict_a = row > col, row < col
    tri3_f, tri3_b = _tri3(False), _tri3(True)
    eye = (row == col).astype(F32)
    same16 = (row >> 4) == (col >> 4)
    same32 = (row >> 5) == (col >> 5)
    in32 = jnp.logical_and(same32, jnp.logical_not(same16))
    n = tb // CHUNK
    neg_a = -jnp.exp(alog_ref[...])
    dt = dt_ref[...]

    def pairs(q_ref, k_ref, v_ref, g_ref, o_ref, rows, rev):
        raw = g_ref[rows, :]
        la = neg_a * _softplus(raw + dt)
        beta = jax.nn.sigmoid(raw)
        g = _sdot(tri3_b if rev else tri3_f, _rhs_split3(la))
        g_rows = jnp.concatenate([g, g], axis=0).T
        la0 = GDN_HEADS if rev else 0
        b0 = (3 if rev else 2) * GDN_HEADS
        last = 0 if rev else CHUNK - 1
        out = []
        for j in range(GDN_HEADS // 2):
            hs = (2 * j, 2 * j + 1)
            g_cols = [jnp.broadcast_to(g[:, la0 + h:la0 + h + 1], (CHUNK, LANES)) for h in hs]
            b_cols = [jnp.broadcast_to(beta[:, b0 + h:b0 + h + 1], (CHUNK, LANES)) for h in hs]
            g_col = jnp.where(low_lanes, g_cols[0], g_cols[1])
            g_row = jnp.where(low_row, g_rows[la0 + hs[0]:la0 + hs[0] + 1], g_rows[la0 + hs[1]:la0 + hs[1] + 1])
            out.append(dict(
                q_ref=q_ref, k_ref=k_ref, v_ref=v_ref, o_ref=o_ref, rows=rows,
                cs=[slice(h * GDN_DK, (h + 1) * GDN_DK) for h in hs],
                slots=[(GDN_HEADS if rev else 0) + h for h in hs],
                strict=strict_a if rev else strict_c,
                decay=jnp.exp(jnp.where(anti if rev else causal, g_col - g_row, -jnp.inf)),
                g_cols=g_cols, b_cols=b_cols,
                g_last=[g[last:last + 1, la0 + h:la0 + h + 1] for h in hs]))
        return out

    def body(c, carry):
        rf = pl.ds(pl.multiple_of(c * CHUNK, CHUNK), CHUNK)
        rb = pl.ds(pl.multiple_of((n - 1 - c) * CHUNK, CHUNK), CHUNK)
        ps = (pairs(qf_ref, kf_ref, vf_ref, gf_ref, of_ref, rf, False)
              + pairs(qb_ref, kb_ref, vb_ref, gb_ref, ob_ref, rb, True))
        for p in ps:
            ks = [p["k_ref"][p["rows"], cs] for cs in p["cs"]]
            qs = [p["q_ref"][p["rows"], cs] for cs in p["cs"]]
            p["kb"] = [k * b for k, b in zip(ks, p["b_cols"])]
            lhs = jnp.concatenate([jnp.concatenate([kb, q], axis=0) for kb, q in zip(p["kb"], qs)], axis=1)
            zero = jnp.zeros_like(ks[0])
            k_bd = jnp.concatenate([jnp.concatenate([ks[0], zero], axis=1),
                                    jnp.concatenate([zero, ks[1]], axis=1)], axis=0)
            kq = _bdot_nt(lhs, k_bd)
            a = jnp.where(p["strict"], kq[:CHUNK] * p["decay"], 0.0)
            p["qk"] = kq[CHUNK:] * p["decay"]
            n1 = -jnp.where(same16, a, 0.0)
            p["n_l"], p["n_r"], p["t"] = _pair_lhs(n1), _pair_rhs(n1, low_lanes), eye + n1
            p["e32"] = _pair_rhs(jnp.where(in32, a, 0.0), low_lanes)
            p["e64"] = _pair_rhs(jnp.where(same32, 0.0, a), low_lanes)
        for step in range(3):
            for p in ps:
                nn = _sdot(p["n_l"], p["n_r"])
                p["n_r"] = _pair_rhs(nn, low_lanes)
                if step < 2:
                    p["n_l"] = _pair_lhs(nn)
            for p in ps:
                p["t"] = p["t"] + _sdot(_pair_lhs(p["t"]), p["n_r"])
        for e in ("e32", "e64"):
            for p in ps:
                p["x"] = _sdot(_pair_lhs(p["t"]), p[e])
            for p in ps:
                p["t"] = p["t"] - _sdot(_pair_lhs(p["x"]), _pair_rhs(p["t"], low_lanes))
        for p in ps:
            rhs = []
            for cs, kb, b, g in zip(p["cs"], p["kb"], p["b_cols"], p["g_cols"]):
                rhs.append(jnp.concatenate([p["v_ref"][p["rows"], cs] * b, kb * jnp.exp(g)], axis=1))
            p["sol"] = _sdot(_pair_lhs(_pair_rows(p["t"], low_lanes)), _rhs_split(jnp.concatenate(rhs, axis=0)))
        for p in ps:
            p["v_new"], p["o"] = [], []
            for i, (cs, slot, g) in enumerate(zip(p["cs"], p["slots"], p["g_cols"])):
                sol = p["sol"][i * CHUNK:(i + 1) * CHUNK]
                q = p["q_ref"][p["rows"], cs]
                ws = _bdot(jnp.concatenate([sol[:, GDN_DV:], q * jnp.exp(g)], axis=0), st_ref[slot])
                p["v_new"].append(sol[:, :GDN_DV] - ws[:CHUNK])
                p["o"].append(ws[CHUNK:])
        for p in ps:
            o_intra = _bdot(_pair_rows(p["qk"], low_lanes), jnp.concatenate(p["v_new"], axis=0))
            for i, (cs, slot, g, g_last) in enumerate(zip(p["cs"], p["slots"], p["g_cols"], p["g_last"])):
                k = p["k_ref"][p["rows"], cs]
                p["o_ref"][p["rows"], cs] = p["o"][i] + o_intra[i * CHUNK:(i + 1) * CHUNK]
                st_ref[slot] = (st_ref[slot] * jnp.exp(g_last)
                                + _bdot_tn(k * jnp.exp(g_last - g), p["v_new"][i]))
        return carry

    lax.fori_loop(0, n, body, 0)


def _gdn(qkv, gates, alog_row, dt_row, *, tb=256):
    s = qkv.shape[0]
    tb = min(tb, s)
    nb = s // tb
    w = GDN_HEADS * GDN_DK
    fwd = lambda width, cb: pl.BlockSpec((tb, width), lambda i: (i, cb))
    bwd = lambda width, cb: pl.BlockSpec((tb, width), lambda i: (nb - 1 - i, cb))
    row = pl.BlockSpec((1, LANES), lambda i: (0, 0))
    return pl.pallas_call(
        functools.partial(_gdn_kernel, tb=tb), grid=(nb,),
        in_specs=[row, row,
                  fwd(w, 0), fwd(w, 1), fwd(w, 2), fwd(LANES, 0),
                  bwd(w, 0), bwd(w, 1), bwd(w, 2), bwd(LANES, 0)],
        out_specs=[fwd(w, 0), bwd(w, 0)],
        out_shape=[jax.ShapeDtypeStruct((s, w), F32)] * 2,
        scratch_shapes=[pltpu.VMEM((2 * GDN_HEADS, GDN_DK, GDN_DV), F32)],
        compiler_params=_params(("arbitrary",), 40), name="gdn_scan",
    )(alog_row, dt_row, qkv, qkv, qkv, gates, qkv, qkv, qkv, gates)


def _rot_cols(w_pe):
    half = w_pe.shape[-1] // 2
    return jnp.concatenate([-w_pe[..., half:], w_pe[..., :half]], axis=-1)


def _pad_lanes(v, fill=0.0):
    v = v.reshape(1, -1).astype(F32)
    return jnp.pad(v, ((0, 0), (0, LANES - v.shape[1])), constant_values=fill)


def kernel(x, positions, norm_mix_w, norm_ffn_w, final_norm_w, hg_lb_logits, even_w_in, hg_norm_w,
           mla_q_norm_w, mla_w_q_b, mla_kv_norm_w, mla_w_kv_b, even_w_out, odd_w_in, ret_norm_w,
           gdn_conv_w, gdn_a_log, gdn_dt_bias, gdn_norm_w, odd_w_out, ffn_w_up, ffn_w_down):
    b, s, d = x.shape
    assert b == 1 and s % CHUNK == 0
    xs = x.reshape(s, d)
    row = lambda v: v.reshape(1, -1).astype(F32)

    pos_col = positions.reshape(s, 1)
    inv_ret = ROPE_THETA ** (-jnp.arange(RET_DK // 2, dtype=F32) / (RET_DK // 2))
    inv_mla = ROPE_THETA ** (-jnp.arange(MLA_ROPE // 2, dtype=F32) / (MLA_ROPE // 2))
    ones64, zeros64 = jnp.ones((64,), F32), jnp.zeros((64,), F32)
    cos_ret, sin_ret = _rope_tables(pos_col, row(jnp.concatenate([inv_ret, inv_ret])),
                                    row(jnp.concatenate([ones64, ones64])),
                                    row(jnp.concatenate([-ones64, ones64])))
    cos_mla, sin_mla = _rope_tables(pos_col, row(jnp.concatenate([inv_mla, inv_mla, zeros64])),
                                    row(jnp.concatenate([ones64, zeros64])),
                                    row(jnp.concatenate([ones64, zeros64])))

    n_main = HG_HEADS * (3 * HG_DK + 2 * HG_DV) + MLA_Q_RANK
    wt_in = even_w_in[0].T.astype(BF16)
    wt_kpe = wt_in[n_main + MLA_KV_RANK:]
    wt_side = jnp.concatenate([wt_in[n_main:], _rot_cols(wt_kpe.T).T], axis=0)
    h_main, h_side = _norm_matmul(xs, row(norm_mix_w[0]), wt_in, wt_side, n=n_main, tn=512)

    o_f, o_b = _hgrn2(h_main, hg_lb_logits.astype(F32), layer=0)

    wq = mla_w_q_b[0].reshape(MLA_Q_RANK, MLA_HEADS, MLA_NOPE + MLA_ROPE)
    wq_pe = wq[..., MLA_NOPE:]
    wq = jnp.concatenate([wq[..., :MLA_NOPE], wq_pe, _rot_cols(wq_pe)], axis=-1)
    wq = wq.reshape(MLA_Q_RANK, MLA_HEADS * MLA_QK_PAD).astype(BF16)
    q, k, v = _mla_proj(h_main, h_side, row(mla_q_norm_w[0]), row(mla_kv_norm_w[0]), wq,
                        mla_w_kv_b[0].astype(BF16), cos_mla, sin_mla)
    o_attn = _attention(q, k, v)

    wa = HG_HEADS * HG_DV
    xs = _mix_out(xs, even_w_out[0].astype(BF16),
                  [(o_f, 0, wa, False), (o_b, 0, wa, False), (h_main, 4, wa, False), (row(hg_norm_w[0]), 0, wa, True),
                   (o_attn, 0, MLA_HEADS * MLA_V, False)],
                  ((HG_HEADS, HG_DV), None))
    w_up, w_down = ffn_w_up.astype(BF16), ffn_w_down.astype(BF16)
    xs = _ffn(xs, row(norm_ffn_w[0]), w_up, w_down, row(final_norm_w), layer=0, final_norm=False)

    n_ret = 2 * RET_HEADS * RET_DK + 2 * RET_HEADS * RET_DV
    n_qkv = GDN_HEADS * (2 * GDN_DK + GDN_DV)
    n_gate = 4 * GDN_HEADS
    wt_in = odd_w_in[0].T.astype(BF16)
    wt_main = jnp.concatenate([wt_in[:n_ret + n_qkv], wt_in[n_ret + n_qkv + n_gate:]], axis=0)
    wt_side = jnp.pad(wt_in[n_ret + n_qkv:n_ret + n_qkv + n_gate], ((0, LANES - n_gate), (0, 0)))
    h_main, h_gates = _norm_matmul(xs, row(norm_mix_w[1]), wt_main, wt_side, n=wt_main.shape[0], tn=1024)

    r_f, r_b = _retention(h_main, cos_ret, sin_ret)
    qkv = _gdn_prep(h_main, gdn_conv_w[0].astype(F32))
    g_f, g_b = _gdn(qkv, h_gates, _pad_lanes(gdn_a_log[0]), _pad_lanes(gdn_dt_bias[0]))

    wr, wg = RET_HEADS * RET_DV, GDN_HEADS * GDN_DV
    xs = _mix_out(xs, odd_w_out[0].astype(BF16),
                  [(r_f, 0, wr, False), (r_b, 0, wr, False), (h_main, 2, wr, False), (row(ret_norm_w[0]), 0, wr, True),
                   (g_f, 0, wg, False), (g_b, 0, wg, False), (h_main, 6, wg, False), (row(gdn_norm_w[0]), 0, wg, True)],
                  ((RET_HEADS, RET_DV), (GDN_HEADS, GDN_DV)))
    xs = _ffn(xs, row(norm_ffn_w[1]), w_up, w_down, row(final_norm_w), layer=1, final_norm=True)
    return xs.reshape(b, s, d)
```

```python
import functools
import math

import jax
import jax.numpy as jnp
from jax import lax
from jax.experimental import pallas as pl
from jax.experimental.pallas import tpu as pltpu

F32 = jnp.float32
BF16 = jnp.bfloat16

EPS = 1e-6
CHUNK = 64
ROPE_THETA = 10000.0
LANES = 128
SUBLANES = 8

HG_HEADS, HG_DK, HG_DV = 8, 128, 128
MLA_HEADS, MLA_NOPE, MLA_ROPE, MLA_V = 8, 128, 64, 128
MLA_Q_RANK, MLA_KV_RANK = 512, 256
MLA_QK_PAD = 256
RET_HEADS, RET_DK, RET_DV = 4, 128, 256
GDN_HEADS, GDN_DK, GDN_DV = 8, 128, 128
CONV_WIDTH = 5

NT_DIMS = (((1,), (1,)), ((), ()))
TN_DIMS = (((0,), (0,)), ((), ()))


def _params(semantics, vmem_mib):
    return pltpu.CompilerParams(dimension_semantics=semantics, vmem_limit_bytes=vmem_mib * 1024 * 1024)


def _rms_rows(x, w):
    ms = jnp.mean(x * x, axis=-1, keepdims=True)
    return x * lax.rsqrt(ms + EPS) * w


def _silu(x):
    return x * jax.nn.sigmoid(x)


def _bdot(a, b):
    return jnp.dot(a.astype(BF16), b.astype(BF16), preferred_element_type=F32)


def _bdot_nt(a, b):
    return lax.dot_general(a.astype(BF16), b.astype(BF16), NT_DIMS, preferred_element_type=F32)


def _bdot_tn(a, b):
    return lax.dot_general(a.astype(BF16), b.astype(BF16), TN_DIMS, preferred_element_type=F32)


def _rhs_split(b):
    hi = b.astype(BF16)
    lo = (b - hi.astype(F32)).astype(BF16)
    return jnp.concatenate([hi, hi, lo], axis=0)


def _rhs_split3(b):
    b1 = b.astype(BF16)
    r = b - b1.astype(F32)
    b2 = r.astype(BF16)
    b3 = (r - b2.astype(F32)).astype(BF16)
    return jnp.concatenate([b1, b2, b3], axis=0)


def _sdot(lhs_split, rhs_split):
    return jnp.dot(lhs_split, rhs_split, preferred_element_type=F32)


def _tri3(rev):
    row = lax.broadcasted_iota(jnp.int32, (CHUNK, 3 * CHUNK), 0)
    col = lax.broadcasted_iota(jnp.int32, (CHUNK, 3 * CHUNK), 1) & (CHUNK - 1)
    return jnp.where(row <= col if rev else row >= col, 1.0, 0.0).astype(BF16)


def _rope_table_kernel(pos_ref, inv_ref, cos_ret_ref, sin_ret_ref, cos_mla_ref, sin_mla_ref):
    ang = pos_ref[...].astype(F32) * inv_ref[...]
    cos, sin = jnp.cos(ang), jnp.sin(ang)
    lane = lax.broadcasted_iota(jnp.int32, cos.shape, 1)
    half, quarter = LANES // 2, LANES // 4
    cos_ret_ref[...] = jnp.where(lane < half, cos, pltpu.roll(cos, half, 1))
    sin_ret_ref[...] = jnp.where(lane < half, -sin, pltpu.roll(sin, half, 1))

    def mla(t):
        return jnp.where(lane < quarter, pltpu.roll(t, half, 1),
                         jnp.where(lane < half, pltpu.roll(t, half + quarter, 1), 0.0))

    cos_mla_ref[...] = mla(cos)
    sin_mla_ref[...] = mla(sin)


def _rope_tables(pos_col, inv_row):
    s = pos_col.shape[0]
    tm = min(512, s)
    tab = pl.BlockSpec((tm, LANES), lambda i: (i, 0))
    return pl.pallas_call(
        _rope_table_kernel, grid=(s // tm,),
        in_specs=[pl.BlockSpec((tm, 1), lambda i: (i, 0)), pl.BlockSpec((1, LANES), lambda i: (0, 0))],
        out_specs=[tab] * 4,
        out_shape=[jax.ShapeDtypeStruct((s, LANES), F32)] * 4,
        compiler_params=_params(("parallel",), 16), name="rope_tables",
    )(pos_col, inv_row)


NORM_ROWS = 128


def _norm_rows_to(x_ref, nw_ref, xn_ref, tm):
    def body(r, carry):
        rows = pl.ds(pl.multiple_of(r * NORM_ROWS, NORM_ROWS), NORM_ROWS)
        xn_ref[rows, :] = _rms_rows(x_ref[rows, :], nw_ref[...]).astype(BF16)
        return carry
    lax.fori_loop(0, tm // NORM_ROWS, body, 0)


def _norm_mm_kernel(*refs, tm, n_a):
    x_ref, nw_ref, wa_ref = refs[:3]
    wb_ref = refs[3] if len(refs) == 8 else None
    wst_ref, o_ref, os_ref, xn_ref = refs[-4:]
    j = pl.program_id(1)

    @pl.when(j == 0)
    def _():
        _norm_rows_to(x_ref, nw_ref, xn_ref, tm)
        os_ref[...] = lax.dot_general(xn_ref[...], wst_ref[...], NT_DIMS, preferred_element_type=F32)

    def project(wt_ref):
        o_ref[...] = lax.dot_general(xn_ref[...], wt_ref[...], NT_DIMS, preferred_element_type=F32)

    if wb_ref is None:
        project(wa_ref)
    else:
        pl.when(j < n_a)(lambda: project(wa_ref))
        pl.when(j >= n_a)(lambda: project(wb_ref))


def _norm_matmul(x, nw, wt_a, n_a_rows, wt_b, wt_side, *, tm=1024, tn=512):
    s, d = x.shape
    ns = wt_side.shape[0]
    tm = min(tm, s)
    n_a = n_a_rows // tn
    n = n_a_rows + (0 if wt_b is None else wt_b.shape[0])
    w_specs = [pl.BlockSpec((tn, d), lambda i, j: (jnp.minimum(j, n_a - 1), 0))]
    weights = [wt_a]
    if wt_b is not None:
        w_specs.append(pl.BlockSpec((tn, d), lambda i, j: (jnp.maximum(j - n_a, 0), 0),
                                    pipeline_mode=pl.Buffered(1)))
        weights.append(wt_b)
    return pl.pallas_call(
        functools.partial(_norm_mm_kernel, tm=tm, n_a=n_a), grid=(s // tm, n // tn),
        in_specs=[pl.BlockSpec((tm, d), lambda i, j: (i, 0)),
                  pl.BlockSpec((1, d), lambda i, j: (0, 0)),
                  *w_specs,
                  pl.BlockSpec((ns, d), lambda i, j: (0, 0))],
        out_specs=[pl.BlockSpec((tm, tn), lambda i, j: (i, j)),
                   pl.BlockSpec((tm, ns), lambda i, j: (i, 0))],
        out_shape=[jax.ShapeDtypeStruct((s, n), F32), jax.ShapeDtypeStruct((s, ns), F32)],
        scratch_shapes=[pltpu.VMEM((tm, d), BF16)],
        compiler_params=_params(("parallel", "arbitrary"), 52), name="norm_matmul",
    )(x, nw, *weights, wt_side)


def _gla_chunks(ps, st_ref):
    for p in ps:
        p["qe"] = p["q"] * jnp.exp(p["b"] - p["b_mid"])
        p["ke"] = p["k"] * jnp.exp(p["b_mid"] - p["b"])
        p["scores"] = jnp.where(p["mask"], _bdot_nt(p["qe"], p["ke"]), 0.0)
    for p in ps:
        q_dec = p["qe"] * jnp.exp(p["b_mid"])
        p["o"] = _bdot(p["scores"], p["v"]) + _bdot_nt(q_dec, st_ref[p["slot"]])
    for p in ps:
        k_dec = p["ke"] * jnp.exp(p["b_last"] - p["b_mid"])
        st_ref[p["slot"]] = st_ref[p["slot"]] * jnp.exp(p["b_last"]) + _bdot_tn(p["v"], k_dec)
        p["o_ref"][p["rows"], p["cols"]] = p["o"]


def _chunk_masks():
    row = lax.broadcasted_iota(jnp.int32, (CHUNK, CHUNK), 0)
    col = lax.broadcasted_iota(jnp.int32, (CHUNK, CHUNK), 1)
    return row, col


def _hgrn_kernel(logit_ref, qf_ref, ff_ref, vf_ref, qb_ref, fb_ref, vb_ref, of_ref, ob_ref, st_ref,
                 *, tb, layer):
    @pl.when(pl.program_id(0) == 0)
    def _():
        st_ref[...] = jnp.zeros_like(st_ref)

    lg = logit_ref[...]
    e = jnp.exp(lg - jnp.max(lg, axis=0, keepdims=True))
    lb = jnp.sum(e[0:layer + 1], axis=0, keepdims=True) / jnp.sum(e, axis=0, keepdims=True)
    row, col = _chunk_masks()
    causal, anti = row >= col, row <= col
    tri3_f, tri3_b = _tri3(False), _tri3(True)
    n = tb // CHUNK
    mid = CHUNK // 2
    scale = HG_DK ** -0.5

    def problem(q_ref, f_ref, v_ref, o_ref, rows, h, rev):
        cs = slice(h * HG_DK, (h + 1) * HG_DK)
        lbh = lb[:, cs]
        sig = jax.nn.sigmoid(f_ref[rows, cs])
        f = lbh + (1.0 - lbh) * sig
        b = _sdot(tri3_b if rev else tri3_f, _rhs_split3(jnp.log(f)))
        if rev:
            b_last, b_mid = b[0:1], b[CHUNK - 1 - mid:CHUNK - mid]
        else:
            b_last, b_mid = b[CHUNK - 1:CHUNK], b[mid:mid + 1]
        return dict(q=_silu(q_ref[rows, cs]) * scale, k=(1.0 - lbh) * (1.0 - sig), v=v_ref[rows, cs],
                    b=b, b_mid=b_mid, b_last=b_last, mask=anti if rev else causal,
                    slot=(HG_HEADS if rev else 0) + h, o_ref=o_ref, rows=rows, cols=cs)

    def body(c, carry):
        rf = pl.ds(pl.multiple_of(c * CHUNK, CHUNK), CHUNK)
        rb = pl.ds(pl.multiple_of((n - 1 - c) * CHUNK, CHUNK), CHUNK)
        ps = [problem(qf_ref, ff_ref, vf_ref, of_ref, rf, h, False) for h in range(HG_HEADS)]
        ps += [problem(qb_ref, fb_ref, vb_ref, ob_ref, rb, h, True) for h in range(HG_HEADS)]
        _gla_chunks(ps, st_ref)
        return carry

    lax.fori_loop(0, n, body, 0)


def _hgrn2(h_main, lb_logits, *, layer, tb=256):
    s = h_main.shape[0]
    tb = min(tb, s)
    nb = s // tb
    w = HG_HEADS * HG_DK
    fwd = lambda cb: pl.BlockSpec((tb, w), lambda i: (i, cb))
    bwd = lambda cb: pl.BlockSpec((tb, w), lambda i: (nb - 1 - i, cb))
    return pl.pallas_call(
        functools.partial(_hgrn_kernel, tb=tb, layer=layer), grid=(nb,),
        in_specs=[pl.BlockSpec(lb_logits.shape, lambda i: (0, 0)),
                  fwd(0), fwd(1), fwd(3), bwd(0), bwd(2), bwd(3)],
        out_specs=[fwd(0), bwd(0)],
        out_shape=[jax.ShapeDtypeStruct((s, w), F32)] * 2,
        scratch_shapes=[pltpu.VMEM((2 * HG_HEADS, HG_DV, HG_DK), F32)],
        compiler_params=_params(("arbitrary",), 40), name="hgrn2_scan",
    )(lb_logits, h_main, h_main, h_main, h_main, h_main, h_main)


def _ret_kernel(qf_ref, kf_ref, vf_ref, cf_ref, sf_ref, qb_ref, kb_ref, vb_ref, cb_ref, sb_ref,
                of_ref, ob_ref, st_ref, *, tb):
    @pl.when(pl.program_id(0) == 0)
    def _():
        st_ref[...] = jnp.zeros_like(st_ref)

    row, col = _chunk_masks()
    causal, anti = row >= col, row <= col
    n = tb // CHUNK
    mid = CHUNK // 2
    scale = RET_DK ** -0.5
    t_idx = lax.broadcasted_iota(jnp.int32, (CHUNK, RET_DK), 0).astype(F32)
    log_gamma = [math.log1p(-2.0 ** (-5 - h)) for h in range(RET_HEADS)]

    def rope(x, cos, sin):
        return x * cos + pltpu.roll(x, RET_DK // 2, 1) * sin

    def problem(q_ref, k_ref, v_ref, c_ref, s_ref, o_ref, rows, h, rev):
        cs = slice(h * RET_DK, (h + 1) * RET_DK)
        vs = slice(h * RET_DV, (h + 1) * RET_DV)
        cos, sin = c_ref[rows, :], s_ref[rows, :]
        lgam = log_gamma[RET_HEADS - 1 - h] if rev else log_gamma[h]
        ones = jnp.ones((1, RET_DK), F32)
        return dict(q=rope(q_ref[rows, cs], cos, sin) * scale, k=rope(k_ref[rows, cs], cos, sin),
                    v=v_ref[rows, vs], b=(CHUNK - t_idx) * lgam if rev else (t_idx + 1.0) * lgam,
                    b_mid=ones * ((mid + 1) * lgam), b_last=ones * (CHUNK * lgam),
                    mask=anti if rev else causal, slot=(RET_HEADS if rev else 0) + h,
                    o_ref=o_ref, rows=rows, cols=vs)

    def body(c, carry):
        rf = pl.ds(pl.multiple_of(c * CHUNK, CHUNK), CHUNK)
        rb = pl.ds(pl.multiple_of((n - 1 - c) * CHUNK, CHUNK), CHUNK)
        ps = [problem(qf_ref, kf_ref, vf_ref, cf_ref, sf_ref, of_ref, rf, h, False) for h in range(RET_HEADS)]
        ps += [problem(qb_ref, kb_ref, vb_ref, cb_ref, sb_ref, ob_ref, rb, h, True) for h in range(RET_HEADS)]
        _gla_chunks(ps, st_ref)
        return carry

    lax.fori_loop(0, n, body, 0)


def _retention(h_main, cos_tab, sin_tab, *, tb=256):
    s = h_main.shape[0]
    tb = min(tb, s)
    nb = s // tb
    wk, wv = RET_HEADS * RET_DK, RET_HEADS * RET_DV
    fwd = lambda w, cb: pl.BlockSpec((tb, w), lambda i: (i, cb))
    bwd = lambda w, cb: pl.BlockSpec((tb, w), lambda i: (nb - 1 - i, cb))
    return pl.pallas_call(
        functools.partial(_ret_kernel, tb=tb), grid=(nb,),
        in_specs=[fwd(wk, 0), fwd(wk, 1), fwd(wv, 1), fwd(LANES, 0), fwd(LANES, 0),
                  bwd(wk, 0), bwd(wk, 1), bwd(wv, 1), bwd(LANES, 0), bwd(LANES, 0)],
        out_specs=[fwd(wv, 0), bwd(wv, 0)],
        out_shape=[jax.ShapeDtypeStruct((s, wv), F32)] * 2,
        scratch_shapes=[pltpu.VMEM((2 * RET_HEADS, RET_DV, RET_DK), F32)],
        compiler_params=_params(("arbitrary",), 40), name="retention_scan",
    )(h_main, h_main, h_main, cos_tab, sin_tab, h_main, h_main, h_main, cos_tab, sin_tab)


def _gated_head_norm(of_ref, ob_ref, gate_ref, nw_ref, rows, heads, hd):
    o = of_ref[rows, :] + ob_ref[rows, :]
    parts = []
    for h in range(heads):
        y = o[:, h * hd:(h + 1) * hd]
        parts.append(y * lax.rsqrt(jnp.mean(y * y, axis=-1, keepdims=True) + EPS))
    y = jnp.concatenate(parts, axis=-1)
    return y * nw_ref[...] * _silu(gate_ref[rows, :])


def _mixout_kernel(*refs, tm, groups):
    x_ref, w_ref = refs[0], refs[1]
    out_ref, lhs_ref = refs[-2], refs[-1]
    grefs = refs[2:-2]

    def body(r, carry):
        rows = pl.ds(pl.multiple_of(r * NORM_ROWS, NORM_ROWS), NORM_ROWS)
        pos, col = 0, 0
        for g in groups:
            if g is None:
                val, width = grefs[pos][rows, :], grefs[pos].shape[1]
                pos += 1
            else:
                heads, hd = g
                val, width = _gated_head_norm(*grefs[pos:pos + 4], rows, heads, hd), heads * hd
                pos += 4
            lhs_ref[rows, col:col + width] = val.astype(BF16)
            col += width
        return carry
    lax.fori_loop(0, tm // NORM_ROWS, body, 0)

    out_ref[...] = x_ref[...] + jnp.dot(lhs_ref[...], w_ref[...], preferred_element_type=F32)


def _mix_out(x, w_out, group_args, groups, *, tm=256):
    s, d = x.shape
    tm = min(tm, s)
    specs = [pl.BlockSpec((tm, d), lambda i: (i, 0)),
             pl.BlockSpec(w_out.shape, lambda i: (0, 0), pipeline_mode=pl.Buffered(1))]
    arrays = [x, w_out]
    for arr, cb, width, rowvec in group_args:
        arrays.append(arr)
        if rowvec:
            specs.append(pl.BlockSpec((1, width), lambda i: (0, 0)))
        else:
            specs.append(pl.BlockSpec((tm, width), lambda i, cb=cb: (i, cb)))
    return pl.pallas_call(
        functools.partial(_mixout_kernel, tm=tm, groups=groups), grid=(s // tm,),
        in_specs=specs,
        out_specs=pl.BlockSpec((tm, d), lambda i: (i, 0)),
        out_shape=jax.ShapeDtypeStruct((s, d), F32),
        scratch_shapes=[pltpu.VMEM((tm, w_out.shape[0]), BF16)],
        compiler_params=_params(("parallel",), 40), name="mix_out",
    )(*arrays)


def _rot_pair(pr, cos, sin):
    return pr * cos + pltpu.roll(pr, MLA_ROPE, 1) * sin


def _mla_proj_kernel(cq_ref, ckv_ref, kpe_ref, qnw_ref, kvnw_ref, wq_ref, wkv_ref, cos_ref, sin_ref,
                     q_ref, k_ref, v_ref, *, scale):
    cos, sin = cos_ref[...], sin_ref[...]
    cqn = _rms_rows(cq_ref[...], qnw_ref[...]).astype(BF16)
    ckvn = _rms_rows(ckv_ref[...], kvnw_ref[...]).astype(BF16)
    pe = _rot_pair(kpe_ref[...], cos, sin).astype(BF16)
    lane = lax.broadcasted_iota(jnp.int32, pe.shape, 1)
    ones_col = jnp.where(lane == 0, 1.0, 0.0).astype(BF16)
    for h in range(MLA_HEADS):
        cols = slice(h * MLA_QK_PAD, (h + 1) * MLA_QK_PAD)
        yq = jnp.dot(cqn, wq_ref[:, cols], preferred_element_type=F32)
        q_rope = _rot_pair(yq[:, MLA_NOPE:], cos, sin)
        q_ref[h] = (jnp.concatenate([yq[:, :MLA_NOPE], q_rope], axis=1) * scale).astype(BF16)
        ykv = jnp.dot(ckvn, wkv_ref[:, cols], preferred_element_type=F32)
        k_ref[h] = jnp.concatenate([ykv[:, :MLA_NOPE].astype(BF16), pe], axis=1)
        v_ref[h] = jnp.concatenate([ykv[:, MLA_NOPE:].astype(BF16), ones_col], axis=1)


def _mla_proj(h_main, h_side, q_norm_w, kv_norm_w, wq, wkv, cos_tab, sin_tab, *, tm=512):
    s = h_main.shape[0]
    tm = min(tm, s)
    cq_block = (HG_HEADS * (3 * HG_DK + 2 * HG_DV)) // MLA_Q_RANK
    scale = (MLA_NOPE + MLA_ROPE) ** -0.5 * math.log2(math.e)
    full = lambda a: pl.BlockSpec(a.shape, lambda i: (0, 0))
    tab = pl.BlockSpec((tm, LANES), lambda i: (i, 0))
    head_out = pl.BlockSpec((MLA_HEADS, tm, MLA_QK_PAD), lambda i: (0, i, 0))
    return pl.pallas_call(
        functools.partial(_mla_proj_kernel, scale=scale), grid=(s // tm,),
        in_specs=[pl.BlockSpec((tm, MLA_Q_RANK), lambda i: (i, cq_block)),
                  pl.BlockSpec((tm, MLA_KV_RANK), lambda i: (i, 0)),
                  pl.BlockSpec((tm, LANES), lambda i: (i, MLA_KV_RANK // LANES)),
                  full(q_norm_w), full(kv_norm_w), full(wq), full(wkv), tab, tab],
        out_specs=[head_out, head_out, head_out],
        out_shape=[jax.ShapeDtypeStruct((MLA_HEADS, s, MLA_QK_PAD), BF16)] * 3,
        compiler_params=_params(("parallel",), 40), name="mla_proj",
    )(h_main, h_side, h_side, q_norm_w, kv_norm_w, wq, wkv, cos_tab, sin_tab)


ATTN_SPLIT = 2
ATTN_ROWS = 32


def _attn_kernel(q_ref, k_ref, v_ref, o_ref, s_ref, p_ref, m_ref, alpha_ref, acc_ref, *, tk, nk):
    tq = q_ref.shape[1]
    sub = tq // ATTN_SPLIT
    m_ref[...] = jnp.full(m_ref.shape, -jnp.inf, F32)
    acc_ref[...] = jnp.zeros(acc_ref.shape, F32)

    def softmax_rows(i):
        for rb in range(sub // ATTN_ROWS):
            loc = slice(rb * ATTN_ROWS, (rb + 1) * ATTN_ROWS)
            glob = slice(i * sub + rb * ATTN_ROWS, i * sub + (rb + 1) * ATTN_ROWS)
            sc = s_ref[i, loc, :]
            cols = [sc[:, c * LANES:(c + 1) * LANES] for c in range(tk // LANES)]
            col_max = functools.reduce(jnp.maximum, cols)
            m_old = m_ref[glob, :]
            m_new = jnp.maximum(m_old, jnp.max(col_max, axis=-1, keepdims=True))
            m_ref[glob, :] = m_new
            alpha_ref[glob, :] = jnp.exp2(m_old - m_new)
            p_ref[i, loc, :] = jnp.concatenate([jnp.exp2(c - m_new) for c in cols], axis=1).astype(BF16)

    def body(j, carry):
        rows = pl.ds(pl.multiple_of(j * tk, tk), tk)
        k, v = k_ref[0, rows, :], v_ref[0, rows, :]
        for i in range(ATTN_SPLIT):
            s_ref[i] = lax.dot_general(q_ref[0, i * sub:(i + 1) * sub, :], k, NT_DIMS, preferred_element_type=F32)
        for i in range(ATTN_SPLIT):
            softmax_rows(i)
            rs = slice(i * sub, (i + 1) * sub)
            alpha = alpha_ref[rs, :]
            acc_ref[rs, :] = (jnp.concatenate([alpha, alpha], axis=1) * acc_ref[rs, :]
                              + jnp.dot(p_ref[i], v, preferred_element_type=F32))
        return carry

    lax.fori_loop(0, nk, body, 0)
    acc = acc_ref[...]
    o_ref[...] = (acc[:, :MLA_V] / acc[:, MLA_V:MLA_V + 1]).astype(BF16)


def _attention(q, k, v, *, tq=1024, tk=2048):
    heads, s, _ = q.shape
    tq, tk = min(tq, s), min(tk, s)
    sub = tq // ATTN_SPLIT
    return pl.pallas_call(
        functools.partial(_attn_kernel, tk=tk, nk=s // tk), grid=(heads, s // tq),
        in_specs=[pl.BlockSpec((1, tq, MLA_QK_PAD), lambda h, i: (h, i, 0)),
                  pl.BlockSpec((1, s, MLA_QK_PAD), lambda h, i: (h, 0, 0)),
                  pl.BlockSpec((1, s, 2 * MLA_V), lambda h, i: (h, 0, 0))],
        out_specs=pl.BlockSpec((tq, MLA_V), lambda h, i: (i, h)),
        out_shape=jax.ShapeDtypeStruct((s, heads * MLA_V), BF16),
        scratch_shapes=[pltpu.VMEM((ATTN_SPLIT, sub, tk), F32), pltpu.VMEM((ATTN_SPLIT, sub, tk), BF16),
                        pltpu.VMEM((tq, LANES), F32), pltpu.VMEM((tq, LANES), F32),
                        pltpu.VMEM((tq, 2 * MLA_V), F32)],
        compiler_params=_params(("parallel", "arbitrary"), 48), name="mla_attention",
    )(q, k, v)


def _ffn_kernel(x_ref, nw_ref, wu_ref, wd_ref, fw_ref, o_ref, xn_ref, *, tm, nf, final_norm):
    f = pl.program_id(1)

    @pl.when(f == 0)
    def _():
        _norm_rows_to(x_ref, nw_ref, xn_ref, tm)
        o_ref[...] = x_ref[...]

    a = jnp.maximum(jnp.dot(xn_ref[...], wu_ref[0], preferred_element_type=F32), 0.0)
    o_ref[...] += jnp.dot((a * a).astype(BF16), wd_ref[0], preferred_element_type=F32)

    if final_norm:
        @pl.when(f == nf - 1)
        def _():
            def body(r, carry):
                rows = pl.ds(pl.multiple_of(r * NORM_ROWS, NORM_ROWS), NORM_ROWS)
                o_ref[rows, :] = _rms_rows(o_ref[rows, :], fw_ref[...])
                return carry
            lax.fori_loop(0, tm // NORM_ROWS, body, 0)


def _ffn(x, nw, w_up, w_down, final_w, *, layer, final_norm, tm=512, tf=1024):
    s, d = x.shape
    dff = w_up.shape[2]
    tm = min(tm, s)
    nf = dff // tf
    return pl.pallas_call(
        functools.partial(_ffn_kernel, tm=tm, nf=nf, final_norm=final_norm), grid=(s // tm, nf),
        in_specs=[pl.BlockSpec((tm, d), lambda i, f: (i, 0)),
                  pl.BlockSpec((1, d), lambda i, f: (0, 0)),
                  pl.BlockSpec((1, d, tf), lambda i, f: (layer, 0, f)),
                  pl.BlockSpec((1, tf, d), lambda i, f: (layer, f, 0)),
                  pl.BlockSpec((1, d), lambda i, f: (0, 0))],
        out_specs=pl.BlockSpec((tm, d), lambda i, f: (i, 0)),
        out_shape=jax.ShapeDtypeStruct((s, d), F32),
        scratch_shapes=[pltpu.VMEM((tm, d), BF16)],
        compiler_params=_params(("parallel", "arbitrary"), 48), name="ffn",
    )(x, nw, w_up, w_down, final_w)


def _gdn_prep_kernel(x_ref, xp_ref, xn_ref, cw_ref, o_ref, *, tm, nblk):
    i, sec = pl.program_id(0), pl.program_id(1)
    x = x_ref[...]
    prev = jnp.where(i > 0, xp_ref[...], 0.0)
    nxt = jnp.where(i < nblk - 1, xn_ref[...], 0.0)
    rows = lax.broadcasted_iota(jnp.int32, prev.shape, 0)
    half = CONV_WIDTH // 2

    def shifted(d):
        if d == 0:
            return x
        r = pltpu.roll(x, (-d) % tm, 0)
        if d < 0:
            edge = r[:SUBLANES]
            for t in range(-d):
                edge = jnp.where(rows == t, prev[SUBLANES + t + d:SUBLANES + t + d + 1], edge)
            return jnp.concatenate([edge, r[SUBLANES:]], axis=0)
        edge = r[tm - SUBLANES:]
        for t in range(SUBLANES - d, SUBLANES):
            edge = jnp.where(rows == t, nxt[t + d - SUBLANES:t + d - SUBLANES + 1], edge)
        return jnp.concatenate([r[:tm - SUBLANES], edge], axis=0)

    y = shifted(-half) * cw_ref[0:1]
    for j in range(1, CONV_WIDTH):
        y = y + shifted(j - half) * cw_ref[j:j + 1]
    y = _silu(y)
    parts = []
    for h in range(GDN_HEADS):
        a = y[:, h * GDN_DK:(h + 1) * GDN_DK]
        parts.append(a * lax.rsqrt(jnp.sum(a * a, axis=-1, keepdims=True) + EPS))
    nrm = jnp.concatenate(parts, axis=-1) * jnp.where(sec == 0, GDN_DK ** -0.5, 1.0)
    o_ref[...] = jnp.where(sec == 2, y, nrm)


def _gdn_prep(h_main, conv_w, *, tm=256):
    s = h_main.shape[0]
    tm = min(tm, s)
    nblk = s // tm
    w = GDN_HEADS * GDN_DK
    base = (2 * RET_HEADS * RET_DK + 2 * RET_HEADS * RET_DV) // w
    per = tm // SUBLANES
    return pl.pallas_call(
        functools.partial(_gdn_prep_kernel, tm=tm, nblk=nblk), grid=(nblk, 3),
        in_specs=[pl.BlockSpec((tm, w), lambda i, c: (i, base + c)),
                  pl.BlockSpec((SUBLANES, w), lambda i, c: (jnp.maximum(i * per - 1, 0), base + c)),
                  pl.BlockSpec((SUBLANES, w), lambda i, c: (jnp.minimum((i + 1) * per, s // SUBLANES - 1), base + c)),
                  pl.BlockSpec((CONV_WIDTH, w), lambda i, c: (0, c))],
        out_specs=pl.BlockSpec((tm, w), lambda i, c: (i, c)),
        out_shape=jax.ShapeDtypeStruct((s, 3 * w), F32),
        compiler_params=_params(("parallel", "arbitrary"), 32), name="gdn_prep",
    )(h_main, h_main, h_main, conv_w)


def _softplus(x):
    return jnp.maximum(x, 0.0) + jnp.log1p(jnp.exp(-jnp.abs(x)))


def _pair_lhs(x):
    hi = x.astype(BF16)
    lo = (x - hi.astype(F32)).astype(BF16)
    return jnp.concatenate([hi, lo, hi], axis=1)


def _pair_rhs(y, low_lanes):
    hi = y.astype(BF16)
    lo = (y - hi.astype(F32)).astype(BF16)
    zero = jnp.zeros_like(hi)

    def bd(a):
        return jnp.concatenate([jnp.where(low_lanes, a, zero), jnp.where(low_lanes, zero, a)], axis=0)

    bd_hi = bd(hi)
    return jnp.concatenate([bd_hi, bd_hi, bd(lo)], axis=0)


def _pair_lhs2(x):
    hi = x.astype(BF16)
    lo = (x - hi.astype(F32)).astype(BF16)
    return jnp.concatenate([hi, lo], axis=1)


def _pair_rhs1(y, low_lanes):
    hi = y.astype(BF16)
    zero = jnp.zeros_like(hi)
    bd = jnp.concatenate([jnp.where(low_lanes, hi, zero), jnp.where(low_lanes, zero, hi)], axis=0)
    return jnp.concatenate([bd, bd], axis=0)


def _pair_rows(x, low_lanes):
    return jnp.concatenate([jnp.where(low_lanes, x, 0.0), jnp.where(low_lanes, 0.0, x)], axis=0)


def _gdn_kernel(alog_ref, dt_ref, qf_ref, kf_ref, vf_ref, gf_ref, qb_ref, kb_ref, vb_ref, gb_ref,
                of_ref, ob_ref, st_ref, *, tb):
    @pl.when(pl.program_id(0) == 0)
    def _():
        st_ref[...] = jnp.zeros_like(st_ref)

    row = lax.broadcasted_iota(jnp.int32, (CHUNK, LANES), 0)
    lane = lax.broadcasted_iota(jnp.int32, (CHUNK, LANES), 1)
    col = lane & (CHUNK - 1)
    low_lanes = lane < CHUNK
    low_row = low_lanes[0:1]
    causal, anti = row >= col, row <= col
    strict_c, strict_a = row > col, row < col
    tri3_f, tri3_b = _tri3(False), _tri3(True)
    eye = (row == col).astype(F32)
    same16 = (row >> 4) == (col >> 4)
    same32 = (row >> 5) == (col >> 5)
    in32 = jnp.logical_and(same32, jnp.logical_not(same16))
    n = tb // CHUNK
    neg_a = -jnp.exp(alog_ref[...])
    dt = dt_ref[...]

    def pairs(q_ref, k_ref, v_ref, g_ref, o_ref, rows, rev):
        raw = g_ref[rows, :]
        la = neg_a * _softplus(raw + dt)
        beta = jax.nn.sigmoid(raw)
        g = _sdot(tri3_b if rev else tri3_f, _rhs_split3(la))
        g_rows = jnp.concatenate([g, g], axis=0).T
        la0 = GDN_HEADS if rev else 0
        b0 = (3 if rev else 2) * GDN_HEADS
        last = 0 if rev else CHUNK - 1
        out = []
        for j in range(GDN_HEADS // 2):
            hs = (2 * j, 2 * j + 1)
            g_cols = [jnp.broadcast_to(g[:, la0 + h:la0 + h + 1], (CHUNK, LANES)) for h in hs]
            b_cols = [jnp.broadcast_to(beta[:, b0 + h:b0 + h + 1], (CHUNK, LANES)) for h in hs]
            g_col = jnp.where(low_lanes, g_cols[0], g_cols[1])
            g_row = jnp.where(low_row, g_rows[la0 + hs[0]:la0 + hs[0] + 1], g_rows[la0 + hs[1]:la0 + hs[1] + 1])
            out.append(dict(
                q_ref=q_ref, k_ref=k_ref, v_ref=v_ref, o_ref=o_ref, rows=rows,
                cs=[slice(h * GDN_DK, (h + 1) * GDN_DK) for h in hs],
                slots=[(GDN_HEADS if rev else 0) + h for h in hs],
                strict=strict_a if rev else strict_c,
                decay=jnp.exp(jnp.where(anti if rev else causal, g_col - g_row, -jnp.inf)),
                g_cols=g_cols, b_cols=b_cols,
                g_last=[g[last:last + 1, la0 + h:la0 + h + 1] for h in hs]))
        return out

    def body(c, carry):
        rf = pl.ds(pl.multiple_of(c * CHUNK, CHUNK), CHUNK)
        rb = pl.ds(pl.multiple_of((n - 1 - c) * CHUNK, CHUNK), CHUNK)
        ps = (pairs(qf_ref, kf_ref, vf_ref, gf_ref, of_ref, rf, False)
              + pairs(qb_ref, kb_ref, vb_ref, gb_ref, ob_ref, rb, True))
        for p in ps:
            ks = [p["k_ref"][p["rows"], cs] for cs in p["cs"]]
            qs = [p["q_ref"][p["rows"], cs] for cs in p["cs"]]
            p["kb"] = [k * b for k, b in zip(ks, p["b_cols"])]
            lhs = jnp.concatenate([jnp.concatenate([kb, q], axis=0) for kb, q in zip(p["kb"], qs)], axis=1)
            zero = jnp.zeros_like(ks[0])
            k_bd = jnp.concatenate([jnp.concatenate([ks[0], zero], axis=1),
                                    jnp.concatenate([zero, ks[1]], axis=1)], axis=0)
            kq = _bdot_nt(lhs, k_bd)
            a = jnp.where(p["strict"], kq[:CHUNK] * p["decay"], 0.0)
            p["qk"] = kq[CHUNK:] * p["decay"]
            n1 = -jnp.where(same16, a, 0.0)
            p["n_l"], p["n_r"], p["t"] = _pair_lhs2(n1), _pair_rhs1(n1, low_lanes), eye + n1
            p["e32"] = _pair_rhs1(jnp.where(in32, a, 0.0), low_lanes)
            p["e64"] = _pair_rhs1(jnp.where(same32, 0.0, a), low_lanes)
        for step in range(3):
            for p in ps:
                nn = _sdot(p["n_l"], p["n_r"])
                p["n_r"] = _pair_rhs(nn, low_lanes)
                if step < 2:
                    p["n_l"] = _pair_lhs(nn)
            for p in ps:
                p["t"] = p["t"] + _sdot(_pair_lhs(p["t"]), p["n_r"])
        for e in ("e32", "e64"):
            for p in ps:
                p["x"] = _sdot(_pair_lhs2(p["t"]), p[e])
            for p in ps:
                p["t"] = p["t"] - _sdot(_pair_lhs(p["x"]), _pair_rhs(p["t"], low_lanes))
        for p in ps:
            rhs = []
            for cs, kb, b, g in zip(p["cs"], p["kb"], p["b_cols"], p["g_cols"]):
                rhs.append(jnp.concatenate([p["v_ref"][p["rows"], cs] * b, kb * jnp.exp(g)], axis=1))
            p["sol"] = _sdot(_pair_lhs(_pair_rows(p["t"], low_lanes)), _rhs_split(jnp.concatenate(rhs, axis=0)))
        for p in ps:
            p["v_new"], p["o"] = [], []
            for i, (cs, slot, g) in enumerate(zip(p["cs"], p["slots"], p["g_cols"])):
                sol = p["sol"][i * CHUNK:(i + 1) * CHUNK]
                q = p["q_ref"][p["rows"], cs]
                ws = _bdot(jnp.concatenate([sol[:, GDN_DV:], q * jnp.exp(g)], axis=0), st_ref[slot])
                p["v_new"].append(sol[:, :GDN_DV] - ws[:CHUNK])
                p["o"].append(ws[CHUNK:])
        for p in ps:
            o_intra = _bdot(_pair_rows(p["qk"], low_lanes), jnp.concatenate(p["v_new"], axis=0))
            for i, (cs, slot, g, g_last) in enumerate(zip(p["cs"], p["slots"], p["g_cols"], p["g_last"])):
                k = p["k_ref"][p["rows"], cs]
                p["o_ref"][p["rows"], cs] = p["o"][i] + o_intra[i * CHUNK:(i + 1) * CHUNK]
                st_ref[slot] = (st_ref[slot] * jnp.exp(g_last)
                                + _bdot_tn(k * jnp.exp(g_last - g), p["v_new"][i]))
        return carry

    lax.fori_loop(0, n, body, 0)


def _gdn(qkv, gates, alog_row, dt_row, *, tb=256):
    s = qkv.shape[0]
    tb = min(tb, s)
    nb = s // tb
    w = GDN_HEADS * GDN_DK
    fwd = lambda width, cb: pl.BlockSpec((tb, width), lambda i: (i, cb))
    bwd = lambda width, cb: pl.BlockSpec((tb, width), lambda i: (nb - 1 - i, cb))
    row = pl.BlockSpec((1, LANES), lambda i: (0, 0))
    return pl.pallas_call(
        functools.partial(_gdn_kernel, tb=tb), grid=(nb,),
        in_specs=[row, row,
                  fwd(w, 0), fwd(w, 1), fwd(w, 2), fwd(LANES, 0),
                  bwd(w, 0), bwd(w, 1), bwd(w, 2), bwd(LANES, 0)],
        out_specs=[fwd(w, 0), bwd(w, 0)],
        out_shape=[jax.ShapeDtypeStruct((s, w), F32)] * 2,
        scratch_shapes=[pltpu.VMEM((2 * GDN_HEADS, GDN_DK, GDN_DV), F32)],
        compiler_params=_params(("arbitrary",), 40), name="gdn_scan",
    )(alog_row, dt_row, qkv, qkv, qkv, gates, qkv, qkv, qkv, gates)


def _rot_cols(w_pe):
    half = w_pe.shape[-1] // 2
    return jnp.concatenate([-w_pe[..., half:], w_pe[..., :half]], axis=-1)


def _pad_lanes(v, fill=0.0):
    v = v.reshape(1, -1).astype(F32)
    return jnp.pad(v, ((0, 0), (0, LANES - v.shape[1])), constant_values=fill)


def kernel(x, positions, norm_mix_w, norm_ffn_w, final_norm_w, hg_lb_logits, even_w_in, hg_norm_w,
           mla_q_norm_w, mla_w_q_b, mla_kv_norm_w, mla_w_kv_b, even_w_out, odd_w_in, ret_norm_w,
           gdn_conv_w, gdn_a_log, gdn_dt_bias, gdn_norm_w, odd_w_out, ffn_w_up, ffn_w_down):
    b, s, d = x.shape
    assert b == 1 and s % CHUNK == 0
    xs = x.reshape(s, d)
    row = lambda v: v.reshape(1, -1).astype(F32)

    pos_col = positions.reshape(s, 1)
    inv_ret = ROPE_THETA ** (-jnp.arange(RET_DK // 2, dtype=F32) / (RET_DK // 2))
    inv_mla = ROPE_THETA ** (-jnp.arange(MLA_ROPE // 2, dtype=F32) / (MLA_ROPE // 2))
    inv_row = row(jnp.concatenate([inv_ret, inv_mla, jnp.zeros((LANES - RET_DK // 2 - MLA_ROPE // 2,), F32)]))
    cos_ret, sin_ret, cos_mla, sin_mla = _rope_tables(pos_col, inv_row)

    n_main = HG_HEADS * (3 * HG_DK + 2 * HG_DV) + MLA_Q_RANK
    wt_in = even_w_in[0].T.astype(BF16)
    wt_kpe = wt_in[n_main + MLA_KV_RANK:]
    wt_side = jnp.concatenate([wt_in[n_main:], _rot_cols(wt_kpe.T).T], axis=0)
    h_main, h_side = _norm_matmul(xs, row(norm_mix_w[0]), wt_in, n_main, None, wt_side, tn=512)

    o_f, o_b = _hgrn2(h_main, hg_lb_logits.astype(F32), layer=0)

    wq = mla_w_q_b[0].reshape(MLA_Q_RANK, MLA_HEADS, MLA_NOPE + MLA_ROPE)
    wq_pe = wq[..., MLA_NOPE:]
    wq = jnp.concatenate([wq[..., :MLA_NOPE], wq_pe, _rot_cols(wq_pe)], axis=-1)
    wq = wq.reshape(MLA_Q_RANK, MLA_HEADS * MLA_QK_PAD).astype(BF16)
    q, k, v = _mla_proj(h_main, h_side, row(mla_q_norm_w[0]), row(mla_kv_norm_w[0]), wq,
                        mla_w_kv_b[0].astype(BF16), cos_mla, sin_mla)
    o_attn = _attention(q, k, v)

    wa = HG_HEADS * HG_DV
    xs = _mix_out(xs, even_w_out[0].astype(BF16),
                  [(o_f, 0, wa, False), (o_b, 0, wa, False), (h_main, 4, wa, False), (row(hg_norm_w[0]), 0, wa, True),
                   (o_attn, 0, MLA_HEADS * MLA_V, False)],
                  ((HG_HEADS, HG_DV), None))
    w_up, w_down = ffn_w_up.astype(BF16), ffn_w_down.astype(BF16)
    xs = _ffn(xs, row(norm_ffn_w[0]), w_up, w_down, row(final_norm_w), layer=0, final_norm=False)

    n_ret = 2 * RET_HEADS * RET_DK + 2 * RET_HEADS * RET_DV
    n_qkv = GDN_HEADS * (2 * GDN_DK + GDN_DV)
    n_gate = 4 * GDN_HEADS
    wt_in = odd_w_in[0].T.astype(BF16)
    wt_gate = wt_in[n_ret + n_qkv + n_gate:]
    wt_side = jnp.pad(wt_in[n_ret + n_qkv:n_ret + n_qkv + n_gate], ((0, LANES - n_gate), (0, 0)))
    h_main, h_gates = _norm_matmul(xs, row(norm_mix_w[1]), wt_in, n_ret + n_qkv, wt_gate, wt_side, tn=1024)

    r_f, r_b = _retention(h_main, cos_ret, sin_ret)
    qkv = _gdn_prep(h_main, gdn_conv_w[0].astype(F32))
    g_f, g_b = _gdn(qkv, h_gates, _pad_lanes(gdn_a_log[0]), _pad_lanes(gdn_dt_bias[0]))

    wr, wg = RET_HEADS * RET_DV, GDN_HEADS * GDN_DV
    xs = _mix_out(xs, odd_w_out[0].astype(BF16),
                  [(r_f, 0, wr, False), (r_b, 0, wr, False), (h_main, 2, wr, False), (row(ret_norm_w[0]), 0, wr, True),
                   (g_f, 0, wg, False), (g_b, 0, wg, False), (h_main, 6, wg, False), (row(gdn_norm_w[0]), 0, wg, True)],
                  ((RET_HEADS, RET_DV), (GDN_HEADS, GDN_DV)))
    xs = _ffn(xs, row(norm_ffn_w[1]), w_up, w_down, row(final_norm_w), layer=1, final_norm=True)
    return xs.reshape(b, s, d)
```

```python
import functools
import math

import jax
import jax.numpy as jnp
from jax import lax
from jax.experimental import pallas as pl
from jax.experimental.pallas import tpu as pltpu

F32 = jnp.float32
BF16 = jnp.bfloat16

EPS = 1e-6
CHUNK = 64
ROPE_THETA = 10000.0
LANES = 128
SUBLANES = 8

HG_HEADS, HG_DK, HG_DV = 8, 128, 128
MLA_HEADS, MLA_NOPE, MLA_ROPE, MLA_V = 8, 128, 64, 128
MLA_Q_RANK, MLA_KV_RANK = 512, 256
MLA_QK_PAD = 256
RET_HEADS, RET_DK, RET_DV = 4, 128, 256
RET_UNROLL = 2
GDN_HEADS, GDN_DK, GDN_DV = 8, 128, 128
CONV_WIDTH = 5

NT_DIMS = (((1,), (1,)), ((), ()))
TN_DIMS = (((0,), (0,)), ((), ()))


def _params(semantics, vmem_mib):
    return pltpu.CompilerParams(dimension_semantics=semantics, vmem_limit_bytes=vmem_mib * 1024 * 1024)


def _rms_rows(x, w):
    ms = jnp.mean(x * x, axis=-1, keepdims=True)
    return x * lax.rsqrt(ms + EPS) * w


def _silu(x):
    return x * jax.nn.sigmoid(x)


def _bdot(a, b):
    return jnp.dot(a.astype(BF16), b.astype(BF16), preferred_element_type=F32)


def _bdot_nt(a, b):
    return lax.dot_general(a.astype(BF16), b.astype(BF16), NT_DIMS, preferred_element_type=F32)


def _bdot_tn(a, b):
    return lax.dot_general(a.astype(BF16), b.astype(BF16), TN_DIMS, preferred_element_type=F32)


def _rhs_split(b):
    hi = b.astype(BF16)
    lo = (b - hi.astype(F32)).astype(BF16)
    return jnp.concatenate([hi, hi, lo], axis=0)


def _rhs_split3(b):
    b1 = b.astype(BF16)
    r = b - b1.astype(F32)
    b2 = r.astype(BF16)
    b3 = (r - b2.astype(F32)).astype(BF16)
    return jnp.concatenate([b1, b2, b3], axis=0)


def _sdot(lhs_split, rhs_split):
    return jnp.dot(lhs_split, rhs_split, preferred_element_type=F32)


def _tri3(rev):
    row = lax.broadcasted_iota(jnp.int32, (CHUNK, 3 * CHUNK), 0)
    col = lax.broadcasted_iota(jnp.int32, (CHUNK, 3 * CHUNK), 1) & (CHUNK - 1)
    return jnp.where(row <= col if rev else row >= col, 1.0, 0.0).astype(BF16)


def _rope_table_kernel(pos_ref, inv_ref, cos_ret_ref, sin_ret_ref, cos_mla_ref, sin_mla_ref):
    ang = pos_ref[...].astype(F32) * inv_ref[...]
    cos, sin = jnp.cos(ang), jnp.sin(ang)
    lane = lax.broadcasted_iota(jnp.int32, cos.shape, 1)
    half, quarter = LANES // 2, LANES // 4
    cos_ret_ref[...] = jnp.where(lane < half, cos, pltpu.roll(cos, half, 1))
    sin_ret_ref[...] = jnp.where(lane < half, -sin, pltpu.roll(sin, half, 1))

    def mla(t):
        return jnp.where(lane < quarter, pltpu.roll(t, half, 1),
                         jnp.where(lane < half, pltpu.roll(t, half + quarter, 1), 0.0))

    cos_mla_ref[...] = mla(cos)
    sin_mla_ref[...] = mla(sin)


def _rope_tables(pos_col, inv_row):
    s = pos_col.shape[0]
    tm = min(512, s)
    tab = pl.BlockSpec((tm, LANES), lambda i: (i, 0))
    return pl.pallas_call(
        _rope_table_kernel, grid=(s // tm,),
        in_specs=[pl.BlockSpec((tm, 1), lambda i: (i, 0)), pl.BlockSpec((1, LANES), lambda i: (0, 0))],
        out_specs=[tab] * 4,
        out_shape=[jax.ShapeDtypeStruct((s, LANES), F32)] * 4,
        compiler_params=_params(("parallel",), 16), name="rope_tables",
    )(pos_col, inv_row)


NORM_ROWS = 128


def _norm_rows_to(x_ref, nw_ref, xn_ref, tm):
    def body(r, carry):
        rows = pl.ds(pl.multiple_of(r * NORM_ROWS, NORM_ROWS), NORM_ROWS)
        xn_ref[rows, :] = _rms_rows(x_ref[rows, :], nw_ref[...]).astype(BF16)
        return carry
    lax.fori_loop(0, tm // NORM_ROWS, body, 0)


def _norm_mm_kernel(*refs, tm, n_a):
    x_ref, nw_ref, wa_ref = refs[:3]
    wb_ref = refs[3] if len(refs) == 8 else None
    wst_ref, o_ref, os_ref, xn_ref = refs[-4:]
    j = pl.program_id(1)

    @pl.when(j == 0)
    def _():
        _norm_rows_to(x_ref, nw_ref, xn_ref, tm)
        os_ref[...] = lax.dot_general(xn_ref[...], wst_ref[...], NT_DIMS, preferred_element_type=F32)

    def project(wt_ref):
        o_ref[...] = lax.dot_general(xn_ref[...], wt_ref[...], NT_DIMS, preferred_element_type=F32)

    if wb_ref is None:
        project(wa_ref)
    else:
        pl.when(j < n_a)(lambda: project(wa_ref))
        pl.when(j >= n_a)(lambda: project(wb_ref))


def _norm_matmul(x, nw, wt_a, n_a_rows, wt_b, wt_side, *, tm=1024, tn=512):
    s, d = x.shape
    ns = wt_side.shape[0]
    tm = min(tm, s)
    n_a = n_a_rows // tn
    n = n_a_rows + (0 if wt_b is None else wt_b.shape[0])
    w_specs = [pl.BlockSpec((tn, d), lambda i, j: (jnp.minimum(j, n_a - 1), 0))]
    weights = [wt_a]
    if wt_b is not None:
        w_specs.append(pl.BlockSpec((tn, d), lambda i, j: (jnp.maximum(j - n_a, 0), 0),
                                    pipeline_mode=pl.Buffered(1)))
        weights.append(wt_b)
    return pl.pallas_call(
        functools.partial(_norm_mm_kernel, tm=tm, n_a=n_a), grid=(s // tm, n // tn),
        in_specs=[pl.BlockSpec((tm, d), lambda i, j: (i, 0)),
                  pl.BlockSpec((1, d), lambda i, j: (0, 0)),
                  *w_specs,
                  pl.BlockSpec((ns, d), lambda i, j: (0, 0))],
        out_specs=[pl.BlockSpec((tm, tn), lambda i, j: (i, j)),
                   pl.BlockSpec((tm, ns), lambda i, j: (i, 0))],
        out_shape=[jax.ShapeDtypeStruct((s, n), F32), jax.ShapeDtypeStruct((s, ns), F32)],
        scratch_shapes=[pltpu.VMEM((tm, d), BF16)],
        compiler_params=_params(("parallel", "arbitrary"), 52), name="norm_matmul",
    )(x, nw, *weights, wt_side)


def _gla_chunks(ps, st_ref):
    for p in ps:
        p["qe"] = p["q"] * jnp.exp(p["b"] - p["b_mid"])
        p["ke"] = p["k"] * jnp.exp(p["b_mid"] - p["b"])
        p["scores"] = jnp.where(p["mask"], _bdot_nt(p["qe"], p["ke"]), 0.0)
    for p in ps:
        q_dec = p["qe"] * jnp.exp(p["b_mid"])
        p["o"] = _bdot(p["scores"], p["v"]) + _bdot_nt(q_dec, st_ref[p["slot"]])
    for p in ps:
        k_dec = p["ke"] * jnp.exp(p["b_last"] - p["b_mid"])
        st_ref[p["slot"]] = st_ref[p["slot"]] * jnp.exp(p["b_last"]) + _bdot_tn(p["v"], k_dec)
        p["o_ref"][p["rows"], p["cols"]] = p["o"]


def _chunk_masks():
    row = lax.broadcasted_iota(jnp.int32, (CHUNK, CHUNK), 0)
    col = lax.broadcasted_iota(jnp.int32, (CHUNK, CHUNK), 1)
    return row, col


def _hgrn_kernel(logit_ref, qf_ref, ff_ref, vf_ref, qb_ref, fb_ref, vb_ref, of_ref, ob_ref, st_ref,
                 *, tb, layer):
    @pl.when(pl.program_id(0) == 0)
    def _():
        st_ref[...] = jnp.zeros_like(st_ref)

    lg = logit_ref[...]
    e = jnp.exp(lg - jnp.max(lg, axis=0, keepdims=True))
    lb = jnp.sum(e[0:layer + 1], axis=0, keepdims=True) / jnp.sum(e, axis=0, keepdims=True)
    row, col = _chunk_masks()
    causal, anti = row >= col, row <= col
    tri3_f, tri3_b = _tri3(False), _tri3(True)
    n = tb // CHUNK
    mid = CHUNK // 2
    scale = HG_DK ** -0.5

    def problem(q_ref, f_ref, v_ref, o_ref, rows, h, rev):
        cs = slice(h * HG_DK, (h + 1) * HG_DK)
        lbh = lb[:, cs]
        sig = jax.nn.sigmoid(f_ref[rows, cs])
        f = lbh + (1.0 - lbh) * sig
        b = _sdot(tri3_b if rev else tri3_f, _rhs_split3(jnp.log(f)))
        if rev:
            b_last, b_mid = b[0:1], b[CHUNK - 1 - mid:CHUNK - mid]
        else:
            b_last, b_mid = b[CHUNK - 1:CHUNK], b[mid:mid + 1]
        return dict(q=_silu(q_ref[rows, cs]) * scale, k=(1.0 - lbh) * (1.0 - sig), v=v_ref[rows, cs],
                    b=b, b_mid=b_mid, b_last=b_last, mask=anti if rev else causal,
                    slot=(HG_HEADS if rev else 0) + h, o_ref=o_ref, rows=rows, cols=cs)

    def body(c, carry):
        rf = pl.ds(pl.multiple_of(c * CHUNK, CHUNK), CHUNK)
        rb = pl.ds(pl.multiple_of((n - 1 - c) * CHUNK, CHUNK), CHUNK)
        ps = [problem(qf_ref, ff_ref, vf_ref, of_ref, rf, h, False) for h in range(HG_HEADS)]
        ps += [problem(qb_ref, fb_ref, vb_ref, ob_ref, rb, h, True) for h in range(HG_HEADS)]
        _gla_chunks(ps, st_ref)
        return carry

    lax.fori_loop(0, n, body, 0)


def _hgrn2(h_main, lb_logits, *, layer, tb=512):
    s = h_main.shape[0]
    tb = min(tb, s)
    nb = s // tb
    w = HG_HEADS * HG_DK
    fwd = lambda cb: pl.BlockSpec((tb, w), lambda i: (i, cb))
    bwd = lambda cb: pl.BlockSpec((tb, w), lambda i: (nb - 1 - i, cb))
    return pl.pallas_call(
        functools.partial(_hgrn_kernel, tb=tb, layer=layer), grid=(nb,),
        in_specs=[pl.BlockSpec(lb_logits.shape, lambda i: (0, 0)),
                  fwd(0), fwd(1), fwd(3), bwd(0), bwd(2), bwd(3)],
        out_specs=[fwd(0), bwd(0)],
        out_shape=[jax.ShapeDtypeStruct((s, w), F32)] * 2,
        scratch_shapes=[pltpu.VMEM((2 * HG_HEADS, HG_DV, HG_DK), F32)],
        compiler_params=_params(("arbitrary",), 40), name="hgrn2_scan",
    )(lb_logits, h_main, h_main, h_main, h_main, h_main, h_main)


def _ret_kernel(qf_ref, kf_ref, vf_ref, cf_ref, sf_ref, qb_ref, kb_ref, vb_ref, cb_ref, sb_ref,
                of_ref, ob_ref, st_ref, *, tb):
    @pl.when(pl.program_id(0) == 0)
    def _():
        st_ref[...] = jnp.zeros_like(st_ref)

    row, col = _chunk_masks()
    n = tb // CHUNK
    scale = RET_DK ** -0.5
    t_idx = lax.broadcasted_iota(jnp.int32, (CHUNK, RET_DK), 0).astype(F32)
    log_gamma = [math.log1p(-2.0 ** (-5 - h)) for h in range(RET_HEADS)]

    def rope(x, cos, sin):
        return x * cos + pltpu.roll(x, RET_DK // 2, 1) * sin

    def decay_tables(h, rev):
        lgam = log_gamma[RET_HEADS - 1 - h] if rev else log_gamma[h]
        steps = (CHUNK - t_idx) if rev else (t_idx + 1.0)
        dist = ((col - row) if rev else (row - col)).astype(F32)
        return dict(intra=jnp.where(dist >= 0.0, jnp.exp(dist * lgam), 0.0),
                    q_fac=jnp.exp(steps * lgam), k_fac=jnp.exp((CHUNK - steps) * lgam),
                    st_fac=math.exp(CHUNK * lgam))

    tables = {(h, rev): decay_tables(h, rev) for h in range(RET_HEADS) for rev in (False, True)}

    def problem(q_ref, k_ref, v_ref, c_ref, s_ref, o_ref, rows, h, rev):
        cs = slice(h * RET_DK, (h + 1) * RET_DK)
        vs = slice(h * RET_DV, (h + 1) * RET_DV)
        cos, sin = c_ref[rows, :], s_ref[rows, :]
        return dict(q=rope(q_ref[rows, cs], cos, sin) * scale, k=rope(k_ref[rows, cs], cos, sin),
                    v=v_ref[rows, vs], slot=(RET_HEADS if rev else 0) + h, o_ref=o_ref, rows=rows, cols=vs,
                    **tables[(h, rev)])

    def body(c, carry):
        groups = []
        for u in range(RET_UNROLL):
            rf = pl.ds(pl.multiple_of((RET_UNROLL * c + u) * CHUNK, CHUNK), CHUNK)
            rb = pl.ds(pl.multiple_of((n - 1 - RET_UNROLL * c - u) * CHUNK, CHUNK), CHUNK)
            groups.append(
                [problem(qf_ref, kf_ref, vf_ref, cf_ref, sf_ref, of_ref, rf, h, False) for h in range(RET_HEADS)]
                + [problem(qb_ref, kb_ref, vb_ref, cb_ref, sb_ref, ob_ref, rb, h, True) for h in range(RET_HEADS)])
        for ps in groups:
            for p in ps:
                p["scores"] = _bdot_nt(p["q"], p["k"]) * p["intra"]
                p["update"] = _bdot_tn(p["v"], p["k"] * p["k_fac"])
        for ps in groups:
            for p in ps:
                p["o"] = _bdot(p["scores"], p["v"]) + _bdot_nt(p["q"] * p["q_fac"], st_ref[p["slot"]])
            for p in ps:
                st_ref[p["slot"]] = st_ref[p["slot"]] * p["st_fac"] + p["update"]
                p["o_ref"][p["rows"], p["cols"]] = p["o"]
        return carry

    assert n % RET_UNROLL == 0
    lax.fori_loop(0, n // RET_UNROLL, body, 0)


def _retention(h_main, cos_tab, sin_tab, *, tb=512):
    s = h_main.shape[0]
    tb = min(tb, s)
    nb = s // tb
    wk, wv = RET_HEADS * RET_DK, RET_HEADS * RET_DV
    fwd = lambda w, cb: pl.BlockSpec((tb, w), lambda i: (i, cb))
    bwd = lambda w, cb: pl.BlockSpec((tb, w), lambda i: (nb - 1 - i, cb))
    return pl.pallas_call(
        functools.partial(_ret_kernel, tb=tb), grid=(nb,),
        in_specs=[fwd(wk, 0), fwd(wk, 1), fwd(wv, 1), fwd(LANES, 0), fwd(LANES, 0),
                  bwd(wk, 0), bwd(wk, 1), bwd(wv, 1), bwd(LANES, 0), bwd(LANES, 0)],
        out_specs=[fwd(wv, 0), bwd(wv, 0)],
        out_shape=[jax.ShapeDtypeStruct((s, wv), F32)] * 2,
        scratch_shapes=[pltpu.VMEM((2 * RET_HEADS, RET_DV, RET_DK), F32)],
        compiler_params=_params(("arbitrary",), 40), name="retention_scan",
    )(h_main, h_main, h_main, cos_tab, sin_tab, h_main, h_main, h_main, cos_tab, sin_tab)


def _gated_head_norm(of_ref, ob_ref, gate_ref, nw_ref, rows, heads, hd):
    o = of_ref[rows, :] + ob_ref[rows, :]
    parts = []
    for h in range(heads):
        y = o[:, h * hd:(h + 1) * hd]
        parts.append(y * lax.rsqrt(jnp.mean(y * y, axis=-1, keepdims=True) + EPS))
    y = jnp.concatenate(parts, axis=-1)
    return y * nw_ref[...] * _silu(gate_ref[rows, :])


def _mixout_kernel(*refs, tm, groups):
    x_ref, w_ref = refs[0], refs[1]
    out_ref, lhs_ref = refs[-2], refs[-1]
    grefs = refs[2:-2]

    def body(r, carry):
        rows = pl.ds(pl.multiple_of(r * NORM_ROWS, NORM_ROWS), NORM_ROWS)
        pos, col = 0, 0
        for g in groups:
            if g is None:
                val, width = grefs[pos][rows, :], grefs[pos].shape[1]
                pos += 1
            else:
                heads, hd = g
                val, width = _gated_head_norm(*grefs[pos:pos + 4], rows, heads, hd), heads * hd
                pos += 4
            lhs_ref[rows, col:col + width] = val.astype(BF16)
            col += width
        return carry
    lax.fori_loop(0, tm // NORM_ROWS, body, 0)

    out_ref[...] = x_ref[...] + jnp.dot(lhs_ref[...], w_ref[...], preferred_element_type=F32)


def _mix_out(x, w_out, group_args, groups, *, tm=256):
    s, d = x.shape
    tm = min(tm, s)
    specs = [pl.BlockSpec((tm, d), lambda i: (i, 0)),
             pl.BlockSpec(w_out.shape, lambda i: (0, 0), pipeline_mode=pl.Buffered(1))]
    arrays = [x, w_out]
    for arr, cb, width, rowvec in group_args:
        arrays.append(arr)
        if rowvec:
            specs.append(pl.BlockSpec((1, width), lambda i: (0, 0)))
        else:
            specs.append(pl.BlockSpec((tm, width), lambda i, cb=cb: (i, cb)))
    return pl.pallas_call(
        functools.partial(_mixout_kernel, tm=tm, groups=groups), grid=(s // tm,),
        in_specs=specs,
        out_specs=pl.BlockSpec((tm, d), lambda i: (i, 0)),
        out_shape=jax.ShapeDtypeStruct((s, d), F32),
        scratch_shapes=[pltpu.VMEM((tm, w_out.shape[0]), BF16)],
        compiler_params=_params(("parallel",), 40), name="mix_out",
    )(*arrays)


def _rot_pair(pr, cos, sin):
    return pr * cos + pltpu.roll(pr, MLA_ROPE, 1) * sin


def _mla_proj_kernel(cq_ref, ckv_ref, kpe_ref, qnw_ref, kvnw_ref, wq_ref, wkv_ref, cos_ref, sin_ref,
                     q_ref, k_ref, v_ref, *, scale):
    cos, sin = cos_ref[...], sin_ref[...]
    cqn = _rms_rows(cq_ref[...], qnw_ref[...]).astype(BF16)
    ckvn = _rms_rows(ckv_ref[...], kvnw_ref[...]).astype(BF16)
    pe = _rot_pair(kpe_ref[...], cos, sin).astype(BF16)
    lane = lax.broadcasted_iota(jnp.int32, pe.shape, 1)
    ones_col = jnp.where(lane == 0, 1.0, 0.0).astype(BF16)
    for h in range(MLA_HEADS):
        cols = slice(h * MLA_QK_PAD, (h + 1) * MLA_QK_PAD)
        yq = jnp.dot(cqn, wq_ref[:, cols], preferred_element_type=F32)
        q_rope = _rot_pair(yq[:, MLA_NOPE:], cos, sin)
        q_ref[h] = (jnp.concatenate([yq[:, :MLA_NOPE], q_rope], axis=1) * scale).astype(BF16)
        ykv = jnp.dot(ckvn, wkv_ref[:, cols], preferred_element_type=F32)
        k_ref[h] = jnp.concatenate([ykv[:, :MLA_NOPE].astype(BF16), pe], axis=1)
        v_ref[h] = jnp.concatenate([ykv[:, MLA_NOPE:].astype(BF16), ones_col], axis=1)


def _mla_proj(h_main, h_side, q_norm_w, kv_norm_w, wq, wkv, cos_tab, sin_tab, *, tm=512):
    s = h_main.shape[0]
    tm = min(tm, s)
    cq_block = (HG_HEADS * (3 * HG_DK + 2 * HG_DV)) // MLA_Q_RANK
    scale = (MLA_NOPE + MLA_ROPE) ** -0.5 * math.log2(math.e)
    full = lambda a: pl.BlockSpec(a.shape, lambda i: (0, 0))
    tab = pl.BlockSpec((tm, LANES), lambda i: (i, 0))
    head_out = pl.BlockSpec((MLA_HEADS, tm, MLA_QK_PAD), lambda i: (0, i, 0))
    return pl.pallas_call(
        functools.partial(_mla_proj_kernel, scale=scale), grid=(s // tm,),
        in_specs=[pl.BlockSpec((tm, MLA_Q_RANK), lambda i: (i, cq_block)),
                  pl.BlockSpec((tm, MLA_KV_RANK), lambda i: (i, 0)),
                  pl.BlockSpec((tm, LANES), lambda i: (i, MLA_KV_RANK // LANES)),
                  full(q_norm_w), full(kv_norm_w), full(wq), full(wkv), tab, tab],
        out_specs=[head_out, head_out, head_out],
        out_shape=[jax.ShapeDtypeStruct((MLA_HEADS, s, MLA_QK_PAD), BF16)] * 3,
        compiler_params=_params(("parallel",), 40), name="mla_proj",
    )(h_main, h_side, h_side, q_norm_w, kv_norm_w, wq, wkv, cos_tab, sin_tab)


ATTN_SPLIT = 2
ATTN_ROWS = 32


def _attn_kernel(q_ref, k_ref, v_ref, o_ref, s_ref, p_ref, m_ref, alpha_ref, acc_ref, *, tk, nk):
    tq = q_ref.shape[1]
    sub = tq // ATTN_SPLIT
    m_ref[...] = jnp.full(m_ref.shape, -jnp.inf, F32)
    acc_ref[...] = jnp.zeros(acc_ref.shape, F32)

    def softmax_rows(i):
        for rb in range(sub // ATTN_ROWS):
            loc = slice(rb * ATTN_ROWS, (rb + 1) * ATTN_ROWS)
            glob = slice(i * sub + rb * ATTN_ROWS, i * sub + (rb + 1) * ATTN_ROWS)
            sc = s_ref[i, loc, :]
            cols = [sc[:, c * LANES:(c + 1) * LANES] for c in range(tk // LANES)]
            col_max = functools.reduce(jnp.maximum, cols)
            m_old = m_ref[glob, :]
            m_new = jnp.maximum(m_old, jnp.max(col_max, axis=-1, keepdims=True))
            m_ref[glob, :] = m_new
            alpha_ref[glob, :] = jnp.exp2(m_old - m_new)
            p_ref[i, loc, :] = jnp.concatenate([jnp.exp2(c - m_new) for c in cols], axis=1).astype(BF16)

    def body(j, carry):
        rows = pl.ds(pl.multiple_of(j * tk, tk), tk)
        k, v = k_ref[0, rows, :], v_ref[0, rows, :]
        for i in range(ATTN_SPLIT):
            s_ref[i] = lax.dot_general(q_ref[0, i * sub:(i + 1) * sub, :], k, NT_DIMS, preferred_element_type=F32)
        for i in range(ATTN_SPLIT):
            softmax_rows(i)
            rs = slice(i * sub, (i + 1) * sub)
            alpha = alpha_ref[rs, :]
            acc_ref[rs, :] = (jnp.concatenate([alpha, alpha], axis=1) * acc_ref[rs, :]
                              + jnp.dot(p_ref[i], v, preferred_element_type=F32))
        return carry

    lax.fori_loop(0, nk, body, 0)
    acc = acc_ref[...]
    o_ref[...] = (acc[:, :MLA_V] / acc[:, MLA_V:MLA_V + 1]).astype(BF16)


def _attention(q, k, v, *, tq=1024, tk=2048):
    heads, s, _ = q.shape
    tq, tk = min(tq, s), min(tk, s)
    sub = tq // ATTN_SPLIT
    return pl.pallas_call(
        functools.partial(_attn_kernel, tk=tk, nk=s // tk), grid=(heads, s // tq),
        in_specs=[pl.BlockSpec((1, tq, MLA_QK_PAD), lambda h, i: (h, i, 0)),
                  pl.BlockSpec((1, s, MLA_QK_PAD), lambda h, i: (h, 0, 0)),
                  pl.BlockSpec((1, s, 2 * MLA_V), lambda h, i: (h, 0, 0))],
        out_specs=pl.BlockSpec((tq, MLA_V), lambda h, i: (i, h)),
        out_shape=jax.ShapeDtypeStruct((s, heads * MLA_V), BF16),
        scratch_shapes=[pltpu.VMEM((ATTN_SPLIT, sub, tk), F32), pltpu.VMEM((ATTN_SPLIT, sub, tk), BF16),
                        pltpu.VMEM((tq, LANES), F32), pltpu.VMEM((tq, LANES), F32),
                        pltpu.VMEM((tq, 2 * MLA_V), F32)],
        compiler_params=_params(("parallel", "arbitrary"), 48), name="mla_attention",
    )(q, k, v)


def _ffn_kernel(x_ref, nw_ref, wu_ref, wd_ref, fw_ref, o_ref, xn_ref, *, tm, nf, final_norm):
    f = pl.program_id(1)

    @pl.when(f == 0)
    def _():
        _norm_rows_to(x_ref, nw_ref, xn_ref, tm)
        o_ref[...] = x_ref[...]

    a = jnp.maximum(jnp.dot(xn_ref[...], wu_ref[0], preferred_element_type=F32), 0.0)
    o_ref[...] += jnp.dot((a * a).astype(BF16), wd_ref[0], preferred_element_type=F32)

    if final_norm:
        @pl.when(f == nf - 1)
        def _():
            def body(r, carry):
                rows = pl.ds(pl.multiple_of(r * NORM_ROWS, NORM_ROWS), NORM_ROWS)
                o_ref[rows, :] = _rms_rows(o_ref[rows, :], fw_ref[...])
                return carry
            lax.fori_loop(0, tm // NORM_ROWS, body, 0)


def _ffn(x, nw, w_up, w_down, final_w, *, layer, final_norm, tm=512, tf=1024):
    s, d = x.shape
    dff = w_up.shape[2]
    tm = min(tm, s)
    nf = dff // tf
    return pl.pallas_call(
        functools.partial(_ffn_kernel, tm=tm, nf=nf, final_norm=final_norm), grid=(s // tm, nf),
        in_specs=[pl.BlockSpec((tm, d), lambda i, f: (i, 0)),
                  pl.BlockSpec((1, d), lambda i, f: (0, 0)),
                  pl.BlockSpec((1, d, tf), lambda i, f: (layer, 0, f)),
                  pl.BlockSpec((1, tf, d), lambda i, f: (layer, f, 0)),
                  pl.BlockSpec((1, d), lambda i, f: (0, 0))],
        out_specs=pl.BlockSpec((tm, d), lambda i, f: (i, 0)),
        out_shape=jax.ShapeDtypeStruct((s, d), F32),
        scratch_shapes=[pltpu.VMEM((tm, d), BF16)],
        compiler_params=_params(("parallel", "arbitrary"), 48), name="ffn",
    )(x, nw, w_up, w_down, final_w)


def _gdn_prep_kernel(x_ref, xp_ref, xn_ref, cw_ref, o_ref, *, tm, nblk):
    i, sec = pl.program_id(0), pl.program_id(1)
    x = x_ref[...]
    prev = jnp.where(i > 0, xp_ref[...], 0.0)
    nxt = jnp.where(i < nblk - 1, xn_ref[...], 0.0)
    rows = lax.broadcasted_iota(jnp.int32, prev.shape, 0)
    half = CONV_WIDTH // 2

    def shifted(d):
        if d == 0:
            return x
        r = pltpu.roll(x, (-d) % tm, 0)
        if d < 0:
            edge = r[:SUBLANES]
            for t in range(-d):
                edge = jnp.where(rows == t, prev[SUBLANES + t + d:SUBLANES + t + d + 1], edge)
            return jnp.concatenate([edge, r[SUBLANES:]], axis=0)
        edge = r[tm - SUBLANES:]
        for t in range(SUBLANES - d, SUBLANES):
            edge = jnp.where(rows == t, nxt[t + d - SUBLANES:t + d - SUBLANES + 1], edge)
        return jnp.concatenate([r[:tm - SUBLANES], edge], axis=0)

    y = shifted(-half) * cw_ref[0:1]
    for j in range(1, CONV_WIDTH):
        y = y + shifted(j - half) * cw_ref[j:j + 1]
    y = _silu(y)
    parts = []
    for h in range(GDN_HEADS):
        a = y[:, h * GDN_DK:(h + 1) * GDN_DK]
        parts.append(a * lax.rsqrt(jnp.sum(a * a, axis=-1, keepdims=True) + EPS))
    nrm = jnp.concatenate(parts, axis=-1) * jnp.where(sec == 0, GDN_DK ** -0.5, 1.0)
    o_ref[...] = jnp.where(sec == 2, y, nrm)


def _gdn_prep(h_main, conv_w, *, tm=256):
    s = h_main.shape[0]
    tm = min(tm, s)
    nblk = s // tm
    w = GDN_HEADS * GDN_DK
    base = (2 * RET_HEADS * RET_DK + 2 * RET_HEADS * RET_DV) // w
    per = tm // SUBLANES
    return pl.pallas_call(
        functools.partial(_gdn_prep_kernel, tm=tm, nblk=nblk), grid=(nblk, 3),
        in_specs=[pl.BlockSpec((tm, w), lambda i, c: (i, base + c)),
                  pl.BlockSpec((SUBLANES, w), lambda i, c: (jnp.maximum(i * per - 1, 0), base + c)),
                  pl.BlockSpec((SUBLANES, w), lambda i, c: (jnp.minimum((i + 1) * per, s // SUBLANES - 1), base + c)),
                  pl.BlockSpec((CONV_WIDTH, w), lambda i, c: (0, c))],
        out_specs=pl.BlockSpec((tm, w), lambda i, c: (i, c)),
        out_shape=jax.ShapeDtypeStruct((s, 3 * w), F32),
        compiler_params=_params(("parallel", "arbitrary"), 32), name="gdn_prep",
    )(h_main, h_main, h_main, conv_w)


def _softplus(x):
    return jnp.maximum(x, 0.0) + jnp.log1p(jnp.exp(-jnp.abs(x)))


def _pair_lhs(x):
    hi = x.astype(BF16)
    lo = (x - hi.astype(F32)).astype(BF16)
    return jnp.concatenate([hi, lo, hi], axis=1)


def _pair_rhs(y, low_lanes):
    hi = y.astype(BF16)
    lo = (y - hi.astype(F32)).astype(BF16)
    zero = jnp.zeros_like(hi)

    def bd(a):
        return jnp.concatenate([jnp.where(low_lanes, a, zero), jnp.where(low_lanes, zero, a)], axis=0)

    bd_hi = bd(hi)
    return jnp.concatenate([bd_hi, bd_hi, bd(lo)], axis=0)


def _pair_lhs2(x):
    hi = x.astype(BF16)
    lo = (x - hi.astype(F32)).astype(BF16)
    return jnp.concatenate([hi, lo], axis=1)


def _pair_rhs1(y, low_lanes):
    hi = y.astype(BF16)
    zero = jnp.zeros_like(hi)
    bd = jnp.concatenate([jnp.where(low_lanes, hi, zero), jnp.where(low_lanes, zero, hi)], axis=0)
    return jnp.concatenate([bd, bd], axis=0)


def _pair_rows(x, low_lanes):
    return jnp.concatenate([jnp.where(low_lanes, x, 0.0), jnp.where(low_lanes, 0.0, x)], axis=0)


def _gdn_kernel(alog_ref, dt_ref, qf_ref, kf_ref, vf_ref, gf_ref, qb_ref, kb_ref, vb_ref, gb_ref,
                of_ref, ob_ref, st_ref, *, tb):
    @pl.when(pl.program_id(0) == 0)
    def _():
        st_ref[...] = jnp.zeros_like(st_ref)

    row = lax.broadcasted_iota(jnp.int32, (CHUNK, LANES), 0)
    lane = lax.broadcasted_iota(jnp.int32, (CHUNK, LANES), 1)
    col = lane & (CHUNK - 1)
    low_lanes = lane < CHUNK
    low_row = low_lanes[0:1]
    causal, anti = row >= col, row <= col
    strict_c, strict_a = row > col, row < col
    tri3_f, tri3_b = _tri3(False), _tri3(True)
    eye = (row == col).astype(F32)
    same16 = (row >> 4) == (col >> 4)
    same32 = (row >> 5) == (col >> 5)
    in32 = jnp.logical_and(same32, jnp.logical_not(same16))
    n = tb // CHUNK
    neg_a = -jnp.exp(alog_ref[...])
    dt = dt_ref[...]

    def pairs(q_ref, k_ref, v_ref, g_ref, o_ref, rows, rev):
        raw = g_ref[rows, :]
        la = neg_a * _softplus(raw + dt)
        beta = jax.nn.sigmoid(raw)
        g = _sdot(tri3_b if rev else tri3_f, _rhs_split3(la))
        g_rows = jnp.concatenate([g, g], axis=0).T
        la0 = GDN_HEADS if rev else 0
        b0 = (3 if rev else 2) * GDN_HEADS
        last = 0 if rev else CHUNK - 1
        out = []
        for j in range(GDN_HEADS // 2):
            hs = (2 * j, 2 * j + 1)
            g_cols = [jnp.broadcast_to(g[:, la0 + h:la0 + h + 1], (CHUNK, LANES)) for h in hs]
            b_cols = [jnp.broadcast_to(beta[:, b0 + h:b0 + h + 1], (CHUNK, LANES)) for h in hs]
            g_col = jnp.where(low_lanes, g_cols[0], g_cols[1])
            g_row = jnp.where(low_row, g_rows[la0 + hs[0]:la0 + hs[0] + 1], g_rows[la0 + hs[1]:la0 + hs[1] + 1])
            out.append(dict(
                q_ref=q_ref, k_ref=k_ref, v_ref=v_ref, o_ref=o_ref, rows=rows,
                cs=[slice(h * GDN_DK, (h + 1) * GDN_DK) for h in hs],
                slots=[(GDN_HEADS if rev else 0) + h for h in hs],
                strict=strict_a if rev else strict_c,
                decay=jnp.exp(jnp.where(anti if rev else causal, g_col - g_row, -jnp.inf)),
                g_cols=g_cols, b_cols=b_cols,
                g_last=[g[last:last + 1, la0 + h:la0 + h + 1] for h in hs]))
        return out

    def body(c, carry):
        rf = pl.ds(pl.multiple_of(c * CHUNK, CHUNK), CHUNK)
        rb = pl.ds(pl.multiple_of((n - 1 - c) * CHUNK, CHUNK), CHUNK)
        ps = (pairs(qf_ref, kf_ref, vf_ref, gf_ref, of_ref, rf, False)
              + pairs(qb_ref, kb_ref, vb_ref, gb_ref, ob_ref, rb, True))
        for p in ps:
            ks = [p["k_ref"][p["rows"], cs] for cs in p["cs"]]
            qs = [p["q_ref"][p["rows"], cs] for cs in p["cs"]]
            p["kb"] = [k * b for k, b in zip(ks, p["b_cols"])]
            lhs = jnp.concatenate([jnp.concatenate([kb, q], axis=0) for kb, q in zip(p["kb"], qs)], axis=1)
            zero = jnp.zeros_like(ks[0])
            k_bd = jnp.concatenate([jnp.concatenate([ks[0], zero], axis=1),
                                    jnp.concatenate([zero, ks[1]], axis=1)], axis=0)
            kq = _bdot_nt(lhs, k_bd)
            a = jnp.where(p["strict"], kq[:CHUNK] * p["decay"], 0.0)
            p["qk"] = kq[CHUNK:] * p["decay"]
            n1 = -jnp.where(same16, a, 0.0)
            p["n_l"], p["n_r"], p["t"] = _pair_lhs2(n1), _pair_rhs1(n1, low_lanes), eye + n1
            p["e32"] = _pair_rhs1(jnp.where(in32, a, 0.0), low_lanes)
            p["e64"] = _pair_rhs1(jnp.where(same32, 0.0, a), low_lanes)
        for step in range(3):
            for p in ps:
                nn = _sdot(p["n_l"], p["n_r"])
                p["n_r"] = _pair_rhs(nn, low_lanes)
                if step < 2:
                    p["n_l"] = _pair_lhs(nn)
            for p in ps:
                p["t"] = p["t"] + _sdot(_pair_lhs(p["t"]), p["n_r"])
        for e in ("e32", "e64"):
            for p in ps:
                p["x"] = _sdot(_pair_lhs2(p["t"]), p[e])
            for p in ps:
                p["t"] = p["t"] - _sdot(_pair_lhs(p["x"]), _pair_rhs(p["t"], low_lanes))
        for p in ps:
            rhs = []
            for cs, kb, b, g in zip(p["cs"], p["kb"], p["b_cols"], p["g_cols"]):
                rhs.append(jnp.concatenate([p["v_ref"][p["rows"], cs] * b, kb * jnp.exp(g)], axis=1))
            p["sol"] = _sdot(_pair_lhs(_pair_rows(p["t"], low_lanes)), _rhs_split(jnp.concatenate(rhs, axis=0)))
        for p in ps:
            p["v_new"], p["o"] = [], []
            for i, (cs, slot, g) in enumerate(zip(p["cs"], p["slots"], p["g_cols"])):
                sol = p["sol"][i * CHUNK:(i + 1) * CHUNK]
                q = p["q_ref"][p["rows"], cs]
                ws = _bdot(jnp.concatenate([sol[:, GDN_DV:], q * jnp.exp(g)], axis=0), st_ref[slot])
                p["v_new"].append(sol[:, :GDN_DV] - ws[:CHUNK])
                p["o"].append(ws[CHUNK:])
        for p in ps:
            o_intra = _bdot(_pair_rows(p["qk"], low_lanes), jnp.concatenate(p["v_new"], axis=0))
            for i, (cs, slot, g, g_last) in enumerate(zip(p["cs"], p["slots"], p["g_cols"], p["g_last"])):
                k = p["k_ref"][p["rows"], cs]
                p["o_ref"][p["rows"], cs] = p["o"][i] + o_intra[i * CHUNK:(i + 1) * CHUNK]
                st_ref[slot] = (st_ref[slot] * jnp.exp(g_last)
                                + _bdot_tn(k * jnp.exp(g_last - g), p["v_new"][i]))
        return carry

    lax.fori_loop(0, n, body, 0)


def _gdn(qkv, gates, alog_row, dt_row, *, tb=512):
    s = qkv.shape[0]
    tb = min(tb, s)
    nb = s // tb
    w = GDN_HEADS * GDN_DK
    fwd = lambda width, cb: pl.BlockSpec((tb, width), lambda i: (i, cb))
    bwd = lambda width, cb: pl.BlockSpec((tb, width), lambda i: (nb - 1 - i, cb))
    row = pl.BlockSpec((1, LANES), lambda i: (0, 0))
    return pl.pallas_call(
        functools.partial(_gdn_kernel, tb=tb), grid=(nb,),
        in_specs=[row, row,
                  fwd(w, 0), fwd(w, 1), fwd(w, 2), fwd(LANES, 0),
                  bwd(w, 0), bwd(w, 1), bwd(w, 2), bwd(LANES, 0)],
        out_specs=[fwd(w, 0), bwd(w, 0)],
        out_shape=[jax.ShapeDtypeStruct((s, w), F32)] * 2,
        scratch_shapes=[pltpu.VMEM((2 * GDN_HEADS, GDN_DK, GDN_DV), F32)],
        compiler_params=_params(("arbitrary",), 40), name="gdn_scan",
    )(alog_row, dt_row, qkv, qkv, qkv, gates, qkv, qkv, qkv, gates)


def _rot_cols(w_pe):
    half = w_pe.shape[-1] // 2
    return jnp.concatenate([-w_pe[..., half:], w_pe[..., :half]], axis=-1)


def _pad_lanes(v, fill=0.0):
    v = v.reshape(1, -1).astype(F32)
    return jnp.pad(v, ((0, 0), (0, LANES - v.shape[1])), constant_values=fill)


def kernel(x, positions, norm_mix_w, norm_ffn_w, final_norm_w, hg_lb_logits, even_w_in, hg_norm_w,
           mla_q_norm_w, mla_w_q_b, mla_kv_norm_w, mla_w_kv_b, even_w_out, odd_w_in, ret_norm_w,
           gdn_conv_w, gdn_a_log, gdn_dt_bias, gdn_norm_w, odd_w_out, ffn_w_up, ffn_w_down):
    b, s, d = x.shape
    assert b == 1 and s % CHUNK == 0
    xs = x.reshape(s, d)
    row = lambda v: v.reshape(1, -1).astype(F32)

    pos_col = positions.reshape(s, 1)
    inv_ret = ROPE_THETA ** (-jnp.arange(RET_DK // 2, dtype=F32) / (RET_DK // 2))
    inv_mla = ROPE_THETA ** (-jnp.arange(MLA_ROPE // 2, dtype=F32) / (MLA_ROPE // 2))
    inv_row = row(jnp.concatenate([inv_ret, inv_mla, jnp.zeros((LANES - RET_DK // 2 - MLA_ROPE // 2,), F32)]))
    cos_ret, sin_ret, cos_mla, sin_mla = _rope_tables(pos_col, inv_row)

    n_main = HG_HEADS * (3 * HG_DK + 2 * HG_DV) + MLA_Q_RANK
    wt_in = even_w_in[0].T.astype(BF16)
    wt_kpe = wt_in[n_main + MLA_KV_RANK:]
    wt_side = jnp.concatenate([wt_in[n_main:], _rot_cols(wt_kpe.T).T], axis=0)
    h_main, h_side = _norm_matmul(xs, row(norm_mix_w[0]), wt_in, n_main, None, wt_side, tn=512)

    o_f, o_b = _hgrn2(h_main, hg_lb_logits.astype(F32), layer=0)

    wq = mla_w_q_b[0].reshape(MLA_Q_RANK, MLA_HEADS, MLA_NOPE + MLA_ROPE)
    wq_pe = wq[..., MLA_NOPE:]
    wq = jnp.concatenate([wq[..., :MLA_NOPE], wq_pe, _rot_cols(wq_pe)], axis=-1)
    wq = wq.reshape(MLA_Q_RANK, MLA_HEADS * MLA_QK_PAD).astype(BF16)
    q, k, v = _mla_proj(h_main, h_side, row(mla_q_norm_w[0]), row(mla_kv_norm_w[0]), wq,
                        mla_w_kv_b[0].astype(BF16), cos_mla, sin_mla)
    o_attn = _attention(q, k, v)

    wa = HG_HEADS * HG_DV
    xs = _mix_out(xs, even_w_out[0].astype(BF16),
                  [(o_f, 0, wa, False), (o_b, 0, wa, False), (h_main, 4, wa, False), (row(hg_norm_w[0]), 0, wa, True),
                   (o_attn, 0, MLA_HEADS * MLA_V, False)],
                  ((HG_HEADS, HG_DV), None))
    w_up, w_down = ffn_w_up.astype(BF16), ffn_w_down.astype(BF16)
    xs = _ffn(xs, row(norm_ffn_w[0]), w_up, w_down, row(final_norm_w), layer=0, final_norm=False)

    n_ret = 2 * RET_HEADS * RET_DK + 2 * RET_HEADS * RET_DV
    n_qkv = GDN_HEADS * (2 * GDN_DK + GDN_DV)
    n_gate = 4 * GDN_HEADS
    wt_in = odd_w_in[0].T.astype(BF16)
    wt_gate = wt_in[n_ret + n_qkv + n_gate:]
    wt_side = jnp.pad(wt_in[n_ret + n_qkv:n_ret + n_qkv + n_gate], ((0, LANES - n_gate), (0, 0)))
    h_main, h_gates = _norm_matmul(xs, row(norm_mix_w[1]), wt_in, n_ret + n_qkv, wt_gate, wt_side, tn=1024)

    r_f, r_b = _retention(h_main, cos_ret, sin_ret)
    qkv = _gdn_prep(h_main, gdn_conv_w[0].astype(F32))
    g_f, g_b = _gdn(qkv, h_gates, _pad_lanes(gdn_a_log[0]), _pad_lanes(gdn_dt_bias[0]))

    wr, wg = RET_HEADS * RET_DV, GDN_HEADS * GDN_DV
    xs = _mix_out(xs, odd_w_out[0].astype(BF16),
                  [(r_f, 0, wr, False), (r_b, 0, wr, False), (h_main, 2, wr, False), (row(ret_norm_w[0]), 0, wr, True),
                   (g_f, 0, wg, False), (g_b, 0, wg, False), (h_main, 6, wg, False), (row(gdn_norm_w[0]), 0, wg, True)],
                  ((RET_HEADS, RET_DV), (GDN_HEADS, GDN_DV)))
    xs = _ffn(xs, row(norm_ffn_w[1]), w_up, w_down, row(final_norm_w), layer=1, final_norm=True)
    return xs.reshape(b, s, d)
```

```python
import functools
import math

import jax
import jax.numpy as jnp
from jax import lax
from jax.experimental import pallas as pl
from jax.experimental.pallas import tpu as pltpu

F32 = jnp.float32
BF16 = jnp.bfloat16

EPS = 1e-6
CHUNK = 64
ROPE_THETA = 10000.0
LANES = 128
SUBLANES = 8

HG_HEADS, HG_DK, HG_DV = 8, 128, 128
MLA_HEADS, MLA_NOPE, MLA_ROPE, MLA_V = 8, 128, 64, 128
MLA_Q_RANK, MLA_KV_RANK = 512, 256
MLA_QK_PAD = 256
RET_HEADS, RET_DK, RET_DV = 4, 128, 256
RET_UNROLL = 2
GDN_HEADS, GDN_DK, GDN_DV = 8, 128, 128
CONV_WIDTH = 5

NT_DIMS = (((1,), (1,)), ((), ()))
TN_DIMS = (((0,), (0,)), ((), ()))


def _params(semantics, vmem_mib):
    return pltpu.CompilerParams(dimension_semantics=semantics, vmem_limit_bytes=vmem_mib * 1024 * 1024)


def _rms_rows(x, w):
    ms = jnp.mean(x * x, axis=-1, keepdims=True)
    return x * lax.rsqrt(ms + EPS) * w


def _silu(x):
    return x * jax.nn.sigmoid(x)


def _bdot(a, b):
    return jnp.dot(a.astype(BF16), b.astype(BF16), preferred_element_type=F32)


def _bdot_nt(a, b):
    return lax.dot_general(a.astype(BF16), b.astype(BF16), NT_DIMS, preferred_element_type=F32)


def _bdot_tn(a, b):
    return lax.dot_general(a.astype(BF16), b.astype(BF16), TN_DIMS, preferred_element_type=F32)


def _rhs_split(b):
    hi = b.astype(BF16)
    lo = (b - hi.astype(F32)).astype(BF16)
    return jnp.concatenate([hi, hi, lo], axis=0)


def _rhs_split3(b):
    b1 = b.astype(BF16)
    r = b - b1.astype(F32)
    b2 = r.astype(BF16)
    b3 = (r - b2.astype(F32)).astype(BF16)
    return jnp.concatenate([b1, b2, b3], axis=0)


def _sdot(lhs_split, rhs_split):
    return jnp.dot(lhs_split, rhs_split, preferred_element_type=F32)


def _tri3(rev):
    row = lax.broadcasted_iota(jnp.int32, (CHUNK, 3 * CHUNK), 0)
    col = lax.broadcasted_iota(jnp.int32, (CHUNK, 3 * CHUNK), 1) & (CHUNK - 1)
    return jnp.where(row <= col if rev else row >= col, 1.0, 0.0).astype(BF16)


def _rope_table_kernel(pos_ref, inv_ref, cos_ret_ref, sin_ret_ref, cos_mla_ref, sin_mla_ref):
    ang = pos_ref[...].astype(F32) * inv_ref[...]
    cos, sin = jnp.cos(ang), jnp.sin(ang)
    lane = lax.broadcasted_iota(jnp.int32, cos.shape, 1)
    half, quarter = LANES // 2, LANES // 4
    cos_ret_ref[...] = jnp.where(lane < half, cos, pltpu.roll(cos, half, 1))
    sin_ret_ref[...] = jnp.where(lane < half, -sin, pltpu.roll(sin, half, 1))

    def mla(t):
        return jnp.where(lane < quarter, pltpu.roll(t, half, 1),
                         jnp.where(lane < half, pltpu.roll(t, half + quarter, 1), 0.0))

    cos_mla_ref[...] = mla(cos)
    sin_mla_ref[...] = mla(sin)


def _rope_tables(pos_col, inv_row):
    s = pos_col.shape[0]
    tm = min(512, s)
    tab = pl.BlockSpec((tm, LANES), lambda i: (i, 0))
    return pl.pallas_call(
        _rope_table_kernel, grid=(s // tm,),
        in_specs=[pl.BlockSpec((tm, 1), lambda i: (i, 0)), pl.BlockSpec((1, LANES), lambda i: (0, 0))],
        out_specs=[tab] * 4,
        out_shape=[jax.ShapeDtypeStruct((s, LANES), F32)] * 4,
        compiler_params=_params(("parallel",), 16), name="rope_tables",
    )(pos_col, inv_row)


NORM_ROWS = 128


def _norm_rows_to(x_ref, nw_ref, xn_ref, tm):
    def body(r, carry):
        rows = pl.ds(pl.multiple_of(r * NORM_ROWS, NORM_ROWS), NORM_ROWS)
        xn_ref[rows, :] = _rms_rows(x_ref[rows, :], nw_ref[...]).astype(BF16)
        return carry
    lax.fori_loop(0, tm // NORM_ROWS, body, 0)


def _norm_mm_kernel(*refs, tm, n_a):
    x_ref, nw_ref, wa_ref = refs[:3]
    wb_ref = refs[3] if len(refs) == 8 else None
    wst_ref, o_ref, os_ref, xn_ref = refs[-4:]
    j = pl.program_id(1)

    @pl.when(j == 0)
    def _():
        _norm_rows_to(x_ref, nw_ref, xn_ref, tm)
        os_ref[...] = lax.dot_general(xn_ref[...], wst_ref[...], NT_DIMS, preferred_element_type=F32)

    def project(wt_ref):
        o_ref[...] = lax.dot_general(xn_ref[...], wt_ref[...], NT_DIMS, preferred_element_type=F32)

    if wb_ref is None:
        project(wa_ref)
    else:
        pl.when(j < n_a)(lambda: project(wa_ref))
        pl.when(j >= n_a)(lambda: project(wb_ref))


def _norm_matmul(x, nw, wt_a, n_a_rows, wt_b, wt_side, *, tm=1024, tn=512):
    s, d = x.shape
    ns = wt_side.shape[0]
    tm = min(tm, s)
    n_a = n_a_rows // tn
    n = n_a_rows + (0 if wt_b is None else wt_b.shape[0])
    w_specs = [pl.BlockSpec((tn, d), lambda i, j: (jnp.minimum(j, n_a - 1), 0))]
    weights = [wt_a]
    if wt_b is not None:
        w_specs.append(pl.BlockSpec((tn, d), lambda i, j: (jnp.maximum(j - n_a, 0), 0),
                                    pipeline_mode=pl.Buffered(1)))
        weights.append(wt_b)
    return pl.pallas_call(
        functools.partial(_norm_mm_kernel, tm=tm, n_a=n_a), grid=(s // tm, n // tn),
        in_specs=[pl.BlockSpec((tm, d), lambda i, j: (i, 0)),
                  pl.BlockSpec((1, d), lambda i, j: (0, 0)),
                  *w_specs,
                  pl.BlockSpec((ns, d), lambda i, j: (0, 0))],
        out_specs=[pl.BlockSpec((tm, tn), lambda i, j: (i, j)),
                   pl.BlockSpec((tm, ns), lambda i, j: (i, 0))],
        out_shape=[jax.ShapeDtypeStruct((s, n), F32), jax.ShapeDtypeStruct((s, ns), F32)],
        scratch_shapes=[pltpu.VMEM((tm, d), BF16)],
        compiler_params=_params(("parallel", "arbitrary"), 52), name="norm_matmul",
    )(x, nw, *weights, wt_side)


def _gla_chunks(ps, st_ref):
    for p in ps:
        p["qe"] = p["q"] * jnp.exp(p["b"] - p["b_mid"])
        p["ke"] = p["k"] * jnp.exp(p["b_mid"] - p["b"])
        p["scores"] = jnp.where(p["mask"], _bdot_nt(p["qe"], p["ke"]), 0.0)
    for p in ps:
        q_dec = p["qe"] * jnp.exp(p["b_mid"])
        p["o"] = _bdot(p["scores"], p["v"]) + _bdot_nt(q_dec, st_ref[p["slot"]])
    for p in ps:
        k_dec = p["ke"] * jnp.exp(p["b_last"] - p["b_mid"])
        st_ref[p["slot"]] = st_ref[p["slot"]] * jnp.exp(p["b_last"]) + _bdot_tn(p["v"], k_dec)
        p["o_ref"][p["rows"], p["cols"]] = p["o"]


def _chunk_masks():
    row = lax.broadcasted_iota(jnp.int32, (CHUNK, CHUNK), 0)
    col = lax.broadcasted_iota(jnp.int32, (CHUNK, CHUNK), 1)
    return row, col


def _hgrn_kernel(logit_ref, qf_ref, ff_ref, vf_ref, qb_ref, fb_ref, vb_ref, of_ref, ob_ref, st_ref,
                 *, tb, layer):
    @pl.when(pl.program_id(0) == 0)
    def _():
        st_ref[...] = jnp.zeros_like(st_ref)

    lg = logit_ref[...]
    e = jnp.exp(lg - jnp.max(lg, axis=0, keepdims=True))
    lb = jnp.sum(e[0:layer + 1], axis=0, keepdims=True) / jnp.sum(e, axis=0, keepdims=True)
    row, col = _chunk_masks()
    causal, anti = row >= col, row <= col
    tri3_f, tri3_b = _tri3(False), _tri3(True)
    n = tb // CHUNK
    mid = CHUNK // 2
    scale = HG_DK ** -0.5

    def problem(q_ref, f_ref, v_ref, o_ref, rows, h, rev):
        cs = slice(h * HG_DK, (h + 1) * HG_DK)
        lbh = lb[:, cs]
        sig = jax.nn.sigmoid(f_ref[rows, cs])
        f = lbh + (1.0 - lbh) * sig
        b = _sdot(tri3_b if rev else tri3_f, _rhs_split3(jnp.log(f)))
        if rev:
            b_last, b_mid = b[0:1], b[CHUNK - 1 - mid:CHUNK - mid]
        else:
            b_last, b_mid = b[CHUNK - 1:CHUNK], b[mid:mid + 1]
        return dict(q=_silu(q_ref[rows, cs]) * scale, k=(1.0 - lbh) * (1.0 - sig), v=v_ref[rows, cs],
                    b=b, b_mid=b_mid, b_last=b_last, mask=anti if rev else causal,
                    slot=(HG_HEADS if rev else 0) + h, o_ref=o_ref, rows=rows, cols=cs)

    def body(c, carry):
        rf = pl.ds(pl.multiple_of(c * CHUNK, CHUNK), CHUNK)
        rb = pl.ds(pl.multiple_of((n - 1 - c) * CHUNK, CHUNK), CHUNK)
        ps = [problem(qf_ref, ff_ref, vf_ref, of_ref, rf, h, False) for h in range(HG_HEADS)]
        ps += [problem(qb_ref, fb_ref, vb_ref, ob_ref, rb, h, True) for h in range(HG_HEADS)]
        _gla_chunks(ps, st_ref)
        return carry

    lax.fori_loop(0, n, body, 0)


def _hgrn2(h_main, lb_logits, *, layer, tb=512):
    s = h_main.shape[0]
    tb = min(tb, s)
    nb = s // tb
    w = HG_HEADS * HG_DK
    fwd = lambda cb: pl.BlockSpec((tb, w), lambda i: (i, cb))
    bwd = lambda cb: pl.BlockSpec((tb, w), lambda i: (nb - 1 - i, cb))
    return pl.pallas_call(
        functools.partial(_hgrn_kernel, tb=tb, layer=layer), grid=(nb,),
        in_specs=[pl.BlockSpec(lb_logits.shape, lambda i: (0, 0)),
                  fwd(0), fwd(1), fwd(3), bwd(0), bwd(2), bwd(3)],
        out_specs=[fwd(0), bwd(0)],
        out_shape=[jax.ShapeDtypeStruct((s, w), F32)] * 2,
        scratch_shapes=[pltpu.VMEM((2 * HG_HEADS, HG_DV, HG_DK), F32)],
        compiler_params=_params(("arbitrary",), 40), name="hgrn2_scan",
    )(lb_logits, h_main, h_main, h_main, h_main, h_main, h_main)


def _ret_kernel(qf_ref, kf_ref, vf_ref, cf_ref, sf_ref, qb_ref, kb_ref, vb_ref, cb_ref, sb_ref,
                of_ref, ob_ref, st_ref, *, tb):
    @pl.when(pl.program_id(0) == 0)
    def _():
        st_ref[...] = jnp.zeros_like(st_ref)

    row, col = _chunk_masks()
    n = tb // CHUNK
    scale = RET_DK ** -0.5
    t_idx = lax.broadcasted_iota(jnp.int32, (CHUNK, RET_DK), 0).astype(F32)
    log_gamma = [math.log1p(-2.0 ** (-5 - h)) for h in range(RET_HEADS)]

    def rope(x, cos, sin):
        return x * cos + pltpu.roll(x, RET_DK // 2, 1) * sin

    def decay_tables(h, rev):
        lgam = log_gamma[RET_HEADS - 1 - h] if rev else log_gamma[h]
        steps = (CHUNK - t_idx) if rev else (t_idx + 1.0)
        dist = ((col - row) if rev else (row - col)).astype(F32)
        return dict(intra=jnp.where(dist >= 0.0, jnp.exp(dist * lgam), 0.0),
                    q_fac=jnp.exp(steps * lgam), k_fac=jnp.exp((CHUNK - steps) * lgam),
                    st_fac=math.exp(CHUNK * lgam))

    tables = {(h, rev): decay_tables(h, rev) for h in range(RET_HEADS) for rev in (False, True)}

    def problem(q_ref, k_ref, v_ref, c_ref, s_ref, o_ref, rows, h, rev):
        cs = slice(h * RET_DK, (h + 1) * RET_DK)
        vs = slice(h * RET_DV, (h + 1) * RET_DV)
        cos, sin = c_ref[rows, :], s_ref[rows, :]
        return dict(q=rope(q_ref[rows, cs], cos, sin) * scale, k=rope(k_ref[rows, cs], cos, sin),
                    v=v_ref[rows, vs], slot=(RET_HEADS if rev else 0) + h, o_ref=o_ref, rows=rows, cols=vs,
                    **tables[(h, rev)])

    def body(c, carry):
        groups = []
        for u in range(RET_UNROLL):
            rf = pl.ds(pl.multiple_of((RET_UNROLL * c + u) * CHUNK, CHUNK), CHUNK)
            rb = pl.ds(pl.multiple_of((n - 1 - RET_UNROLL * c - u) * CHUNK, CHUNK), CHUNK)
            groups.append(
                [problem(qf_ref, kf_ref, vf_ref, cf_ref, sf_ref, of_ref, rf, h, False) for h in range(RET_HEADS)]
                + [problem(qb_ref, kb_ref, vb_ref, cb_ref, sb_ref, ob_ref, rb, h, True) for h in range(RET_HEADS)])
        for ps in groups:
            for p in ps:
                p["scores"] = _bdot_nt(p["q"], p["k"]) * p["intra"]
                p["update"] = _bdot_tn(p["v"], p["k"] * p["k_fac"])
        for ps in groups:
            for p in ps:
                p["o"] = _bdot(p["scores"], p["v"]) + _bdot_nt(p["q"] * p["q_fac"], st_ref[p["slot"]])
            for p in ps:
                st_ref[p["slot"]] = st_ref[p["slot"]] * p["st_fac"] + p["update"]
                p["o_ref"][p["rows"], p["cols"]] = p["o"]
        return carry

    assert n % RET_UNROLL == 0
    lax.fori_loop(0, n // RET_UNROLL, body, 0)


def _retention(h_main, cos_tab, sin_tab, *, tb=512):
    s = h_main.shape[0]
    tb = min(tb, s)
    nb = s // tb
    wk, wv = RET_HEADS * RET_DK, RET_HEADS * RET_DV
    fwd = lambda w, cb: pl.BlockSpec((tb, w), lambda i: (i, cb))
    bwd = lambda w, cb: pl.BlockSpec((tb, w), lambda i: (nb - 1 - i, cb))
    return pl.pallas_call(
        functools.partial(_ret_kernel, tb=tb), grid=(nb,),
        in_specs=[fwd(wk, 0), fwd(wk, 1), fwd(wv, 1), fwd(LANES, 0), fwd(LANES, 0),
                  bwd(wk, 0), bwd(wk, 1), bwd(wv, 1), bwd(LANES, 0), bwd(LANES, 0)],
        out_specs=[fwd(wv, 0), bwd(wv, 0)],
        out_shape=[jax.ShapeDtypeStruct((s, wv), F32)] * 2,
        scratch_shapes=[pltpu.VMEM((2 * RET_HEADS, RET_DV, RET_DK), F32)],
        compiler_params=_params(("arbitrary",), 40), name="retention_scan",
    )(h_main, h_main, h_main, cos_tab, sin_tab, h_main, h_main, h_main, cos_tab, sin_tab)


def _gated_head_norm(of_ref, ob_ref, gate_ref, nw_ref, rows, heads, hd):
    o = of_ref[rows, :] + ob_ref[rows, :]
    parts = []
    for h in range(heads):
        y = o[:, h * hd:(h + 1) * hd]
        parts.append(y * lax.rsqrt(jnp.mean(y * y, axis=-1, keepdims=True) + EPS))
    y = jnp.concatenate(parts, axis=-1)
    return y * nw_ref[...] * _silu(gate_ref[rows, :])


def _mixout_kernel(*refs, tm, groups):
    x_ref, w_ref = refs[0], refs[1]
    out_ref, lhs_ref = refs[-2], refs[-1]
    grefs = refs[2:-2]

    def body(r, carry):
        rows = pl.ds(pl.multiple_of(r * NORM_ROWS, NORM_ROWS), NORM_ROWS)
        pos, col = 0, 0
        for g in groups:
            if g is None:
                val, width = grefs[pos][rows, :], grefs[pos].shape[1]
                pos += 1
            else:
                heads, hd = g
                val, width = _gated_head_norm(*grefs[pos:pos + 4], rows, heads, hd), heads * hd
                pos += 4
            lhs_ref[rows, col:col + width] = val.astype(BF16)
            col += width
        return carry
    lax.fori_loop(0, tm // NORM_ROWS, body, 0)

    out_ref[...] = x_ref[...] + jnp.dot(lhs_ref[...], w_ref[...], preferred_element_type=F32)


def _mix_out(x, w_out, group_args, groups, *, tm=256):
    s, d = x.shape
    tm = min(tm, s)
    specs = [pl.BlockSpec((tm, d), lambda i: (i, 0)),
             pl.BlockSpec(w_out.shape, lambda i: (0, 0), pipeline_mode=pl.Buffered(1))]
    arrays = [x, w_out]
    for arr, cb, width, rowvec in group_args:
        arrays.append(arr)
        if rowvec:
            specs.append(pl.BlockSpec((1, width), lambda i: (0, 0)))
        else:
            specs.append(pl.BlockSpec((tm, width), lambda i, cb=cb: (i, cb)))
    return pl.pallas_call(
        functools.partial(_mixout_kernel, tm=tm, groups=groups), grid=(s // tm,),
        in_specs=specs,
        out_specs=pl.BlockSpec((tm, d), lambda i: (i, 0)),
        out_shape=jax.ShapeDtypeStruct((s, d), F32),
        scratch_shapes=[pltpu.VMEM((tm, w_out.shape[0]), BF16)],
        compiler_params=_params(("parallel",), 40), name="mix_out",
    )(*arrays)


def _rot_pair(pr, cos, sin):
    return pr * cos + pltpu.roll(pr, MLA_ROPE, 1) * sin


def _mla_proj_kernel(cq_ref, ckv_ref, kpe_ref, qnw_ref, kvnw_ref, wq_ref, wkv_ref, cos_ref, sin_ref,
                     q_ref, k_ref, v_ref, *, scale):
    cos, sin = cos_ref[...], sin_ref[...]
    cqn = _rms_rows(cq_ref[...], qnw_ref[...]).astype(BF16)
    ckvn = _rms_rows(ckv_ref[...], kvnw_ref[...]).astype(BF16)
    pe = _rot_pair(kpe_ref[...], cos, sin).astype(BF16)
    lane = lax.broadcasted_iota(jnp.int32, pe.shape, 1)
    ones_col = jnp.where(lane == 0, 1.0, 0.0).astype(BF16)
    for h in range(MLA_HEADS):
        cols = slice(h * MLA_QK_PAD, (h + 1) * MLA_QK_PAD)
        yq = jnp.dot(cqn, wq_ref[:, cols], preferred_element_type=F32)
        q_rope = _rot_pair(yq[:, MLA_NOPE:], cos, sin)
        q_ref[h] = (jnp.concatenate([yq[:, :MLA_NOPE], q_rope], axis=1) * scale).astype(BF16)
        ykv = jnp.dot(ckvn, wkv_ref[:, cols], preferred_element_type=F32)
        k_ref[h] = jnp.concatenate([ykv[:, :MLA_NOPE].astype(BF16), pe], axis=1)
        v_ref[h] = jnp.concatenate([ykv[:, MLA_NOPE:].astype(BF16), ones_col], axis=1)


def _mla_proj(h_main, h_side, q_norm_w, kv_norm_w, wq, wkv, cos_tab, sin_tab, *, tm=512):
    s = h_main.shape[0]
    tm = min(tm, s)
    cq_block = (HG_HEADS * (3 * HG_DK + 2 * HG_DV)) // MLA_Q_RANK
    scale = (MLA_NOPE + MLA_ROPE) ** -0.5 * math.log2(math.e)
    full = lambda a: pl.BlockSpec(a.shape, lambda i: (0, 0))
    tab = pl.BlockSpec((tm, LANES), lambda i: (i, 0))
    head_out = pl.BlockSpec((MLA_HEADS, tm, MLA_QK_PAD), lambda i: (0, i, 0))
    return pl.pallas_call(
        functools.partial(_mla_proj_kernel, scale=scale), grid=(s // tm,),
        in_specs=[pl.BlockSpec((tm, MLA_Q_RANK), lambda i: (i, cq_block)),
                  pl.BlockSpec((tm, MLA_KV_RANK), lambda i: (i, 0)),
                  pl.BlockSpec((tm, LANES), lambda i: (i, MLA_KV_RANK // LANES)),
                  full(q_norm_w), full(kv_norm_w), full(wq), full(wkv), tab, tab],
        out_specs=[head_out, head_out, head_out],
        out_shape=[jax.ShapeDtypeStruct((MLA_HEADS, s, MLA_QK_PAD), BF16)] * 3,
        compiler_params=_params(("parallel",), 40), name="mla_proj",
    )(h_main, h_side, h_side, q_norm_w, kv_norm_w, wq, wkv, cos_tab, sin_tab)


ATTN_SPLIT = 2
ATTN_ROWS = 32


def _attn_kernel(q_ref, k_ref, v_ref, *refs, tk, nk, n_cast):
    cast_in, o_ref, cast_out = refs[:n_cast], refs[n_cast], refs[n_cast + 1:2 * n_cast + 1]
    s_ref, p_ref, m_ref, alpha_ref, acc_ref = refs[2 * n_cast + 1:]
    tq = q_ref.shape[1]
    sub = tq // ATTN_SPLIT
    m_ref[...] = jnp.full(m_ref.shape, -jnp.inf, F32)
    acc_ref[...] = jnp.zeros(acc_ref.shape, F32)

    def softmax_rows(i):
        for rb in range(sub // ATTN_ROWS):
            loc = slice(rb * ATTN_ROWS, (rb + 1) * ATTN_ROWS)
            glob = slice(i * sub + rb * ATTN_ROWS, i * sub + (rb + 1) * ATTN_ROWS)
            sc = s_ref[i, loc, :]
            cols = [sc[:, c * LANES:(c + 1) * LANES] for c in range(tk // LANES)]
            col_max = functools.reduce(jnp.maximum, cols)
            m_old = m_ref[glob, :]
            m_new = jnp.maximum(m_old, jnp.max(col_max, axis=-1, keepdims=True))
            m_ref[glob, :] = m_new
            alpha_ref[glob, :] = jnp.exp2(m_old - m_new)
            p_ref[i, loc, :] = jnp.concatenate([jnp.exp2(c - m_new) for c in cols], axis=1).astype(BF16)

    def body(j, carry):
        rows = pl.ds(pl.multiple_of(j * tk, tk), tk)
        k, v = k_ref[0, rows, :], v_ref[0, rows, :]
        for i in range(ATTN_SPLIT):
            s_ref[i] = lax.dot_general(q_ref[0, i * sub:(i + 1) * sub, :], k, NT_DIMS, preferred_element_type=F32)
        for i in range(ATTN_SPLIT):
            softmax_rows(i)
            rs = slice(i * sub, (i + 1) * sub)
            alpha = alpha_ref[rs, :]
            acc_ref[rs, :] = (jnp.concatenate([alpha, alpha], axis=1) * acc_ref[rs, :]
                              + jnp.dot(p_ref[i], v, preferred_element_type=F32))
        for src, dst in zip(cast_in, cast_out):
            piece = src.shape[0] // nk
            part = pl.ds(pl.multiple_of(j * piece, piece), piece)
            dst[part, :] = src[part, :].astype(BF16)
        return carry

    lax.fori_loop(0, nk, body, 0)
    acc = acc_ref[...]
    o_ref[...] = (acc[:, :MLA_V] / acc[:, MLA_V:MLA_V + 1]).astype(BF16)


CAST_COLS = 1024


def _attention(q, k, v, to_cast, *, tq=1024, tk=2048):
    heads, s, _ = q.shape
    tq, tk = min(tq, s), min(tk, s)
    sub = tq // ATTN_SPLIT
    nq, nk = s // tq, s // tk
    flat = [a.reshape(-1, CAST_COLS) for a in to_cast]
    blocks = [a.shape[0] // (heads * nq) for a in flat]
    assert all(a.shape[0] == b * heads * nq and b % (2 * SUBLANES * nk) == 0 for a, b in zip(flat, blocks))
    cast_specs = [pl.BlockSpec((b, CAST_COLS), lambda h, i: (h * nq + i, 0)) for b in blocks]
    outs = pl.pallas_call(
        functools.partial(_attn_kernel, tk=tk, nk=nk, n_cast=len(flat)), grid=(heads, nq),
        in_specs=[pl.BlockSpec((1, tq, MLA_QK_PAD), lambda h, i: (h, i, 0)),
                  pl.BlockSpec((1, s, MLA_QK_PAD), lambda h, i: (h, 0, 0)),
                  pl.BlockSpec((1, s, 2 * MLA_V), lambda h, i: (h, 0, 0))] + cast_specs,
        out_specs=[pl.BlockSpec((tq, MLA_V), lambda h, i: (i, h))] + cast_specs,
        out_shape=[jax.ShapeDtypeStruct((s, heads * MLA_V), BF16)]
                  + [jax.ShapeDtypeStruct(a.shape, BF16) for a in flat],
        scratch_shapes=[pltpu.VMEM((ATTN_SPLIT, sub, tk), F32), pltpu.VMEM((ATTN_SPLIT, sub, tk), BF16),
                        pltpu.VMEM((tq, LANES), F32), pltpu.VMEM((tq, LANES), F32),
                        pltpu.VMEM((tq, 2 * MLA_V), F32)],
        compiler_params=_params(("parallel", "arbitrary"), 56), name="mla_attention",
    )(q, k, v, *flat)
    return outs[0], [o.reshape(a.shape) for o, a in zip(outs[1:], to_cast)]


def _ffn_kernel(x_ref, nw_ref, wu_ref, wd_ref, fw_ref, o_ref, xn_ref, *, tm, nf, final_norm):
    f = pl.program_id(1)

    @pl.when(f == 0)
    def _():
        _norm_rows_to(x_ref, nw_ref, xn_ref, tm)
        o_ref[...] = x_ref[...]

    a = jnp.maximum(jnp.dot(xn_ref[...], wu_ref[0], preferred_element_type=F32), 0.0)
    o_ref[...] += jnp.dot((a * a).astype(BF16), wd_ref[0], preferred_element_type=F32)

    if final_norm:
        @pl.when(f == nf - 1)
        def _():
            def body(r, carry):
                rows = pl.ds(pl.multiple_of(r * NORM_ROWS, NORM_ROWS), NORM_ROWS)
                o_ref[rows, :] = _rms_rows(o_ref[rows, :], fw_ref[...])
                return carry
            lax.fori_loop(0, tm // NORM_ROWS, body, 0)


def _ffn(x, nw, w_up, w_down, final_w, *, layer, final_norm, tm=512, tf=1024):
    s, d = x.shape
    dff = w_up.shape[2]
    tm = min(tm, s)
    nf = dff // tf
    return pl.pallas_call(
        functools.partial(_ffn_kernel, tm=tm, nf=nf, final_norm=final_norm), grid=(s // tm, nf),
        in_specs=[pl.BlockSpec((tm, d), lambda i, f: (i, 0)),
                  pl.BlockSpec((1, d), lambda i, f: (0, 0)),
                  pl.BlockSpec((1, d, tf), lambda i, f: (layer, 0, f)),
                  pl.BlockSpec((1, tf, d), lambda i, f: (layer, f, 0)),
                  pl.BlockSpec((1, d), lambda i, f: (0, 0))],
        out_specs=pl.BlockSpec((tm, d), lambda i, f: (i, 0)),
        out_shape=jax.ShapeDtypeStruct((s, d), F32),
        scratch_shapes=[pltpu.VMEM((tm, d), BF16)],
        compiler_params=_params(("parallel", "arbitrary"), 48), name="ffn",
    )(x, nw, w_up, w_down, final_w)


def _gdn_prep_kernel(x_ref, xp_ref, xn_ref, cw_ref, o_ref, *, tm, nblk):
    i, sec = pl.program_id(0), pl.program_id(1)
    x = x_ref[...]
    prev = jnp.where(i > 0, xp_ref[...], 0.0)
    nxt = jnp.where(i < nblk - 1, xn_ref[...], 0.0)
    rows = lax.broadcasted_iota(jnp.int32, prev.shape, 0)
    half = CONV_WIDTH // 2

    def shifted(d):
        if d == 0:
            return x
        r = pltpu.roll(x, (-d) % tm, 0)
        if d < 0:
            edge = r[:SUBLANES]
            for t in range(-d):
                edge = jnp.where(rows == t, prev[SUBLANES + t + d:SUBLANES + t + d + 1], edge)
            return jnp.concatenate([edge, r[SUBLANES:]], axis=0)
        edge = r[tm - SUBLANES:]
        for t in range(SUBLANES - d, SUBLANES):
            edge = jnp.where(rows == t, nxt[t + d - SUBLANES:t + d - SUBLANES + 1], edge)
        return jnp.concatenate([r[:tm - SUBLANES], edge], axis=0)

    y = shifted(-half) * cw_ref[0:1]
    for j in range(1, CONV_WIDTH):
        y = y + shifted(j - half) * cw_ref[j:j + 1]
    y = _silu(y)
    parts = []
    for h in range(GDN_HEADS):
        a = y[:, h * GDN_DK:(h + 1) * GDN_DK]
        parts.append(a * lax.rsqrt(jnp.sum(a * a, axis=-1, keepdims=True) + EPS))
    nrm = jnp.concatenate(parts, axis=-1) * jnp.where(sec == 0, GDN_DK ** -0.5, 1.0)
    o_ref[...] = jnp.where(sec == 2, y, nrm)


def _gdn_prep(h_main, conv_w, *, tm=256):
    s = h_main.shape[0]
    tm = min(tm, s)
    nblk = s // tm
    w = GDN_HEADS * GDN_DK
    base = (2 * RET_HEADS * RET_DK + 2 * RET_HEADS * RET_DV) // w
    per = tm // SUBLANES
    return pl.pallas_call(
        functools.partial(_gdn_prep_kernel, tm=tm, nblk=nblk), grid=(nblk, 3),
        in_specs=[pl.BlockSpec((tm, w), lambda i, c: (i, base + c)),
                  pl.BlockSpec((SUBLANES, w), lambda i, c: (jnp.maximum(i * per - 1, 0), base + c)),
                  pl.BlockSpec((SUBLANES, w), lambda i, c: (jnp.minimum((i + 1) * per, s // SUBLANES - 1), base + c)),
                  pl.BlockSpec((CONV_WIDTH, w), lambda i, c: (0, c))],
        out_specs=pl.BlockSpec((tm, w), lambda i, c: (i, c)),
        out_shape=jax.ShapeDtypeStruct((s, 3 * w), F32),
        compiler_params=_params(("parallel", "arbitrary"), 32), name="gdn_prep",
    )(h_main, h_main, h_main, conv_w)


def _softplus(x):
    return jnp.maximum(x, 0.0) + jnp.log1p(jnp.exp(-jnp.abs(x)))


def _pair_lhs(x):
    hi = x.astype(BF16)
    lo = (x - hi.astype(F32)).astype(BF16)
    return jnp.concatenate([hi, lo, hi], axis=1)


def _pair_rhs(y, low_lanes):
    hi = y.astype(BF16)
    lo = (y - hi.astype(F32)).astype(BF16)
    zero = jnp.zeros_like(hi)

    def bd(a):
        return jnp.concatenate([jnp.where(low_lanes, a, zero), jnp.where(low_lanes, zero, a)], axis=0)

    bd_hi = bd(hi)
    return jnp.concatenate([bd_hi, bd_hi, bd(lo)], axis=0)


def _pair_lhs2(x):
    hi = x.astype(BF16)
    lo = (x - hi.astype(F32)).astype(BF16)
    return jnp.concatenate([hi, lo], axis=1)


def _pair_rhs1(y, low_lanes):
    hi = y.astype(BF16)
    zero = jnp.zeros_like(hi)
    bd = jnp.concatenate([jnp.where(low_lanes, hi, zero), jnp.where(low_lanes, zero, hi)], axis=0)
    return jnp.concatenate([bd, bd], axis=0)


def _pair_rows(x, low_lanes):
    return jnp.concatenate([jnp.where(low_lanes, x, 0.0), jnp.where(low_lanes, 0.0, x)], axis=0)


def _gdn_kernel(alog_ref, dt_ref, qf_ref, kf_ref, vf_ref, gf_ref, qb_ref, kb_ref, vb_ref, gb_ref,
                of_ref, ob_ref, st_ref, *, tb):
    @pl.when(pl.program_id(0) == 0)
    def _():
        st_ref[...] = jnp.zeros_like(st_ref)

    row = lax.broadcasted_iota(jnp.int32, (CHUNK, LANES), 0)
    lane = lax.broadcasted_iota(jnp.int32, (CHUNK, LANES), 1)
    col = lane & (CHUNK - 1)
    low_lanes = lane < CHUNK
    low_row = low_lanes[0:1]
    causal, anti = row >= col, row <= col
    strict_c, strict_a = row > col, row < col
    tri3_f, tri3_b = _tri3(False), _tri3(True)
    eye = (row == col).astype(F32)
    same16 = (row >> 4) == (col >> 4)
    same32 = (row >> 5) == (col >> 5)
    in32 = jnp.logical_and(same32, jnp.logical_not(same16))
    n = tb // CHUNK
    neg_a = -jnp.exp(alog_ref[...])
    dt = dt_ref[...]

    def pairs(q_ref, k_ref, v_ref, g_ref, o_ref, rows, rev):
        raw = g_ref[rows, :]
        la = neg_a * _softplus(raw + dt)
        beta = jax.nn.sigmoid(raw)
        g = _sdot(tri3_b if rev else tri3_f, _rhs_split3(la))
        g_rows = jnp.concatenate([g, g], axis=0).T
        la0 = GDN_HEADS if rev else 0
        b0 = (3 if rev else 2) * GDN_HEADS
        last = 0 if rev else CHUNK - 1
        out = []
        for j in range(GDN_HEADS // 2):
            hs = (2 * j, 2 * j + 1)
            g_cols = [jnp.broadcast_to(g[:, la0 + h:la0 + h + 1], (CHUNK, LANES)) for h in hs]
            b_cols = [jnp.broadcast_to(beta[:, b0 + h:b0 + h + 1], (CHUNK, LANES)) for h in hs]
            g_col = jnp.where(low_lanes, g_cols[0], g_cols[1])
            g_row = jnp.where(low_row, g_rows[la0 + hs[0]:la0 + hs[0] + 1], g_rows[la0 + hs[1]:la0 + hs[1] + 1])
            out.append(dict(
                q_ref=q_ref, k_ref=k_ref, v_ref=v_ref, o_ref=o_ref, rows=rows,
                cs=[slice(h * GDN_DK, (h + 1) * GDN_DK) for h in hs],
                slots=[(GDN_HEADS if rev else 0) + h for h in hs],
                strict=strict_a if rev else strict_c,
                decay=jnp.exp(jnp.where(anti if rev else causal, g_col - g_row, -jnp.inf)),
                g_cols=g_cols, b_cols=b_cols,
                g_last=[g[last:last + 1, la0 + h:la0 + h + 1] for h in hs]))
        return out

    def body(c, carry):
        rf = pl.ds(pl.multiple_of(c * CHUNK, CHUNK), CHUNK)
        rb = pl.ds(pl.multiple_of((n - 1 - c) * CHUNK, CHUNK), CHUNK)
        ps = (pairs(qf_ref, kf_ref, vf_ref, gf_ref, of_ref, rf, False)
              + pairs(qb_ref, kb_ref, vb_ref, gb_ref, ob_ref, rb, True))
        for p in ps:
            ks = [p["k_ref"][p["rows"], cs] for cs in p["cs"]]
            qs = [p["q_ref"][p["rows"], cs] for cs in p["cs"]]
            p["kb"] = [k * b for k, b in zip(ks, p["b_cols"])]
            lhs = jnp.concatenate([jnp.concatenate([kb, q], axis=0) for kb, q in zip(p["kb"], qs)], axis=1)
            zero = jnp.zeros_like(ks[0])
            k_bd = jnp.concatenate([jnp.concatenate([ks[0], zero], axis=1),
                                    jnp.concatenate([zero, ks[1]], axis=1)], axis=0)
            kq = _bdot_nt(lhs, k_bd)
            a = jnp.where(p["strict"], kq[:CHUNK] * p["decay"], 0.0)
            p["qk"] = kq[CHUNK:] * p["decay"]
            n1 = -jnp.where(same16, a, 0.0)
            p["n_l"], p["n_r"], p["t"] = _pair_lhs2(n1), _pair_rhs1(n1, low_lanes), eye + n1
            p["e32"] = _pair_rhs1(jnp.where(in32, a, 0.0), low_lanes)
            p["e64"] = _pair_rhs1(jnp.where(same32, 0.0, a), low_lanes)
        for step in range(3):
            for p in ps:
                nn = _sdot(p["n_l"], p["n_r"])
                p["n_r"] = _pair_rhs(nn, low_lanes)
                if step < 2:
                    p["n_l"] = _pair_lhs(nn)
            for p in ps:
                p["t"] = p["t"] + _sdot(_pair_lhs(p["t"]), p["n_r"])
        for e in ("e32", "e64"):
            for p in ps:
                p["x"] = _sdot(_pair_lhs2(p["t"]), p[e])
            for p in ps:
                p["t"] = p["t"] - _sdot(_pair_lhs(p["x"]), _pair_rhs(p["t"], low_lanes))
        for p in ps:
            rhs = []
            for cs, kb, b, g in zip(p["cs"], p["kb"], p["b_cols"], p["g_cols"]):
                rhs.append(jnp.concatenate([p["v_ref"][p["rows"], cs] * b, kb * jnp.exp(g)], axis=1))
            p["sol"] = _sdot(_pair_lhs(_pair_rows(p["t"], low_lanes)), _rhs_split(jnp.concatenate(rhs, axis=0)))
        for p in ps:
            p["v_new"], p["o"] = [], []
            for i, (cs, slot, g) in enumerate(zip(p["cs"], p["slots"], p["g_cols"])):
                sol = p["sol"][i * CHUNK:(i + 1) * CHUNK]
                q = p["q_ref"][p["rows"], cs]
                ws = _bdot(jnp.concatenate([sol[:, GDN_DV:], q * jnp.exp(g)], axis=0), st_ref[slot])
                p["v_new"].append(sol[:, :GDN_DV] - ws[:CHUNK])
                p["o"].append(ws[CHUNK:])
        for p in ps:
            o_intra = _bdot(_pair_rows(p["qk"], low_lanes), jnp.concatenate(p["v_new"], axis=0))
            for i, (cs, slot, g, g_last) in enumerate(zip(p["cs"], p["slots"], p["g_cols"], p["g_last"])):
                k = p["k_ref"][p["rows"], cs]
                p["o_ref"][p["rows"], cs] = p["o"][i] + o_intra[i * CHUNK:(i + 1) * CHUNK]
                st_ref[slot] = (st_ref[slot] * jnp.exp(g_last)
                                + _bdot_tn(k * jnp.exp(g_last - g), p["v_new"][i]))
        return carry

    lax.fori_loop(0, n, body, 0)


def _gdn(qkv, gates, alog_row, dt_row, *, tb=512):
    s = qkv.shape[0]
    tb = min(tb, s)
    nb = s // tb
    w = GDN_HEADS * GDN_DK
    fwd = lambda width, cb: pl.BlockSpec((tb, width), lambda i: (i, cb))
    bwd = lambda width, cb: pl.BlockSpec((tb, width), lambda i: (nb - 1 - i, cb))
    row = pl.BlockSpec((1, LANES), lambda i: (0, 0))
    return pl.pallas_call(
        functools.partial(_gdn_kernel, tb=tb), grid=(nb,),
        in_specs=[row, row,
                  fwd(w, 0), fwd(w, 1), fwd(w, 2), fwd(LANES, 0),
                  bwd(w, 0), bwd(w, 1), bwd(w, 2), bwd(LANES, 0)],
        out_specs=[fwd(w, 0), bwd(w, 0)],
        out_shape=[jax.ShapeDtypeStruct((s, w), F32)] * 2,
        scratch_shapes=[pltpu.VMEM((2 * GDN_HEADS, GDN_DK, GDN_DV), F32)],
        compiler_params=_params(("arbitrary",), 40), name="gdn_scan",
    )(alog_row, dt_row, qkv, qkv, qkv, gates, qkv, qkv, qkv, gates)


def _rot_cols(w_pe):
    half = w_pe.shape[-1] // 2
    return jnp.concatenate([-w_pe[..., half:], w_pe[..., :half]], axis=-1)


def _pad_lanes(v, fill=0.0):
    v = v.reshape(1, -1).astype(F32)
    return jnp.pad(v, ((0, 0), (0, LANES - v.shape[1])), constant_values=fill)


def kernel(x, positions, norm_mix_w, norm_ffn_w, final_norm_w, hg_lb_logits, even_w_in, hg_norm_w,
           mla_q_norm_w, mla_w_q_b, mla_kv_norm_w, mla_w_kv_b, even_w_out, odd_w_in, ret_norm_w,
           gdn_conv_w, gdn_a_log, gdn_dt_bias, gdn_norm_w, odd_w_out, ffn_w_up, ffn_w_down):
    b, s, d = x.shape
    assert b == 1 and s % CHUNK == 0
    xs = x.reshape(s, d)
    row = lambda v: v.reshape(1, -1).astype(F32)

    pos_col = positions.reshape(s, 1)
    inv_ret = ROPE_THETA ** (-jnp.arange(RET_DK // 2, dtype=F32) / (RET_DK // 2))
    inv_mla = ROPE_THETA ** (-jnp.arange(MLA_ROPE // 2, dtype=F32) / (MLA_ROPE // 2))
    inv_row = row(jnp.concatenate([inv_ret, inv_mla, jnp.zeros((LANES - RET_DK // 2 - MLA_ROPE // 2,), F32)]))
    cos_ret, sin_ret, cos_mla, sin_mla = _rope_tables(pos_col, inv_row)

    n_main = HG_HEADS * (3 * HG_DK + 2 * HG_DV) + MLA_Q_RANK
    wt_in = even_w_in[0].T.astype(BF16)
    wt_kpe = wt_in[n_main + MLA_KV_RANK:]
    wt_side = jnp.concatenate([wt_in[n_main:], _rot_cols(wt_kpe.T).T], axis=0)
    h_main, h_side = _norm_matmul(xs, row(norm_mix_w[0]), wt_in, n_main, None, wt_side, tn=512)

    o_f, o_b = _hgrn2(h_main, hg_lb_logits.astype(F32), layer=0)

    wq = mla_w_q_b[0].reshape(MLA_Q_RANK, MLA_HEADS, MLA_NOPE + MLA_ROPE)
    wq_pe = wq[..., MLA_NOPE:]
    wq = jnp.concatenate([wq[..., :MLA_NOPE], wq_pe, _rot_cols(wq_pe)], axis=-1)
    wq = wq.reshape(MLA_Q_RANK, MLA_HEADS * MLA_QK_PAD).astype(BF16)
    q, k, v = _mla_proj(h_main, h_side, row(mla_q_norm_w[0]), row(mla_kv_norm_w[0]), wq,
                        mla_w_kv_b[0].astype(BF16), cos_mla, sin_mla)
    o_attn, (w_up, w_down) = _attention(q, k, v, [ffn_w_up, ffn_w_down])

    wa = HG_HEADS * HG_DV
    xs = _mix_out(xs, even_w_out[0].astype(BF16),
                  [(o_f, 0, wa, False), (o_b, 0, wa, False), (h_main, 4, wa, False), (row(hg_norm_w[0]), 0, wa, True),
                   (o_attn, 0, MLA_HEADS * MLA_V, False)],
                  ((HG_HEADS, HG_DV), None))
    xs = _ffn(xs, row(norm_ffn_w[0]), w_up, w_down, row(final_norm_w), layer=0, final_norm=False)

    n_ret = 2 * RET_HEADS * RET_DK + 2 * RET_HEADS * RET_DV
    n_qkv = GDN_HEADS * (2 * GDN_DK + GDN_DV)
    n_gate = 4 * GDN_HEADS
    wt_in = odd_w_in[0].T.astype(BF16)
    wt_gate = wt_in[n_ret + n_qkv + n_gate:]
    wt_side = jnp.pad(wt_in[n_ret + n_qkv:n_ret + n_qkv + n_gate], ((0, LANES - n_gate), (0, 0)))
    h_main, h_gates = _norm_matmul(xs, row(norm_mix_w[1]), wt_in, n_ret + n_qkv, wt_gate, wt_side, tn=1024)

    r_f, r_b = _retention(h_main, cos_ret, sin_ret)
    qkv = _gdn_prep(h_main, gdn_conv_w[0].astype(F32))
    g_f, g_b = _gdn(qkv, h_gates, _pad_lanes(gdn_a_log[0]), _pad_lanes(gdn_dt_bias[0]))

    wr, wg = RET_HEADS * RET_DV, GDN_HEADS * GDN_DV
    xs = _mix_out(xs, odd_w_out[0].astype(BF16),
                  [(r_f, 0, wr, False), (r_b, 0, wr, False), (h_main, 2, wr, False), (row(ret_norm_w[0]), 0, wr, True),
                   (g_f, 0, wg, False), (g_b, 0, wg, False), (h_main, 6, wg, False), (row(gdn_norm_w[0]), 0, wg, True)],
                  ((RET_HEADS, RET_DV), (GDN_HEADS, GDN_DV)))
    xs = _ffn(xs, row(norm_ffn_w[1]), w_up, w_down, row(final_norm_w), layer=1, final_norm=True)
    return xs.reshape(b, s, d)
```

```python
import functools
import math

import jax
import jax.numpy as jnp
from jax import lax
from jax.experimental import pallas as pl
from jax.experimental.pallas import tpu as pltpu

F32 = jnp.float32
BF16 = jnp.bfloat16

EPS = 1e-6
CHUNK = 64
ROPE_THETA = 10000.0
LANES = 128
SUBLANES = 8

HG_HEADS, HG_DK, HG_DV = 8, 128, 128
MLA_HEADS, MLA_NOPE, MLA_ROPE, MLA_V = 8, 128, 64, 128
MLA_Q_RANK, MLA_KV_RANK = 512, 256
MLA_QK_PAD = 256
RET_HEADS, RET_DK, RET_DV = 4, 128, 256
RET_UNROLL = 2
GDN_HEADS, GDN_DK, GDN_DV = 8, 128, 128
CONV_WIDTH = 5

NT_DIMS = (((1,), (1,)), ((), ()))
TN_DIMS = (((0,), (0,)), ((), ()))


def _params(semantics, vmem_mib):
    return pltpu.CompilerParams(dimension_semantics=semantics, vmem_limit_bytes=vmem_mib * 1024 * 1024)


def _rms_rows(x, w):
    ms = jnp.mean(x * x, axis=-1, keepdims=True)
    return x * lax.rsqrt(ms + EPS) * w


def _silu(x):
    return x * jax.nn.sigmoid(x)


def _bdot(a, b):
    return jnp.dot(a.astype(BF16), b.astype(BF16), preferred_element_type=F32)


def _bdot_nt(a, b):
    return lax.dot_general(a.astype(BF16), b.astype(BF16), NT_DIMS, preferred_element_type=F32)


def _bdot_tn(a, b):
    return lax.dot_general(a.astype(BF16), b.astype(BF16), TN_DIMS, preferred_element_type=F32)


def _rhs_split(b):
    hi = b.astype(BF16)
    lo = (b - hi.astype(F32)).astype(BF16)
    return jnp.concatenate([hi, hi, lo], axis=0)


def _rhs_split3(b):
    b1 = b.astype(BF16)
    r = b - b1.astype(F32)
    b2 = r.astype(BF16)
    b3 = (r - b2.astype(F32)).astype(BF16)
    return jnp.concatenate([b1, b2, b3], axis=0)


def _sdot(lhs_split, rhs_split):
    return jnp.dot(lhs_split, rhs_split, preferred_element_type=F32)


def _tri3(rev):
    row = lax.broadcasted_iota(jnp.int32, (CHUNK, 3 * CHUNK), 0)
    col = lax.broadcasted_iota(jnp.int32, (CHUNK, 3 * CHUNK), 1) & (CHUNK - 1)
    return jnp.where(row <= col if rev else row >= col, 1.0, 0.0).astype(BF16)


def _rope_table_kernel(pos_ref, inv_ref, cos_ret_ref, sin_ret_ref, cos_mla_ref, sin_mla_ref):
    ang = pos_ref[...].astype(F32) * inv_ref[...]
    cos, sin = jnp.cos(ang), jnp.sin(ang)
    lane = lax.broadcasted_iota(jnp.int32, cos.shape, 1)
    half, quarter = LANES // 2, LANES // 4
    cos_ret_ref[...] = jnp.where(lane < half, cos, pltpu.roll(cos, half, 1))
    sin_ret_ref[...] = jnp.where(lane < half, -sin, pltpu.roll(sin, half, 1))

    def mla(t):
        return jnp.where(lane < quarter, pltpu.roll(t, half, 1),
                         jnp.where(lane < half, pltpu.roll(t, half + quarter, 1), 0.0))

    cos_mla_ref[...] = mla(cos)
    sin_mla_ref[...] = mla(sin)


def _rope_tables(pos_col, inv_row):
    s = pos_col.shape[0]
    tm = min(512, s)
    tab = pl.BlockSpec((tm, LANES), lambda i: (i, 0))
    return pl.pallas_call(
        _rope_table_kernel, grid=(s // tm,),
        in_specs=[pl.BlockSpec((tm, 1), lambda i: (i, 0)), pl.BlockSpec((1, LANES), lambda i: (0, 0))],
        out_specs=[tab] * 4,
        out_shape=[jax.ShapeDtypeStruct((s, LANES), F32)] * 4,
        compiler_params=_params(("parallel",), 16), name="rope_tables",
    )(pos_col, inv_row)


NORM_ROWS = 128


def _norm_rows_to(x_ref, nw_ref, xn_ref, tm):
    def body(r, carry):
        rows = pl.ds(pl.multiple_of(r * NORM_ROWS, NORM_ROWS), NORM_ROWS)
        xn_ref[rows, :] = _rms_rows(x_ref[rows, :], nw_ref[...]).astype(BF16)
        return carry
    lax.fori_loop(0, tm // NORM_ROWS, body, 0)


def _norm_mm_kernel(*refs, tm, n_a):
    x_ref, nw_ref, wa_ref = refs[:3]
    wb_ref = refs[3] if len(refs) == 8 else None
    wst_ref, o_ref, os_ref, xn_ref = refs[-4:]
    j = pl.program_id(1)

    @pl.when(j == 0)
    def _():
        _norm_rows_to(x_ref, nw_ref, xn_ref, tm)
        os_ref[...] = lax.dot_general(xn_ref[...], wst_ref[...], NT_DIMS, preferred_element_type=F32)

    def project(wt_ref):
        o_ref[...] = lax.dot_general(xn_ref[...], wt_ref[...], NT_DIMS, preferred_element_type=F32)

    if wb_ref is None:
        project(wa_ref)
    else:
        pl.when(j < n_a)(lambda: project(wa_ref))
        pl.when(j >= n_a)(lambda: project(wb_ref))


def _norm_matmul(x, nw, wt_a, n_a_rows, wt_b, wt_side, *, tm=1024, tn=512):
    s, d = x.shape
    ns = wt_side.shape[0]
    tm = min(tm, s)
    n_a = n_a_rows // tn
    n = n_a_rows + (0 if wt_b is None else wt_b.shape[0])
    w_specs = [pl.BlockSpec((tn, d), lambda i, j: (jnp.minimum(j, n_a - 1), 0))]
    weights = [wt_a]
    if wt_b is not None:
        w_specs.append(pl.BlockSpec((tn, d), lambda i, j: (jnp.maximum(j - n_a, 0), 0),
                                    pipeline_mode=pl.Buffered(1)))
        weights.append(wt_b)
    return pl.pallas_call(
        functools.partial(_norm_mm_kernel, tm=tm, n_a=n_a), grid=(s // tm, n // tn),
        in_specs=[pl.BlockSpec((tm, d), lambda i, j: (i, 0)),
                  pl.BlockSpec((1, d), lambda i, j: (0, 0)),
                  *w_specs,
                  pl.BlockSpec((ns, d), lambda i, j: (0, 0))],
        out_specs=[pl.BlockSpec((tm, tn), lambda i, j: (i, j)),
                   pl.BlockSpec((tm, ns), lambda i, j: (i, 0))],
        out_shape=[jax.ShapeDtypeStruct((s, n), F32), jax.ShapeDtypeStruct((s, ns), F32)],
        scratch_shapes=[pltpu.VMEM((tm, d), BF16)],
        compiler_params=_params(("parallel", "arbitrary"), 52), name="norm_matmul",
    )(x, nw, *weights, wt_side)


def _gla_chunks(ps, st_ref):
    for p in ps:
        p["qe"] = p["q"] * jnp.exp(p["b"] - p["b_mid"])
        p["ke"] = p["k"] * jnp.exp(p["b_mid"] - p["b"])
        p["scores"] = jnp.where(p["mask"], _bdot_nt(p["qe"], p["ke"]), 0.0)
    for p in ps:
        q_dec = p["qe"] * jnp.exp(p["b_mid"])
        p["o"] = _bdot(p["scores"], p["v"]) + _bdot_nt(q_dec, st_ref[p["slot"]])
    for p in ps:
        k_dec = p["ke"] * jnp.exp(p["b_last"] - p["b_mid"])
        st_ref[p["slot"]] = st_ref[p["slot"]] * jnp.exp(p["b_last"]) + _bdot_tn(p["v"], k_dec)
        p["o_ref"][p["rows"], p["cols"]] = p["o"]


def _chunk_masks():
    row = lax.broadcasted_iota(jnp.int32, (CHUNK, CHUNK), 0)
    col = lax.broadcasted_iota(jnp.int32, (CHUNK, CHUNK), 1)
    return row, col


def _hgrn_kernel(logit_ref, qf_ref, ff_ref, vf_ref, qb_ref, fb_ref, vb_ref, of_ref, ob_ref, st_ref,
                 *, tb, layer):
    @pl.when(pl.program_id(0) == 0)
    def _():
        st_ref[...] = jnp.zeros_like(st_ref)

    lg = logit_ref[...]
    e = jnp.exp(lg - jnp.max(lg, axis=0, keepdims=True))
    lb = jnp.sum(e[0:layer + 1], axis=0, keepdims=True) / jnp.sum(e, axis=0, keepdims=True)
    row, col = _chunk_masks()
    causal, anti = row >= col, row <= col
    tri3_f, tri3_b = _tri3(False), _tri3(True)
    n = tb // CHUNK
    mid = CHUNK // 2
    scale = HG_DK ** -0.5

    def problem(q_ref, f_ref, v_ref, o_ref, rows, h, rev):
        cs = slice(h * HG_DK, (h + 1) * HG_DK)
        lbh = lb[:, cs]
        sig = jax.nn.sigmoid(f_ref[rows, cs])
        f = lbh + (1.0 - lbh) * sig
        b = _sdot(tri3_b if rev else tri3_f, _rhs_split3(jnp.log(f)))
        if rev:
            b_last, b_mid = b[0:1], b[CHUNK - 1 - mid:CHUNK - mid]
        else:
            b_last, b_mid = b[CHUNK - 1:CHUNK], b[mid:mid + 1]
        return dict(q=_silu(q_ref[rows, cs]) * scale, k=(1.0 - lbh) * (1.0 - sig), v=v_ref[rows, cs],
                    b=b, b_mid=b_mid, b_last=b_last, mask=anti if rev else causal,
                    slot=(HG_HEADS if rev else 0) + h, o_ref=o_ref, rows=rows, cols=cs)

    def body(c, carry):
        rf = pl.ds(pl.multiple_of(c * CHUNK, CHUNK), CHUNK)
        rb = pl.ds(pl.multiple_of((n - 1 - c) * CHUNK, CHUNK), CHUNK)
        ps = [problem(qf_ref, ff_ref, vf_ref, of_ref, rf, h, False) for h in range(HG_HEADS)]
        ps += [problem(qb_ref, fb_ref, vb_ref, ob_ref, rb, h, True) for h in range(HG_HEADS)]
        _gla_chunks(ps, st_ref)
        return carry

    lax.fori_loop(0, n, body, 0)


def _hgrn2(h_main, lb_logits, *, layer, tb=512):
    s = h_main.shape[0]
    tb = min(tb, s)
    nb = s // tb
    w = HG_HEADS * HG_DK
    fwd = lambda cb: pl.BlockSpec((tb, w), lambda i: (i, cb))
    bwd = lambda cb: pl.BlockSpec((tb, w), lambda i: (nb - 1 - i, cb))
    return pl.pallas_call(
        functools.partial(_hgrn_kernel, tb=tb, layer=layer), grid=(nb,),
        in_specs=[pl.BlockSpec(lb_logits.shape, lambda i: (0, 0)),
                  fwd(0), fwd(1), fwd(3), bwd(0), bwd(2), bwd(3)],
        out_specs=[fwd(0), bwd(0)],
        out_shape=[jax.ShapeDtypeStruct((s, w), F32)] * 2,
        scratch_shapes=[pltpu.VMEM((2 * HG_HEADS, HG_DV, HG_DK), F32)],
        compiler_params=_params(("arbitrary",), 40), name="hgrn2_scan",
    )(lb_logits, h_main, h_main, h_main, h_main, h_main, h_main)


def _ret_kernel(qf_ref, kf_ref, vf_ref, cf_ref, sf_ref, qb_ref, kb_ref, vb_ref, cb_ref, sb_ref,
                of_ref, ob_ref, st_ref, *, tb):
    @pl.when(pl.program_id(0) == 0)
    def _():
        st_ref[...] = jnp.zeros_like(st_ref)

    row, col = _chunk_masks()
    n = tb // CHUNK
    scale = RET_DK ** -0.5
    t_idx = lax.broadcasted_iota(jnp.int32, (CHUNK, RET_DK), 0).astype(F32)
    log_gamma = [math.log1p(-2.0 ** (-5 - h)) for h in range(RET_HEADS)]

    def rope(x, cos, sin):
        return x * cos + pltpu.roll(x, RET_DK // 2, 1) * sin

    def decay_tables(h, rev):
        lgam = log_gamma[RET_HEADS - 1 - h] if rev else log_gamma[h]
        steps = (CHUNK - t_idx) if rev else (t_idx + 1.0)
        dist = ((col - row) if rev else (row - col)).astype(F32)
        return dict(intra=jnp.where(dist >= 0.0, jnp.exp(dist * lgam), 0.0),
                    q_fac=jnp.exp(steps * lgam), k_fac=jnp.exp((CHUNK - steps) * lgam),
                    st_fac=math.exp(CHUNK * lgam))

    tables = {(h, rev): decay_tables(h, rev) for h in range(RET_HEADS) for rev in (False, True)}

    def problem(q_ref, k_ref, v_ref, c_ref, s_ref, o_ref, rows, h, rev):
        cs = slice(h * RET_DK, (h + 1) * RET_DK)
        vs = slice(h * RET_DV, (h + 1) * RET_DV)
        cos, sin = c_ref[rows, :], s_ref[rows, :]
        return dict(q=rope(q_ref[rows, cs], cos, sin) * scale, k=rope(k_ref[rows, cs], cos, sin),
                    v=v_ref[rows, vs], slot=(RET_HEADS if rev else 0) + h, o_ref=o_ref, rows=rows, cols=vs,
                    **tables[(h, rev)])

    def body(c, carry):
        groups = []
        for u in range(RET_UNROLL):
            rf = pl.ds(pl.multiple_of((RET_UNROLL * c + u) * CHUNK, CHUNK), CHUNK)
            rb = pl.ds(pl.multiple_of((n - 1 - RET_UNROLL * c - u) * CHUNK, CHUNK), CHUNK)
            groups.append(
                [problem(qf_ref, kf_ref, vf_ref, cf_ref, sf_ref, of_ref, rf, h, False) for h in range(RET_HEADS)]
                + [problem(qb_ref, kb_ref, vb_ref, cb_ref, sb_ref, ob_ref, rb, h, True) for h in range(RET_HEADS)])
        for ps in groups:
            for p in ps:
                p["scores"] = _bdot_nt(p["q"], p["k"]) * p["intra"]
                p["update"] = _bdot_tn(p["v"], p["k"] * p["k_fac"])
        for ps in groups:
            for p in ps:
                p["o"] = _bdot(p["scores"], p["v"]) + _bdot_nt(p["q"] * p["q_fac"], st_ref[p["slot"]])
            for p in ps:
                st_ref[p["slot"]] = st_ref[p["slot"]] * p["st_fac"] + p["update"]
                p["o_ref"][p["rows"], p["cols"]] = p["o"]
        return carry

    assert n % RET_UNROLL == 0
    lax.fori_loop(0, n // RET_UNROLL, body, 0)


def _retention(h_main, cos_tab, sin_tab, *, tb=512):
    s = h_main.shape[0]
    tb = min(tb, s)
    nb = s // tb
    wk, wv = RET_HEADS * RET_DK, RET_HEADS * RET_DV
    fwd = lambda w, cb: pl.BlockSpec((tb, w), lambda i: (i, cb))
    bwd = lambda w, cb: pl.BlockSpec((tb, w), lambda i: (nb - 1 - i, cb))
    return pl.pallas_call(
        functools.partial(_ret_kernel, tb=tb), grid=(nb,),
        in_specs=[fwd(wk, 0), fwd(wk, 1), fwd(wv, 1), fwd(LANES, 0), fwd(LANES, 0),
                  bwd(wk, 0), bwd(wk, 1), bwd(wv, 1), bwd(LANES, 0), bwd(LANES, 0)],
        out_specs=[fwd(wv, 0), bwd(wv, 0)],
        out_shape=[jax.ShapeDtypeStruct((s, wv), F32)] * 2,
        scratch_shapes=[pltpu.VMEM((2 * RET_HEADS, RET_DV, RET_DK), F32)],
        compiler_params=_params(("arbitrary",), 40), name="retention_scan",
    )(h_main, h_main, h_main, cos_tab, sin_tab, h_main, h_main, h_main, cos_tab, sin_tab)


def _gated_head_norm(of_ref, ob_ref, gate_ref, nw_ref, rows, heads, hd):
    o = of_ref[rows, :] + ob_ref[rows, :]
    parts = []
    for h in range(heads):
        y = o[:, h * hd:(h + 1) * hd]
        parts.append(y * lax.rsqrt(jnp.mean(y * y, axis=-1, keepdims=True) + EPS))
    y = jnp.concatenate(parts, axis=-1)
    return y * nw_ref[...] * _silu(gate_ref[rows, :])


def _mixout_kernel(*refs, tm, groups):
    x_ref, w_ref = refs[0], refs[1]
    out_ref, lhs_ref = refs[-2], refs[-1]
    grefs = refs[2:-2]

    def body(r, carry):
        rows = pl.ds(pl.multiple_of(r * NORM_ROWS, NORM_ROWS), NORM_ROWS)
        pos, col = 0, 0
        for g in groups:
            if g is None:
                val, width = grefs[pos][rows, :], grefs[pos].shape[1]
                pos += 1
            else:
                heads, hd = g
                val, width = _gated_head_norm(*grefs[pos:pos + 4], rows, heads, hd), heads * hd
                pos += 4
            lhs_ref[rows, col:col + width] = val.astype(BF16)
            col += width
        return carry
    lax.fori_loop(0, tm // NORM_ROWS, body, 0)

    out_ref[...] = x_ref[...] + jnp.dot(lhs_ref[...], w_ref[...], preferred_element_type=F32)


def _mix_out(x, w_out, group_args, groups, *, tm=256):
    s, d = x.shape
    tm = min(tm, s)
    specs = [pl.BlockSpec((tm, d), lambda i: (i, 0)),
             pl.BlockSpec(w_out.shape, lambda i: (0, 0), pipeline_mode=pl.Buffered(1))]
    arrays = [x, w_out]
    for arr, cb, width, rowvec in group_args:
        arrays.append(arr)
        if rowvec:
            specs.append(pl.BlockSpec((1, width), lambda i: (0, 0)))
        else:
            specs.append(pl.BlockSpec((tm, width), lambda i, cb=cb: (i, cb)))
    return pl.pallas_call(
        functools.partial(_mixout_kernel, tm=tm, groups=groups), grid=(s // tm,),
        in_specs=specs,
        out_specs=pl.BlockSpec((tm, d), lambda i: (i, 0)),
        out_shape=jax.ShapeDtypeStruct((s, d), F32),
        scratch_shapes=[pltpu.VMEM((tm, w_out.shape[0]), BF16)],
        compiler_params=_params(("parallel",), 40), name="mix_out",
    )(*arrays)


def _rot_pair(pr, cos, sin):
    return pr * cos + pltpu.roll(pr, MLA_ROPE, 1) * sin


def _mla_proj_kernel(cq_ref, ckv_ref, kpe_ref, qnw_ref, kvnw_ref, wq_ref, wkv_ref, cos_ref, sin_ref,
                     q_ref, k_ref, v_ref, *, scale):
    cos, sin = cos_ref[...], sin_ref[...]
    cqn = _rms_rows(cq_ref[...], qnw_ref[...]).astype(BF16)
    ckvn = _rms_rows(ckv_ref[...], kvnw_ref[...]).astype(BF16)
    pe = _rot_pair(kpe_ref[...], cos, sin).astype(BF16)
    lane = lax.broadcasted_iota(jnp.int32, pe.shape, 1)
    ones_col = jnp.where(lane == 0, 1.0, 0.0).astype(BF16)
    for h in range(MLA_HEADS):
        cols = slice(h * MLA_QK_PAD, (h + 1) * MLA_QK_PAD)
        yq = jnp.dot(cqn, wq_ref[:, cols], preferred_element_type=F32)
        q_rope = _rot_pair(yq[:, MLA_NOPE:], cos, sin)
        q_ref[h] = (jnp.concatenate([yq[:, :MLA_NOPE], q_rope], axis=1) * scale).astype(BF16)
        ykv = jnp.dot(ckvn, wkv_ref[:, cols], preferred_element_type=F32)
        k_ref[h] = jnp.concatenate([ykv[:, :MLA_NOPE].astype(BF16), pe], axis=1)
        v_ref[h] = jnp.concatenate([ykv[:, MLA_NOPE:].astype(BF16), ones_col], axis=1)


def _mla_proj(h_main, h_side, q_norm_w, kv_norm_w, wq, wkv, cos_tab, sin_tab, *, tm=512):
    s = h_main.shape[0]
    tm = min(tm, s)
    cq_block = (HG_HEADS * (3 * HG_DK + 2 * HG_DV)) // MLA_Q_RANK
    scale = (MLA_NOPE + MLA_ROPE) ** -0.5 * math.log2(math.e)
    full = lambda a: pl.BlockSpec(a.shape, lambda i: (0, 0))
    tab = pl.BlockSpec((tm, LANES), lambda i: (i, 0))
    head_out = pl.BlockSpec((MLA_HEADS, tm, MLA_QK_PAD), lambda i: (0, i, 0))
    return pl.pallas_call(
        functools.partial(_mla_proj_kernel, scale=scale), grid=(s // tm,),
        in_specs=[pl.BlockSpec((tm, MLA_Q_RANK), lambda i: (i, cq_block)),
                  pl.BlockSpec((tm, MLA_KV_RANK), lambda i: (i, 0)),
                  pl.BlockSpec((tm, LANES), lambda i: (i, MLA_KV_RANK // LANES)),
                  full(q_norm_w), full(kv_norm_w), full(wq), full(wkv), tab, tab],
        out_specs=[head_out, head_out, head_out],
        out_shape=[jax.ShapeDtypeStruct((MLA_HEADS, s, MLA_QK_PAD), BF16)] * 3,
        compiler_params=_params(("parallel",), 40), name="mla_proj",
    )(h_main, h_side, h_side, q_norm_w, kv_norm_w, wq, wkv, cos_tab, sin_tab)


ATTN_SPLIT = 2
ATTN_ROWS = 32


def _attn_kernel(q_ref, k_ref, v_ref, *refs, tk, nk, n_cast):
    cast_in, o_ref, cast_out = refs[:n_cast], refs[n_cast], refs[n_cast + 1:2 * n_cast + 1]
    s_ref, p_ref, m_ref, alpha_ref, acc_ref = refs[2 * n_cast + 1:]
    tq = q_ref.shape[1]
    sub = tq // ATTN_SPLIT
    m_ref[...] = jnp.full(m_ref.shape, -jnp.inf, F32)
    acc_ref[...] = jnp.zeros(acc_ref.shape, F32)

    def softmax_rows(i):
        for rb in range(sub // ATTN_ROWS):
            loc = slice(rb * ATTN_ROWS, (rb + 1) * ATTN_ROWS)
            glob = slice(i * sub + rb * ATTN_ROWS, i * sub + (rb + 1) * ATTN_ROWS)
            sc = s_ref[i, loc, :]
            cols = [sc[:, c * LANES:(c + 1) * LANES] for c in range(tk // LANES)]
            col_max = functools.reduce(jnp.maximum, cols)
            m_old = m_ref[glob, :]
            m_new = jnp.maximum(m_old, jnp.max(col_max, axis=-1, keepdims=True))
            m_ref[glob, :] = m_new
            alpha_ref[glob, :] = jnp.exp2(m_old - m_new)
            p_ref[i, loc, :] = jnp.concatenate([jnp.exp2(c - m_new) for c in cols], axis=1).astype(BF16)

    def body(j, carry):
        rows = pl.ds(pl.multiple_of(j * tk, tk), tk)
        k, v = k_ref[0, rows, :], v_ref[0, rows, :]
        for i in range(ATTN_SPLIT):
            s_ref[i] = lax.dot_general(q_ref[0, i * sub:(i + 1) * sub, :], k, NT_DIMS, preferred_element_type=F32)
        for i in range(ATTN_SPLIT):
            softmax_rows(i)
            rs = slice(i * sub, (i + 1) * sub)
            alpha = alpha_ref[rs, :]
            acc_ref[rs, :] = (jnp.concatenate([alpha, alpha], axis=1) * acc_ref[rs, :]
                              + jnp.dot(p_ref[i], v, preferred_element_type=F32))
        for src, dst in zip(cast_in, cast_out):
            piece = src.shape[1] // nk
            part = pl.ds(pl.multiple_of(j * piece, piece), piece)
            dst[0, part, :] = src[0, part, :].astype(BF16)
        return carry

    lax.fori_loop(0, nk, body, 0)
    acc = acc_ref[...]
    o_ref[...] = (acc[:, :MLA_V] / acc[:, MLA_V:MLA_V + 1]).astype(BF16)


def _attention(q, k, v, to_cast, *, tq=1024, tk=2048):
    heads, s, _ = q.shape
    tq, tk = min(tq, s), min(tk, s)
    sub = tq // ATTN_SPLIT
    nq, nk = s // tq, s // tk
    steps = heads * nq
    cast_specs = []
    for a in to_cast:
        layers, rows, cols = a.shape
        block = layers * rows // steps
        per_layer = rows // block
        assert block * steps == layers * rows and per_layer * block == rows and block % (2 * SUBLANES * nk) == 0
        cast_specs.append(pl.BlockSpec(
            (1, block, cols), lambda h, i, per_layer=per_layer: ((h * nq + i) // per_layer, (h * nq + i) % per_layer, 0)))
    outs = pl.pallas_call(
        functools.partial(_attn_kernel, tk=tk, nk=nk, n_cast=len(to_cast)), grid=(heads, nq),
        in_specs=[pl.BlockSpec((1, tq, MLA_QK_PAD), lambda h, i: (h, i, 0)),
                  pl.BlockSpec((1, s, MLA_QK_PAD), lambda h, i: (h, 0, 0)),
                  pl.BlockSpec((1, s, 2 * MLA_V), lambda h, i: (h, 0, 0))] + cast_specs,
        out_specs=[pl.BlockSpec((tq, MLA_V), lambda h, i: (i, h))] + cast_specs,
        out_shape=[jax.ShapeDtypeStruct((s, heads * MLA_V), BF16)]
                  + [jax.ShapeDtypeStruct(a.shape, BF16) for a in to_cast],
        scratch_shapes=[pltpu.VMEM((ATTN_SPLIT, sub, tk), F32), pltpu.VMEM((ATTN_SPLIT, sub, tk), BF16),
                        pltpu.VMEM((tq, LANES), F32), pltpu.VMEM((tq, LANES), F32),
                        pltpu.VMEM((tq, 2 * MLA_V), F32)],
        compiler_params=_params(("parallel", "arbitrary"), 56), name="mla_attention",
    )(q, k, v, *to_cast)
    return outs[0], outs[1:]


def _ffn_kernel(x_ref, nw_ref, wu_ref, wd_ref, fw_ref, o_ref, xn_ref, *, tm, nf, final_norm):
    f = pl.program_id(1)

    @pl.when(f == 0)
    def _():
        _norm_rows_to(x_ref, nw_ref, xn_ref, tm)
        o_ref[...] = x_ref[...]

    a = jnp.maximum(jnp.dot(xn_ref[...], wu_ref[0], preferred_element_type=F32), 0.0)
    o_ref[...] += jnp.dot((a * a).astype(BF16), wd_ref[0], preferred_element_type=F32)

    if final_norm:
        @pl.when(f == nf - 1)
        def _():
            def body(r, carry):
                rows = pl.ds(pl.multiple_of(r * NORM_ROWS, NORM_ROWS), NORM_ROWS)
                o_ref[rows, :] = _rms_rows(o_ref[rows, :], fw_ref[...])
                return carry
            lax.fori_loop(0, tm // NORM_ROWS, body, 0)


def _ffn(x, nw, w_up, w_down, final_w, *, layer, final_norm, tm=512, tf=1024):
    s, d = x.shape
    dff = w_up.shape[2]
    tm = min(tm, s)
    nf = dff // tf
    return pl.pallas_call(
        functools.partial(_ffn_kernel, tm=tm, nf=nf, final_norm=final_norm), grid=(s // tm, nf),
        in_specs=[pl.BlockSpec((tm, d), lambda i, f: (i, 0)),
                  pl.BlockSpec((1, d), lambda i, f: (0, 0)),
                  pl.BlockSpec((1, d, tf), lambda i, f: (layer, 0, f)),
                  pl.BlockSpec((1, tf, d), lambda i, f: (layer, f, 0)),
                  pl.BlockSpec((1, d), lambda i, f: (0, 0))],
        out_specs=pl.BlockSpec((tm, d), lambda i, f: (i, 0)),
        out_shape=jax.ShapeDtypeStruct((s, d), F32),
        scratch_shapes=[pltpu.VMEM((tm, d), BF16)],
        compiler_params=_params(("parallel", "arbitrary"), 48), name="ffn",
    )(x, nw, w_up, w_down, final_w)


def _gdn_prep_kernel(x_ref, xp_ref, xn_ref, cw_ref, o_ref, *, tm, nblk):
    i, sec = pl.program_id(0), pl.program_id(1)
    x = x_ref[...]
    prev = jnp.where(i > 0, xp_ref[...], 0.0)
    nxt = jnp.where(i < nblk - 1, xn_ref[...], 0.0)
    rows = lax.broadcasted_iota(jnp.int32, prev.shape, 0)
    half = CONV_WIDTH // 2

    def shifted(d):
        if d == 0:
            return x
        r = pltpu.roll(x, (-d) % tm, 0)
        if d < 0:
            edge = r[:SUBLANES]
            for t in range(-d):
                edge = jnp.where(rows == t, prev[SUBLANES + t + d:SUBLANES + t + d + 1], edge)
            return jnp.concatenate([edge, r[SUBLANES:]], axis=0)
        edge = r[tm - SUBLANES:]
        for t in range(SUBLANES - d, SUBLANES):
            edge = jnp.where(rows == t, nxt[t + d - SUBLANES:t + d - SUBLANES + 1], edge)
        return jnp.concatenate([r[:tm - SUBLANES], edge], axis=0)

    y = shifted(-half) * cw_ref[0:1]
    for j in range(1, CONV_WIDTH):
        y = y + shifted(j - half) * cw_ref[j:j + 1]
    y = _silu(y)
    parts = []
    for h in range(GDN_HEADS):
        a = y[:, h * GDN_DK:(h + 1) * GDN_DK]
        parts.append(a * lax.rsqrt(jnp.sum(a * a, axis=-1, keepdims=True) + EPS))
    nrm = jnp.concatenate(parts, axis=-1) * jnp.where(sec == 0, GDN_DK ** -0.5, 1.0)
    o_ref[...] = jnp.where(sec == 2, y, nrm)


def _gdn_prep(h_main, conv_w, *, tm=256):
    s = h_main.shape[0]
    tm = min(tm, s)
    nblk = s // tm
    w = GDN_HEADS * GDN_DK
    base = (2 * RET_HEADS * RET_DK + 2 * RET_HEADS * RET_DV) // w
    per = tm // SUBLANES
    return pl.pallas_call(
        functools.partial(_gdn_prep_kernel, tm=tm, nblk=nblk), grid=(nblk, 3),
        in_specs=[pl.BlockSpec((tm, w), lambda i, c: (i, base + c)),
                  pl.BlockSpec((SUBLANES, w), lambda i, c: (jnp.maximum(i * per - 1, 0), base + c)),
                  pl.BlockSpec((SUBLANES, w), lambda i, c: (jnp.minimum((i + 1) * per, s // SUBLANES - 1), base + c)),
                  pl.BlockSpec((CONV_WIDTH, w), lambda i, c: (0, c))],
        out_specs=pl.BlockSpec((tm, w), lambda i, c: (i, c)),
        out_shape=jax.ShapeDtypeStruct((s, 3 * w), F32),
        compiler_params=_params(("parallel", "arbitrary"), 32), name="gdn_prep",
    )(h_main, h_main, h_main, conv_w)


def _softplus(x):
    return jnp.maximum(x, 0.0) + jnp.log1p(jnp.exp(-jnp.abs(x)))


def _pair_lhs(x):
    hi = x.astype(BF16)
    lo = (x - hi.astype(F32)).astype(BF16)
    return jnp.concatenate([hi, lo, hi], axis=1)


def _pair_rhs(y, low_lanes):
    hi = y.astype(BF16)
    lo = (y - hi.astype(F32)).astype(BF16)
    zero = jnp.zeros_like(hi)

    def bd(a):
        return jnp.concatenate([jnp.where(low_lanes, a, zero), jnp.where(low_lanes, zero, a)], axis=0)

    bd_hi = bd(hi)
    return jnp.concatenate([bd_hi, bd_hi, bd(lo)], axis=0)


def _pair_lhs2(x):
    hi = x.astype(BF16)
    lo = (x - hi.astype(F32)).astype(BF16)
    return jnp.concatenate([hi, lo], axis=1)


def _pair_rhs1(y, low_lanes):
    hi = y.astype(BF16)
    zero = jnp.zeros_like(hi)
    bd = jnp.concatenate([jnp.where(low_lanes, hi, zero), jnp.where(low_lanes, zero, hi)], axis=0)
    return jnp.concatenate([bd, bd], axis=0)


def _pair_rows(x, low_lanes):
    return jnp.concatenate([jnp.where(low_lanes, x, 0.0), jnp.where(low_lanes, 0.0, x)], axis=0)


def _gdn_kernel(alog_ref, dt_ref, qf_ref, kf_ref, vf_ref, gf_ref, qb_ref, kb_ref, vb_ref, gb_ref,
                of_ref, ob_ref, st_ref, *, tb):
    @pl.when(pl.program_id(0) == 0)
    def _():
        st_ref[...] = jnp.zeros_like(st_ref)

    row = lax.broadcasted_iota(jnp.int32, (CHUNK, LANES), 0)
    lane = lax.broadcasted_iota(jnp.int32, (CHUNK, LANES), 1)
    col = lane & (CHUNK - 1)
    low_lanes = lane < CHUNK
    low_row = low_lanes[0:1]
    causal, anti = row >= col, row <= col
    strict_c, strict_a = row > col, row < col
    tri3_f, tri3_b = _tri3(False), _tri3(True)
    eye = (row == col).astype(F32)
    same16 = (row >> 4) == (col >> 4)
    same32 = (row >> 5) == (col >> 5)
    in32 = jnp.logical_and(same32, jnp.logical_not(same16))
    n = tb // CHUNK
    neg_a = -jnp.exp(alog_ref[...])
    dt = dt_ref[...]

    def pairs(q_ref, k_ref, v_ref, g_ref, o_ref, rows, rev):
        raw = g_ref[rows, :]
        la = neg_a * _softplus(raw + dt)
        beta = jax.nn.sigmoid(raw)
        g = _sdot(tri3_b if rev else tri3_f, _rhs_split3(la))
        g_rows = jnp.concatenate([g, g], axis=0).T
        la0 = GDN_HEADS if rev else 0
        b0 = (3 if rev else 2) * GDN_HEADS
        last = 0 if rev else CHUNK - 1
        out = []
        for j in range(GDN_HEADS // 2):
            hs = (2 * j, 2 * j + 1)
            g_cols = [jnp.broadcast_to(g[:, la0 + h:la0 + h + 1], (CHUNK, LANES)) for h in hs]
            b_cols = [jnp.broadcast_to(beta[:, b0 + h:b0 + h + 1], (CHUNK, LANES)) for h in hs]
            g_col = jnp.where(low_lanes, g_cols[0], g_cols[1])
            g_row = jnp.where(low_row, g_rows[la0 + hs[0]:la0 + hs[0] + 1], g_rows[la0 + hs[1]:la0 + hs[1] + 1])
            out.append(dict(
                q_ref=q_ref, k_ref=k_ref, v_ref=v_ref, o_ref=o_ref, rows=rows,
                cs=[slice(h * GDN_DK, (h + 1) * GDN_DK) for h in hs],
                slots=[(GDN_HEADS if rev else 0) + h for h in hs],
                strict=strict_a if rev else strict_c,
                decay=jnp.exp(jnp.where(anti if rev else causal, g_col - g_row, -jnp.inf)),
                g_cols=g_cols, b_cols=b_cols,
                g_last=[g[last:last + 1, la0 + h:la0 + h + 1] for h in hs]))
        return out

    def body(c, carry):
        rf = pl.ds(pl.multiple_of(c * CHUNK, CHUNK), CHUNK)
        rb = pl.ds(pl.multiple_of((n - 1 - c) * CHUNK, CHUNK), CHUNK)
        ps = (pairs(qf_ref, kf_ref, vf_ref, gf_ref, of_ref, rf, False)
              + pairs(qb_ref, kb_ref, vb_ref, gb_ref, ob_ref, rb, True))
        for p in ps:
            ks = [p["k_ref"][p["rows"], cs] for cs in p["cs"]]
            qs = [p["q_ref"][p["rows"], cs] for cs in p["cs"]]
            p["kb"] = [k * b for k, b in zip(ks, p["b_cols"])]
            lhs = jnp.concatenate([jnp.concatenate([kb, q], axis=0) for kb, q in zip(p["kb"], qs)], axis=1)
            zero = jnp.zeros_like(ks[0])
            k_bd = jnp.concatenate([jnp.concatenate([ks[0], zero], axis=1),
                                    jnp.concatenate([zero, ks[1]], axis=1)], axis=0)
            kq = _bdot_nt(lhs, k_bd)
            a = jnp.where(p["strict"], kq[:CHUNK] * p["decay"], 0.0)
            p["qk"] = kq[CHUNK:] * p["decay"]
            n1 = -jnp.where(same16, a, 0.0)
            p["n_l"], p["n_r"], p["t"] = _pair_lhs2(n1), _pair_rhs1(n1, low_lanes), eye + n1
            p["e32"] = _pair_rhs1(jnp.where(in32, a, 0.0), low_lanes)
            p["e64"] = _pair_rhs1(jnp.where(same32, 0.0, a), low_lanes)
        for step in range(3):
            for p in ps:
                nn = _sdot(p["n_l"], p["n_r"])
                p["n_r"] = _pair_rhs(nn, low_lanes)
                if step < 2:
                    p["n_l"] = _pair_lhs(nn)
            for p in ps:
                p["t"] = p["t"] + _sdot(_pair_lhs(p["t"]), p["n_r"])
        for e in ("e32", "e64"):
            for p in ps:
                p["x"] = _sdot(_pair_lhs2(p["t"]), p[e])
            for p in ps:
                p["t"] = p["t"] - _sdot(_pair_lhs(p["x"]), _pair_rhs(p["t"], low_lanes))
        for p in ps:
            rhs = []
            for cs, kb, b, g in zip(p["cs"], p["kb"], p["b_cols"], p["g_cols"]):
                rhs.append(jnp.concatenate([p["v_ref"][p["rows"], cs] * b, kb * jnp.exp(g)], axis=1))
            p["sol"] = _sdot(_pair_lhs(_pair_rows(p["t"], low_lanes)), _rhs_split(jnp.concatenate(rhs, axis=0)))
        for p in ps:
            p["v_new"], p["o"] = [], []
            for i, (cs, slot, g) in enumerate(zip(p["cs"], p["slots"], p["g_cols"])):
                sol = p["sol"][i * CHUNK:(i + 1) * CHUNK]
                q = p["q_ref"][p["rows"], cs]
                ws = _bdot(jnp.concatenate([sol[:, GDN_DV:], q * jnp.exp(g)], axis=0), st_ref[slot])
                p["v_new"].append(sol[:, :GDN_DV] - ws[:CHUNK])
                p["o"].append(ws[CHUNK:])
        for p in ps:
            o_intra = _bdot(_pair_rows(p["qk"], low_lanes), jnp.concatenate(p["v_new"], axis=0))
            for i, (cs, slot, g, g_last) in enumerate(zip(p["cs"], p["slots"], p["g_cols"], p["g_last"])):
                k = p["k_ref"][p["rows"], cs]
                p["o_ref"][p["rows"], cs] = p["o"][i] + o_intra[i * CHUNK:(i + 1) * CHUNK]
                st_ref[slot] = (st_ref[slot] * jnp.exp(g_last)
                                + _bdot_tn(k * jnp.exp(g_last - g), p["v_new"][i]))
        return carry

    lax.fori_loop(0, n, body, 0)


def _gdn(qkv, gates, alog_row, dt_row, *, tb=512):
    s = qkv.shape[0]
    tb = min(tb, s)
    nb = s // tb
    w = GDN_HEADS * GDN_DK
    fwd = lambda width, cb: pl.BlockSpec((tb, width), lambda i: (i, cb))
    bwd = lambda width, cb: pl.BlockSpec((tb, width), lambda i: (nb - 1 - i, cb))
    row = pl.BlockSpec((1, LANES), lambda i: (0, 0))
    return pl.pallas_call(
        functools.partial(_gdn_kernel, tb=tb), grid=(nb,),
        in_specs=[row, row,
                  fwd(w, 0), fwd(w, 1), fwd(w, 2), fwd(LANES, 0),
                  bwd(w, 0), bwd(w, 1), bwd(w, 2), bwd(LANES, 0)],
        out_specs=[fwd(w, 0), bwd(w, 0)],
        out_shape=[jax.ShapeDtypeStruct((s, w), F32)] * 2,
        scratch_shapes=[pltpu.VMEM((2 * GDN_HEADS, GDN_DK, GDN_DV), F32)],
        compiler_params=_params(("arbitrary",), 40), name="gdn_scan",
    )(alog_row, dt_row, qkv, qkv, qkv, gates, qkv, qkv, qkv, gates)


def _rot_cols(w_pe):
    half = w_pe.shape[-1] // 2
    return jnp.concatenate([-w_pe[..., half:], w_pe[..., :half]], axis=-1)


def _pad_lanes(v, fill=0.0):
    v = v.reshape(1, -1).astype(F32)
    return jnp.pad(v, ((0, 0), (0, LANES - v.shape[1])), constant_values=fill)


def kernel(x, positions, norm_mix_w, norm_ffn_w, final_norm_w, hg_lb_logits, even_w_in, hg_norm_w,
           mla_q_norm_w, mla_w_q_b, mla_kv_norm_w, mla_w_kv_b, even_w_out, odd_w_in, ret_norm_w,
           gdn_conv_w, gdn_a_log, gdn_dt_bias, gdn_norm_w, odd_w_out, ffn_w_up, ffn_w_down):
    b, s, d = x.shape
    assert b == 1 and s % CHUNK == 0
    xs = x.reshape(s, d)
    row = lambda v: v.reshape(1, -1).astype(F32)

    pos_col = positions.reshape(s, 1)
    inv_ret = ROPE_THETA ** (-jnp.arange(RET_DK // 2, dtype=F32) / (RET_DK // 2))
    inv_mla = ROPE_THETA ** (-jnp.arange(MLA_ROPE // 2, dtype=F32) / (MLA_ROPE // 2))
    inv_row = row(jnp.concatenate([inv_ret, inv_mla, jnp.zeros((LANES - RET_DK // 2 - MLA_ROPE // 2,), F32)]))
    cos_ret, sin_ret, cos_mla, sin_mla = _rope_tables(pos_col, inv_row)

    n_main = HG_HEADS * (3 * HG_DK + 2 * HG_DV) + MLA_Q_RANK
    wt_in = even_w_in[0].T.astype(BF16)
    wt_kpe = wt_in[n_main + MLA_KV_RANK:]
    wt_side = jnp.concatenate([wt_in[n_main:], _rot_cols(wt_kpe.T).T], axis=0)
    h_main, h_side = _norm_matmul(xs, row(norm_mix_w[0]), wt_in, n_main, None, wt_side, tn=512)

    o_f, o_b = _hgrn2(h_main, hg_lb_logits.astype(F32), layer=0)

    wq = mla_w_q_b[0].reshape(MLA_Q_RANK, MLA_HEADS, MLA_NOPE + MLA_ROPE)
    wq_pe = wq[..., MLA_NOPE:]
    wq = jnp.concatenate([wq[..., :MLA_NOPE], wq_pe, _rot_cols(wq_pe)], axis=-1)
    wq = wq.reshape(MLA_Q_RANK, MLA_HEADS * MLA_QK_PAD).astype(BF16)
    q, k, v = _mla_proj(h_main, h_side, row(mla_q_norm_w[0]), row(mla_kv_norm_w[0]), wq,
                        mla_w_kv_b[0].astype(BF16), cos_mla, sin_mla)
    o_attn, (w_up, w_down) = _attention(q, k, v, [ffn_w_up, ffn_w_down])

    wa = HG_HEADS * HG_DV
    xs = _mix_out(xs, even_w_out[0].astype(BF16),
                  [(o_f, 0, wa, False), (o_b, 0, wa, False), (h_main, 4, wa, False), (row(hg_norm_w[0]), 0, wa, True),
                   (o_attn, 0, MLA_HEADS * MLA_V, False)],
                  ((HG_HEADS, HG_DV), None))
    xs = _ffn(xs, row(norm_ffn_w[0]), w_up, w_down, row(final_norm_w), layer=0, final_norm=False)

    n_ret = 2 * RET_HEADS * RET_DK + 2 * RET_HEADS * RET_DV
    n_qkv = GDN_HEADS * (2 * GDN_DK + GDN_DV)
    n_gate = 4 * GDN_HEADS
    wt_in = odd_w_in[0].T.astype(BF16)
    wt_gate = wt_in[n_ret + n_qkv + n_gate:]
    wt_side = jnp.pad(wt_in[n_ret + n_qkv:n_ret + n_qkv + n_gate], ((0, LANES - n_gate), (0, 0)))
    h_main, h_gates = _norm_matmul(xs, row(norm_mix_w[1]), wt_in, n_ret + n_qkv, wt_gate, wt_side, tn=1024)

    r_f, r_b = _retention(h_main, cos_ret, sin_ret)
    qkv = _gdn_prep(h_main, gdn_conv_w[0].astype(F32))
    g_f, g_b = _gdn(qkv, h_gates, _pad_lanes(gdn_a_log[0]), _pad_lanes(gdn_dt_bias[0]))

    wr, wg = RET_HEADS * RET_DV, GDN_HEADS * GDN_DV
    xs = _mix_out(xs, odd_w_out[0].astype(BF16),
                  [(r_f, 0, wr, False), (r_b, 0, wr, False), (h_main, 2, wr, False), (row(ret_norm_w[0]), 0, wr, True),
                   (g_f, 0, wg, False), (g_b, 0, wg, False), (h_main, 6, wg, False), (row(gdn_norm_w[0]), 0, wg, True)],
                  ((RET_HEADS, RET_DV), (GDN_HEADS, GDN_DV)))
    xs = _ffn(xs, row(norm_ffn_w[1]), w_up, w_down, row(final_norm_w), layer=1, final_norm=True)
    return xs.reshape(b, s, d)
```

```python
import functools
import math

import jax
import jax.numpy as jnp
from jax import lax
from jax.experimental import pallas as pl
from jax.experimental.pallas import tpu as pltpu

F32 = jnp.float32
BF16 = jnp.bfloat16

EPS = 1e-6
CHUNK = 64
ROPE_THETA = 10000.0
LANES = 128
SUBLANES = 8

HG_HEADS, HG_DK, HG_DV = 8, 128, 128
MLA_HEADS, MLA_NOPE, MLA_ROPE, MLA_V = 8, 128, 64, 128
MLA_Q_RANK, MLA_KV_RANK = 512, 256
MLA_QK_PAD = 256
RET_HEADS, RET_DK, RET_DV = 4, 128, 256
RET_UNROLL = 2
GDN_HEADS, GDN_DK, GDN_DV = 8, 128, 128
CONV_WIDTH = 5

NT_DIMS = (((1,), (1,)), ((), ()))
TN_DIMS = (((0,), (0,)), ((), ()))


def _params(semantics, vmem_mib):
    return pltpu.CompilerParams(dimension_semantics=semantics, vmem_limit_bytes=vmem_mib * 1024 * 1024)


def _rms_rows(x, w):
    ms = jnp.mean(x * x, axis=-1, keepdims=True)
    return x * lax.rsqrt(ms + EPS) * w


def _silu(x):
    return x * jax.nn.sigmoid(x)


def _bdot(a, b):
    return jnp.dot(a.astype(BF16), b.astype(BF16), preferred_element_type=F32)


def _bdot_nt(a, b):
    return lax.dot_general(a.astype(BF16), b.astype(BF16), NT_DIMS, preferred_element_type=F32)


def _bdot_tn(a, b):
    return lax.dot_general(a.astype(BF16), b.astype(BF16), TN_DIMS, preferred_element_type=F32)


def _rhs_split(b):
    hi = b.astype(BF16)
    lo = (b - hi.astype(F32)).astype(BF16)
    return jnp.concatenate([hi, hi, lo], axis=0)


def _rhs_split3(b):
    b1 = b.astype(BF16)
    r = b - b1.astype(F32)
    b2 = r.astype(BF16)
    b3 = (r - b2.astype(F32)).astype(BF16)
    return jnp.concatenate([b1, b2, b3], axis=0)


def _sdot(lhs_split, rhs_split):
    return jnp.dot(lhs_split, rhs_split, preferred_element_type=F32)


def _tri3(rev):
    row = lax.broadcasted_iota(jnp.int32, (CHUNK, 3 * CHUNK), 0)
    col = lax.broadcasted_iota(jnp.int32, (CHUNK, 3 * CHUNK), 1) & (CHUNK - 1)
    return jnp.where(row <= col if rev else row >= col, 1.0, 0.0).astype(BF16)


def _rope_table_kernel(pos_ref, inv_ref, cos_ret_ref, sin_ret_ref, cos_mla_ref, sin_mla_ref):
    ang = pos_ref[...].astype(F32) * inv_ref[...]
    cos, sin = jnp.cos(ang), jnp.sin(ang)
    lane = lax.broadcasted_iota(jnp.int32, cos.shape, 1)
    half, quarter = LANES // 2, LANES // 4
    cos_ret_ref[...] = jnp.where(lane < half, cos, pltpu.roll(cos, half, 1))
    sin_ret_ref[...] = jnp.where(lane < half, -sin, pltpu.roll(sin, half, 1))

    def mla(t):
        return jnp.where(lane < quarter, pltpu.roll(t, half, 1),
                         jnp.where(lane < half, pltpu.roll(t, half + quarter, 1), 0.0))

    cos_mla_ref[...] = mla(cos)
    sin_mla_ref[...] = mla(sin)


def _rope_tables(pos_col, inv_row):
    s = pos_col.shape[0]
    tm = min(512, s)
    tab = pl.BlockSpec((tm, LANES), lambda i: (i, 0))
    return pl.pallas_call(
        _rope_table_kernel, grid=(s // tm,),
        in_specs=[pl.BlockSpec((tm, 1), lambda i: (i, 0)), pl.BlockSpec((1, LANES), lambda i: (0, 0))],
        out_specs=[tab] * 4,
        out_shape=[jax.ShapeDtypeStruct((s, LANES), F32)] * 4,
        compiler_params=_params(("parallel",), 16), name="rope_tables",
    )(pos_col, inv_row)


NORM_ROWS = 128


def _norm_rows_to(x_ref, nw_ref, xn_ref, tm):
    def body(r, carry):
        rows = pl.ds(pl.multiple_of(r * NORM_ROWS, NORM_ROWS), NORM_ROWS)
        xn_ref[rows, :] = _rms_rows(x_ref[rows, :], nw_ref[...]).astype(BF16)
        return carry
    lax.fori_loop(0, tm // NORM_ROWS, body, 0)


def _norm_mm_kernel(*refs, tm, n_a):
    x_ref, nw_ref, wa_ref = refs[:3]
    wb_ref = refs[3] if len(refs) == 8 else None
    wst_ref, o_ref, os_ref, xn_ref = refs[-4:]
    j = pl.program_id(1)

    @pl.when(j == 0)
    def _():
        _norm_rows_to(x_ref, nw_ref, xn_ref, tm)
        os_ref[...] = lax.dot_general(xn_ref[...], wst_ref[...], NT_DIMS, preferred_element_type=F32)

    def project(wt_ref):
        o_ref[...] = lax.dot_general(xn_ref[...], wt_ref[...], NT_DIMS, preferred_element_type=F32)

    if wb_ref is None:
        project(wa_ref)
    else:
        pl.when(j < n_a)(lambda: project(wa_ref))
        pl.when(j >= n_a)(lambda: project(wb_ref))


def _norm_matmul(x, nw, wt_a, n_a_rows, wt_b, wt_side, *, tm=1024, tn=512):
    s, d = x.shape
    ns = wt_side.shape[0]
    tm = min(tm, s)
    n_a = n_a_rows // tn
    n = n_a_rows + (0 if wt_b is None else wt_b.shape[0])
    w_specs = [pl.BlockSpec((tn, d), lambda i, j: (jnp.minimum(j, n_a - 1), 0))]
    weights = [wt_a]
    if wt_b is not None:
        w_specs.append(pl.BlockSpec((tn, d), lambda i, j: (jnp.maximum(j - n_a, 0), 0),
                                    pipeline_mode=pl.Buffered(1)))
        weights.append(wt_b)
    return pl.pallas_call(
        functools.partial(_norm_mm_kernel, tm=tm, n_a=n_a), grid=(s // tm, n // tn),
        in_specs=[pl.BlockSpec((tm, d), lambda i, j: (i, 0)),
                  pl.BlockSpec((1, d), lambda i, j: (0, 0)),
                  *w_specs,
                  pl.BlockSpec((ns, d), lambda i, j: (0, 0))],
        out_specs=[pl.BlockSpec((tm, tn), lambda i, j: (i, j)),
                   pl.BlockSpec((tm, ns), lambda i, j: (i, 0))],
        out_shape=[jax.ShapeDtypeStruct((s, n), F32), jax.ShapeDtypeStruct((s, ns), F32)],
        scratch_shapes=[pltpu.VMEM((tm, d), BF16)],
        compiler_params=_params(("parallel", "arbitrary"), 52), name="norm_matmul",
    )(x, nw, *weights, wt_side)


def _gla_chunks(ps, st_ref):
    for p in ps:
        p["qe"] = p["q"] * jnp.exp(p["b"] - p["b_mid"])
        p["ke"] = p["k"] * jnp.exp(p["b_mid"] - p["b"])
        p["scores"] = jnp.where(p["mask"], _bdot_nt(p["qe"], p["ke"]), 0.0)
    for p in ps:
        q_dec = p["qe"] * jnp.exp(p["b_mid"])
        p["o"] = _bdot(p["scores"], p["v"]) + _bdot_nt(q_dec, st_ref[p["slot"]])
    for p in ps:
        k_dec = p["ke"] * jnp.exp(p["b_last"] - p["b_mid"])
        st_ref[p["slot"]] = st_ref[p["slot"]] * jnp.exp(p["b_last"]) + _bdot_tn(p["v"], k_dec)
        p["o_ref"][p["rows"], p["cols"]] = p["o"]


def _chunk_masks():
    row = lax.broadcasted_iota(jnp.int32, (CHUNK, CHUNK), 0)
    col = lax.broadcasted_iota(jnp.int32, (CHUNK, CHUNK), 1)
    return row, col


def _hgrn_kernel(logit_ref, qf_ref, ff_ref, vf_ref, qb_ref, fb_ref, vb_ref, of_ref, ob_ref, st_ref,
                 *, tb, layer):
    @pl.when(pl.program_id(0) == 0)
    def _():
        st_ref[...] = jnp.zeros_like(st_ref)

    lg = logit_ref[...]
    e = jnp.exp(lg - jnp.max(lg, axis=0, keepdims=True))
    lb = jnp.sum(e[0:layer + 1], axis=0, keepdims=True) / jnp.sum(e, axis=0, keepdims=True)
    row, col = _chunk_masks()
    causal, anti = row >= col, row <= col
    tri3_f, tri3_b = _tri3(False), _tri3(True)
    n = tb // CHUNK
    mid = CHUNK // 2
    scale = HG_DK ** -0.5

    def problem(q_ref, f_ref, v_ref, o_ref, rows, h, rev):
        cs = slice(h * HG_DK, (h + 1) * HG_DK)
        lbh = lb[:, cs]
        sig = jax.nn.sigmoid(f_ref[rows, cs])
        f = lbh + (1.0 - lbh) * sig
        b = _sdot(tri3_b if rev else tri3_f, _rhs_split3(jnp.log(f)))
        if rev:
            b_last, b_mid = b[0:1], b[CHUNK - 1 - mid:CHUNK - mid]
        else:
            b_last, b_mid = b[CHUNK - 1:CHUNK], b[mid:mid + 1]
        return dict(q=_silu(q_ref[rows, cs]) * scale, k=(1.0 - lbh) * (1.0 - sig), v=v_ref[rows, cs],
                    b=b, b_mid=b_mid, b_last=b_last, mask=anti if rev else causal,
                    slot=(HG_HEADS if rev else 0) + h, o_ref=o_ref, rows=rows, cols=cs)

    def body(c, carry):
        rf = pl.ds(pl.multiple_of(c * CHUNK, CHUNK), CHUNK)
        rb = pl.ds(pl.multiple_of((n - 1 - c) * CHUNK, CHUNK), CHUNK)
        ps = [problem(qf_ref, ff_ref, vf_ref, of_ref, rf, h, False) for h in range(HG_HEADS)]
        ps += [problem(qb_ref, fb_ref, vb_ref, ob_ref, rb, h, True) for h in range(HG_HEADS)]
        _gla_chunks(ps, st_ref)
        return carry

    lax.fori_loop(0, n, body, 0)


def _hgrn2(h_main, lb_logits, *, layer, tb=512):
    s = h_main.shape[0]
    tb = min(tb, s)
    nb = s // tb
    w = HG_HEADS * HG_DK
    fwd = lambda cb: pl.BlockSpec((tb, w), lambda i: (i, cb))
    bwd = lambda cb: pl.BlockSpec((tb, w), lambda i: (nb - 1 - i, cb))
    return pl.pallas_call(
        functools.partial(_hgrn_kernel, tb=tb, layer=layer), grid=(nb,),
        in_specs=[pl.BlockSpec(lb_logits.shape, lambda i: (0, 0)),
                  fwd(0), fwd(1), fwd(3), bwd(0), bwd(2), bwd(3)],
        out_specs=[fwd(0), bwd(0)],
        out_shape=[jax.ShapeDtypeStruct((s, w), F32)] * 2,
        scratch_shapes=[pltpu.VMEM((2 * HG_HEADS, HG_DV, HG_DK), F32)],
        compiler_params=_params(("arbitrary",), 40), name="hgrn2_scan",
    )(lb_logits, h_main, h_main, h_main, h_main, h_main, h_main)


def _ret_kernel(qf_ref, kf_ref, vf_ref, cf_ref, sf_ref, qb_ref, kb_ref, vb_ref, cb_ref, sb_ref,
                of_ref, ob_ref, st_ref, *, tb):
    @pl.when(pl.program_id(0) == 0)
    def _():
        st_ref[...] = jnp.zeros_like(st_ref)

    row, col = _chunk_masks()
    n = tb // CHUNK
    scale = RET_DK ** -0.5
    t_idx = lax.broadcasted_iota(jnp.int32, (CHUNK, RET_DK), 0).astype(F32)
    log_gamma = [math.log1p(-2.0 ** (-5 - h)) for h in range(RET_HEADS)]

    def rope(x, cos, sin):
        return x * cos + pltpu.roll(x, RET_DK // 2, 1) * sin

    def decay_tables(h, rev):
        lgam = log_gamma[RET_HEADS - 1 - h] if rev else log_gamma[h]
        steps = (CHUNK - t_idx) if rev else (t_idx + 1.0)
        dist = ((col - row) if rev else (row - col)).astype(F32)
        return dict(intra=jnp.where(dist >= 0.0, jnp.exp(dist * lgam), 0.0),
                    q_fac=jnp.exp(steps * lgam), k_fac=jnp.exp((CHUNK - steps) * lgam),
                    st_fac=math.exp(CHUNK * lgam))

    tables = {(h, rev): decay_tables(h, rev) for h in range(RET_HEADS) for rev in (False, True)}

    def problem(q_ref, k_ref, v_ref, c_ref, s_ref, o_ref, rows, h, rev):
        cs = slice(h * RET_DK, (h + 1) * RET_DK)
        vs = slice(h * RET_DV, (h + 1) * RET_DV)
        cos, sin = c_ref[rows, :], s_ref[rows, :]
        return dict(q=rope(q_ref[rows, cs], cos, sin) * scale, k=rope(k_ref[rows, cs], cos, sin),
                    v=v_ref[rows, vs], slot=(RET_HEADS if rev else 0) + h, o_ref=o_ref, rows=rows, cols=vs,
                    **tables[(h, rev)])

    def body(c, carry):
        groups = []
        for u in range(RET_UNROLL):
            rf = pl.ds(pl.multiple_of((RET_UNROLL * c + u) * CHUNK, CHUNK), CHUNK)
            rb = pl.ds(pl.multiple_of((n - 1 - RET_UNROLL * c - u) * CHUNK, CHUNK), CHUNK)
            groups.append(
                [problem(qf_ref, kf_ref, vf_ref, cf_ref, sf_ref, of_ref, rf, h, False) for h in range(RET_HEADS)]
                + [problem(qb_ref, kb_ref, vb_ref, cb_ref, sb_ref, ob_ref, rb, h, True) for h in range(RET_HEADS)])
        for ps in groups:
            for p in ps:
                p["scores"] = _bdot_nt(p["q"], p["k"]) * p["intra"]
                p["update"] = _bdot_tn(p["v"], p["k"] * p["k_fac"])
        for ps in groups:
            for p in ps:
                p["o"] = _bdot(p["scores"], p["v"]) + _bdot_nt(p["q"] * p["q_fac"], st_ref[p["slot"]])
            for p in ps:
                st_ref[p["slot"]] = st_ref[p["slot"]] * p["st_fac"] + p["update"]
                p["o_ref"][p["rows"], p["cols"]] = p["o"]
        return carry

    assert n % RET_UNROLL == 0
    lax.fori_loop(0, n // RET_UNROLL, body, 0)


def _retention(h_main, cos_tab, sin_tab, *, tb=512):
    s = h_main.shape[0]
    tb = min(tb, s)
    nb = s // tb
    wk, wv = RET_HEADS * RET_DK, RET_HEADS * RET_DV
    fwd = lambda w, cb: pl.BlockSpec((tb, w), lambda i: (i, cb))
    bwd = lambda w, cb: pl.BlockSpec((tb, w), lambda i: (nb - 1 - i, cb))
    return pl.pallas_call(
        functools.partial(_ret_kernel, tb=tb), grid=(nb,),
        in_specs=[fwd(wk, 0), fwd(wk, 1), fwd(wv, 1), fwd(LANES, 0), fwd(LANES, 0),
                  bwd(wk, 0), bwd(wk, 1), bwd(wv, 1), bwd(LANES, 0), bwd(LANES, 0)],
        out_specs=[fwd(wv, 0), bwd(wv, 0)],
        out_shape=[jax.ShapeDtypeStruct((s, wv), F32)] * 2,
        scratch_shapes=[pltpu.VMEM((2 * RET_HEADS, RET_DV, RET_DK), F32)],
        compiler_params=_params(("arbitrary",), 40), name="retention_scan",
    )(h_main, h_main, h_main, cos_tab, sin_tab, h_main, h_main, h_main, cos_tab, sin_tab)


def _gated_head_norm(of_ref, ob_ref, gate_ref, nw_ref, rows, heads, hd):
    o = of_ref[rows, :] + ob_ref[rows, :]
    parts = []
    for h in range(heads):
        y = o[:, h * hd:(h + 1) * hd]
        parts.append(y * lax.rsqrt(jnp.mean(y * y, axis=-1, keepdims=True) + EPS))
    y = jnp.concatenate(parts, axis=-1)
    return y * nw_ref[...] * _silu(gate_ref[rows, :])


def _mixout_kernel(*refs, tm, groups):
    x_ref, w_ref = refs[0], refs[1]
    out_ref, lhs_ref = refs[-2], refs[-1]
    grefs = refs[2:-2]

    def body(r, carry):
        rows = pl.ds(pl.multiple_of(r * NORM_ROWS, NORM_ROWS), NORM_ROWS)
        pos, col = 0, 0
        for g in groups:
            if g is None:
                val, width = grefs[pos][rows, :], grefs[pos].shape[1]
                pos += 1
            else:
                heads, hd = g
                val, width = _gated_head_norm(*grefs[pos:pos + 4], rows, heads, hd), heads * hd
                pos += 4
            lhs_ref[rows, col:col + width] = val.astype(BF16)
            col += width
        return carry
    lax.fori_loop(0, tm // NORM_ROWS, body, 0)

    out_ref[...] = x_ref[...] + jnp.dot(lhs_ref[...], w_ref[...], preferred_element_type=F32)


def _mix_out(x, w_out, group_args, groups, *, tm=256):
    s, d = x.shape
    tm = min(tm, s)
    specs = [pl.BlockSpec((tm, d), lambda i: (i, 0)),
             pl.BlockSpec(w_out.shape, lambda i: (0, 0), pipeline_mode=pl.Buffered(1))]
    arrays = [x, w_out]
    for arr, cb, width, rowvec in group_args:
        arrays.append(arr)
        if rowvec:
            specs.append(pl.BlockSpec((1, width), lambda i: (0, 0)))
        else:
            specs.append(pl.BlockSpec((tm, width), lambda i, cb=cb: (i, cb)))
    return pl.pallas_call(
        functools.partial(_mixout_kernel, tm=tm, groups=groups), grid=(s // tm,),
        in_specs=specs,
        out_specs=pl.BlockSpec((tm, d), lambda i: (i, 0)),
        out_shape=jax.ShapeDtypeStruct((s, d), F32),
        scratch_shapes=[pltpu.VMEM((tm, w_out.shape[0]), BF16)],
        compiler_params=_params(("parallel",), 40), name="mix_out",
    )(*arrays)


def _rot_pair(pr, cos, sin):
    return pr * cos + pltpu.roll(pr, MLA_ROPE, 1) * sin


def _mla_proj_kernel(cq_ref, ckv_ref, kpe_ref, qnw_ref, kvnw_ref, wq_ref, wkv_ref, cos_ref, sin_ref,
                     q_ref, k_ref, v_ref, *, scale):
    cos, sin = cos_ref[...], sin_ref[...]
    cqn = _rms_rows(cq_ref[...], qnw_ref[...]).astype(BF16)
    ckvn = _rms_rows(ckv_ref[...], kvnw_ref[...]).astype(BF16)
    pe = _rot_pair(kpe_ref[...], cos, sin).astype(BF16)
    lane = lax.broadcasted_iota(jnp.int32, pe.shape, 1)
    ones_col = jnp.where(lane == 0, 1.0, 0.0).astype(BF16)
    for h in range(MLA_HEADS):
        cols = slice(h * MLA_QK_PAD, (h + 1) * MLA_QK_PAD)
        yq = jnp.dot(cqn, wq_ref[:, cols], preferred_element_type=F32)
        q_rope = _rot_pair(yq[:, MLA_NOPE:], cos, sin)
        q_ref[h] = (jnp.concatenate([yq[:, :MLA_NOPE], q_rope], axis=1) * scale).astype(BF16)
        ykv = jnp.dot(ckvn, wkv_ref[:, cols], preferred_element_type=F32)
        k_ref[h] = jnp.concatenate([ykv[:, :MLA_NOPE].astype(BF16), pe], axis=1)
        v_ref[h] = jnp.concatenate([ykv[:, MLA_NOPE:].astype(BF16), ones_col], axis=1)


def _mla_proj(h_main, h_side, q_norm_w, kv_norm_w, wq, wkv, cos_tab, sin_tab, *, tm=512):
    s = h_main.shape[0]
    tm = min(tm, s)
    cq_block = (HG_HEADS * (3 * HG_DK + 2 * HG_DV)) // MLA_Q_RANK
    scale = (MLA_NOPE + MLA_ROPE) ** -0.5 * math.log2(math.e)
    full = lambda a: pl.BlockSpec(a.shape, lambda i: (0, 0))
    tab = pl.BlockSpec((tm, LANES), lambda i: (i, 0))
    head_out = pl.BlockSpec((MLA_HEADS, tm, MLA_QK_PAD), lambda i: (0, i, 0))
    return pl.pallas_call(
        functools.partial(_mla_proj_kernel, scale=scale), grid=(s // tm,),
        in_specs=[pl.BlockSpec((tm, MLA_Q_RANK), lambda i: (i, cq_block)),
                  pl.BlockSpec((tm, MLA_KV_RANK), lambda i: (i, 0)),
                  pl.BlockSpec((tm, LANES), lambda i: (i, MLA_KV_RANK // LANES)),
                  full(q_norm_w), full(kv_norm_w), full(wq), full(wkv), tab, tab],
        out_specs=[head_out, head_out, head_out],
        out_shape=[jax.ShapeDtypeStruct((MLA_HEADS, s, MLA_QK_PAD), BF16)] * 3,
        compiler_params=_params(("parallel",), 40), name="mla_proj",
    )(h_main, h_side, h_side, q_norm_w, kv_norm_w, wq, wkv, cos_tab, sin_tab)


ATTN_SPLIT = 2
ATTN_ROWS = 32


def _attn_kernel(q_ref, k_ref, v_ref, *refs, tk, nk, n_cast):
    cast_in, o_ref, cast_out = refs[:n_cast], refs[n_cast], refs[n_cast + 1:2 * n_cast + 1]
    s_ref, p_ref, m_ref, alpha_ref, acc_ref = refs[2 * n_cast + 1:]
    tq = q_ref.shape[1]
    sub = tq // ATTN_SPLIT
    m_ref[...] = jnp.full(m_ref.shape, -jnp.inf, F32)
    acc_ref[...] = jnp.zeros(acc_ref.shape, F32)

    def softmax_rows(i):
        for rb in range(sub // ATTN_ROWS):
            loc = slice(rb * ATTN_ROWS, (rb + 1) * ATTN_ROWS)
            glob = slice(i * sub + rb * ATTN_ROWS, i * sub + (rb + 1) * ATTN_ROWS)
            sc = s_ref[i, loc, :]
            cols = [sc[:, c * LANES:(c + 1) * LANES] for c in range(tk // LANES)]
            col_max = functools.reduce(jnp.maximum, cols)
            m_old = m_ref[glob, :]
            m_new = jnp.maximum(m_old, jnp.max(col_max, axis=-1, keepdims=True))
            m_ref[glob, :] = m_new
            alpha_ref[glob, :] = jnp.exp2(m_old - m_new)
            p_ref[i, loc, :] = jnp.concatenate([jnp.exp2(c - m_new) for c in cols], axis=1).astype(BF16)

    def body(j, carry):
        rows = pl.ds(pl.multiple_of(j * tk, tk), tk)
        k, v = k_ref[0, rows, :], v_ref[0, rows, :]
        for i in range(ATTN_SPLIT):
            s_ref[i] = lax.dot_general(q_ref[0, i * sub:(i + 1) * sub, :], k, NT_DIMS, preferred_element_type=F32)
        for i in range(ATTN_SPLIT):
            softmax_rows(i)
            rs = slice(i * sub, (i + 1) * sub)
            alpha = alpha_ref[rs, :]
            acc_ref[rs, :] = (jnp.concatenate([alpha, alpha], axis=1) * acc_ref[rs, :]
                              + jnp.dot(p_ref[i], v, preferred_element_type=F32))
        for src, dst in zip(cast_in, cast_out):
            if src.shape[1] % (2 * SUBLANES * nk) == 0:
                piece = src.shape[1] // nk
                part = pl.ds(pl.multiple_of(j * piece, piece), piece)
                dst[0, part, :] = src[0, part, :].astype(BF16)
        return carry

    for src, dst in zip(cast_in, cast_out):
        if src.shape[1] % (2 * SUBLANES * nk) != 0:
            dst[...] = src[...].astype(BF16)

    lax.fori_loop(0, nk, body, 0)
    acc = acc_ref[...]
    o_ref[...] = (acc[:, :MLA_V] / acc[:, MLA_V:MLA_V + 1]).astype(BF16)


def _attention(q, k, v, to_cast, *, tq=1024, tk=2048):
    heads, s, _ = q.shape
    tq, tk = min(tq, s), min(tk, s)
    sub = tq // ATTN_SPLIT
    nq, nk = s // tq, s // tk
    steps = heads * nq
    cast_specs = []
    for a in to_cast:
        layers, rows, cols = a.shape
        block = next(b for b in range(-(-layers * rows // steps), rows + 1)
                     if rows % b == 0 and b % (2 * SUBLANES) == 0)
        per_layer, last = rows // block, layers * rows // block - 1

        def index(h, i, per_layer=per_layer, last=last):
            t = jnp.minimum(h * nq + i, last)
            return t // per_layer, t % per_layer, 0

        cast_specs.append(pl.BlockSpec((1, block, cols), index))
    outs = pl.pallas_call(
        functools.partial(_attn_kernel, tk=tk, nk=nk, n_cast=len(to_cast)), grid=(heads, nq),
        in_specs=[pl.BlockSpec((1, tq, MLA_QK_PAD), lambda h, i: (h, i, 0)),
                  pl.BlockSpec((1, s, MLA_QK_PAD), lambda h, i: (h, 0, 0)),
                  pl.BlockSpec((1, s, 2 * MLA_V), lambda h, i: (h, 0, 0))] + cast_specs,
        out_specs=[pl.BlockSpec((tq, MLA_V), lambda h, i: (i, h))] + cast_specs,
        out_shape=[jax.ShapeDtypeStruct((s, heads * MLA_V), BF16)]
                  + [jax.ShapeDtypeStruct(a.shape, BF16) for a in to_cast],
        scratch_shapes=[pltpu.VMEM((ATTN_SPLIT, sub, tk), F32), pltpu.VMEM((ATTN_SPLIT, sub, tk), BF16),
                        pltpu.VMEM((tq, LANES), F32), pltpu.VMEM((tq, LANES), F32),
                        pltpu.VMEM((tq, 2 * MLA_V), F32)],
        compiler_params=_params(("arbitrary", "arbitrary"), 56), name="mla_attention",
    )(q, k, v, *to_cast)
    return outs[0], outs[1:]


def _ffn_kernel(x_ref, nw_ref, wu_ref, wd_ref, fw_ref, o_ref, xn_ref, *, tm, nf, final_norm):
    f = pl.program_id(1)

    @pl.when(f == 0)
    def _():
        _norm_rows_to(x_ref, nw_ref, xn_ref, tm)
        o_ref[...] = x_ref[...]

    a = jnp.maximum(jnp.dot(xn_ref[...], wu_ref[0], preferred_element_type=F32), 0.0)
    o_ref[...] += jnp.dot((a * a).astype(BF16), wd_ref[0], preferred_element_type=F32)

    if final_norm:
        @pl.when(f == nf - 1)
        def _():
            def body(r, carry):
                rows = pl.ds(pl.multiple_of(r * NORM_ROWS, NORM_ROWS), NORM_ROWS)
                o_ref[rows, :] = _rms_rows(o_ref[rows, :], fw_ref[...])
                return carry
            lax.fori_loop(0, tm // NORM_ROWS, body, 0)


def _ffn(x, nw, w_up, w_down, final_w, *, layer, final_norm, tm=512, tf=1024):
    s, d = x.shape
    dff = w_up.shape[2]
    tm = min(tm, s)
    nf = dff // tf
    return pl.pallas_call(
        functools.partial(_ffn_kernel, tm=tm, nf=nf, final_norm=final_norm), grid=(s // tm, nf),
        in_specs=[pl.BlockSpec((tm, d), lambda i, f: (i, 0)),
                  pl.BlockSpec((1, d), lambda i, f: (0, 0)),
                  pl.BlockSpec((1, d, tf), lambda i, f: (layer, 0, f)),
                  pl.BlockSpec((1, tf, d), lambda i, f: (layer, f, 0)),
                  pl.BlockSpec((1, d), lambda i, f: (0, 0))],
        out_specs=pl.BlockSpec((tm, d), lambda i, f: (i, 0)),
        out_shape=jax.ShapeDtypeStruct((s, d), F32),
        scratch_shapes=[pltpu.VMEM((tm, d), BF16)],
        compiler_params=_params(("parallel", "arbitrary"), 48), name="ffn",
    )(x, nw, w_up, w_down, final_w)


def _gdn_prep_kernel(x_ref, xp_ref, xn_ref, cw_ref, o_ref, *, tm, nblk):
    i, sec = pl.program_id(0), pl.program_id(1)
    x = x_ref[...]
    prev = jnp.where(i > 0, xp_ref[...], 0.0)
    nxt = jnp.where(i < nblk - 1, xn_ref[...], 0.0)
    rows = lax.broadcasted_iota(jnp.int32, prev.shape, 0)
    half = CONV_WIDTH // 2

    def shifted(d):
        if d == 0:
            return x
        r = pltpu.roll(x, (-d) % tm, 0)
        if d < 0:
            edge = r[:SUBLANES]
            for t in range(-d):
                edge = jnp.where(rows == t, prev[SUBLANES + t + d:SUBLANES + t + d + 1], edge)
            return jnp.concatenate([edge, r[SUBLANES:]], axis=0)
        edge = r[tm - SUBLANES:]
        for t in range(SUBLANES - d, SUBLANES):
            edge = jnp.where(rows == t, nxt[t + d - SUBLANES:t + d - SUBLANES + 1], edge)
        return jnp.concatenate([r[:tm - SUBLANES], edge], axis=0)

    y = shifted(-half) * cw_ref[0:1]
    for j in range(1, CONV_WIDTH):
        y = y + shifted(j - half) * cw_ref[j:j + 1]
    y = _silu(y)
    parts = []
    for h in range(GDN_HEADS):
        a = y[:, h * GDN_DK:(h + 1) * GDN_DK]
        parts.append(a * lax.rsqrt(jnp.sum(a * a, axis=-1, keepdims=True) + EPS))
    nrm = jnp.concatenate(parts, axis=-1) * jnp.where(sec == 0, GDN_DK ** -0.5, 1.0)
    o_ref[...] = jnp.where(sec == 2, y, nrm)


def _gdn_prep(h_main, conv_w, *, tm=256):
    s = h_main.shape[0]
    tm = min(tm, s)
    nblk = s // tm
    w = GDN_HEADS * GDN_DK
    base = (2 * RET_HEADS * RET_DK + 2 * RET_HEADS * RET_DV) // w
    per = tm // SUBLANES
    return pl.pallas_call(
        functools.partial(_gdn_prep_kernel, tm=tm, nblk=nblk), grid=(nblk, 3),
        in_specs=[pl.BlockSpec((tm, w), lambda i, c: (i, base + c)),
                  pl.BlockSpec((SUBLANES, w), lambda i, c: (jnp.maximum(i * per - 1, 0), base + c)),
                  pl.BlockSpec((SUBLANES, w), lambda i, c: (jnp.minimum((i + 1) * per, s // SUBLANES - 1), base + c)),
                  pl.BlockSpec((CONV_WIDTH, w), lambda i, c: (0, c))],
        out_specs=pl.BlockSpec((tm, w), lambda i, c: (i, c)),
        out_shape=jax.ShapeDtypeStruct((s, 3 * w), F32),
        compiler_params=_params(("parallel", "arbitrary"), 32), name="gdn_prep",
    )(h_main, h_main, h_main, conv_w)


def _softplus(x):
    return jnp.maximum(x, 0.0) + jnp.log1p(jnp.exp(-jnp.abs(x)))


def _pair_lhs(x):
    hi = x.astype(BF16)
    lo = (x - hi.astype(F32)).astype(BF16)
    return jnp.concatenate([hi, lo, hi], axis=1)


def _pair_rhs(y, low_lanes):
    hi = y.astype(BF16)
    lo = (y - hi.astype(F32)).astype(BF16)
    zero = jnp.zeros_like(hi)

    def bd(a):
        return jnp.concatenate([jnp.where(low_lanes, a, zero), jnp.where(low_lanes, zero, a)], axis=0)

    bd_hi = bd(hi)
    return jnp.concatenate([bd_hi, bd_hi, bd(lo)], axis=0)


def _pair_lhs2(x):
    hi = x.astype(BF16)
    lo = (x - hi.astype(F32)).astype(BF16)
    return jnp.concatenate([hi, lo], axis=1)


def _pair_rhs1(y, low_lanes):
    hi = y.astype(BF16)
    zero = jnp.zeros_like(hi)
    bd = jnp.concatenate([jnp.where(low_lanes, hi, zero), jnp.where(low_lanes, zero, hi)], axis=0)
    return jnp.concatenate([bd, bd], axis=0)


def _pair_rows(x, low_lanes):
    return jnp.concatenate([jnp.where(low_lanes, x, 0.0), jnp.where(low_lanes, 0.0, x)], axis=0)


def _gdn_kernel(alog_ref, dt_ref, qf_ref, kf_ref, vf_ref, gf_ref, qb_ref, kb_ref, vb_ref, gb_ref,
                of_ref, ob_ref, st_ref, *, tb):
    @pl.when(pl.program_id(0) == 0)
    def _():
        st_ref[...] = jnp.zeros_like(st_ref)

    row = lax.broadcasted_iota(jnp.int32, (CHUNK, LANES), 0)
    lane = lax.broadcasted_iota(jnp.int32, (CHUNK, LANES), 1)
    col = lane & (CHUNK - 1)
    low_lanes = lane < CHUNK
    low_row = low_lanes[0:1]
    causal, anti = row >= col, row <= col
    strict_c, strict_a = row > col, row < col
    tri3_f, tri3_b = _tri3(False), _tri3(True)
    eye = (row == col).astype(F32)
    same16 = (row >> 4) == (col >> 4)
    same32 = (row >> 5) == (col >> 5)
    in32 = jnp.logical_and(same32, jnp.logical_not(same16))
    n = tb // CHUNK
    neg_a = -jnp.exp(alog_ref[...])
    dt = dt_ref[...]

    def pairs(q_ref, k_ref, v_ref, g_ref, o_ref, rows, rev):
        raw = g_ref[rows, :]
        la = neg_a * _softplus(raw + dt)
        beta = jax.nn.sigmoid(raw)
        g = _sdot(tri3_b if rev else tri3_f, _rhs_split3(la))
        g_rows = jnp.concatenate([g, g], axis=0).T
        la0 = GDN_HEADS if rev else 0
        b0 = (3 if rev else 2) * GDN_HEADS
        last = 0 if rev else CHUNK - 1
        out = []
        for j in range(GDN_HEADS // 2):
            hs = (2 * j, 2 * j + 1)
            g_cols = [jnp.broadcast_to(g[:, la0 + h:la0 + h + 1], (CHUNK, LANES)) for h in hs]
            b_cols = [jnp.broadcast_to(beta[:, b0 + h:b0 + h + 1], (CHUNK, LANES)) for h in hs]
            g_col = jnp.where(low_lanes, g_cols[0], g_cols[1])
            g_row = jnp.where(low_row, g_rows[la0 + hs[0]:la0 + hs[0] + 1], g_rows[la0 + hs[1]:la0 + hs[1] + 1])
            out.append(dict(
                q_ref=q_ref, k_ref=k_ref, v_ref=v_ref, o_ref=o_ref, rows=rows,
                cs=[slice(h * GDN_DK, (h + 1) * GDN_DK) for h in hs],
                slots=[(GDN_HEADS if rev else 0) + h for h in hs],
                strict=strict_a if rev else strict_c,
                decay=jnp.exp(jnp.where(anti if rev else causal, g_col - g_row, -jnp.inf)),
                g_cols=g_cols, b_cols=b_cols,
                g_last=[g[last:last + 1, la0 + h:la0 + h + 1] for h in hs]))
        return out

    def body(c, carry):
        rf = pl.ds(pl.multiple_of(c * CHUNK, CHUNK), CHUNK)
        rb = pl.ds(pl.multiple_of((n - 1 - c) * CHUNK, CHUNK), CHUNK)
        ps = (pairs(qf_ref, kf_ref, vf_ref, gf_ref, of_ref, rf, False)
              + pairs(qb_ref, kb_ref, vb_ref, gb_ref, ob_ref, rb, True))
        for p in ps:
            ks = [p["k_ref"][p["rows"], cs] for cs in p["cs"]]
            qs = [p["q_ref"][p["rows"], cs] for cs in p["cs"]]
            p["kb"] = [k * b for k, b in zip(ks, p["b_cols"])]
            lhs = jnp.concatenate([jnp.concatenate([kb, q], axis=0) for kb, q in zip(p["kb"], qs)], axis=1)
            zero = jnp.zeros_like(ks[0])
            k_bd = jnp.concatenate([jnp.concatenate([ks[0], zero], axis=1),
                                    jnp.concatenate([zero, ks[1]], axis=1)], axis=0)
            kq = _bdot_nt(lhs, k_bd)
            a = jnp.where(p["strict"], kq[:CHUNK] * p["decay"], 0.0)
            p["qk"] = kq[CHUNK:] * p["decay"]
            n1 = -jnp.where(same16, a, 0.0)
            p["n_l"], p["n_r"], p["t"] = _pair_lhs2(n1), _pair_rhs1(n1, low_lanes), eye + n1
            p["e32"] = _pair_rhs1(jnp.where(in32, a, 0.0), low_lanes)
            p["e64"] = _pair_rhs1(jnp.where(same32, 0.0, a), low_lanes)
        for step in range(3):
            for p in ps:
                nn = _sdot(p["n_l"], p["n_r"])
                p["n_r"] = _pair_rhs(nn, low_lanes)
                if step < 2:
                    p["n_l"] = _pair_lhs(nn)
            for p in ps:
                p["t"] = p["t"] + _sdot(_pair_lhs(p["t"]), p["n_r"])
        for e in ("e32", "e64"):
            for p in ps:
                p["x"] = _sdot(_pair_lhs2(p["t"]), p[e])
            for p in ps:
                p["t"] = p["t"] - _sdot(_pair_lhs(p["x"]), _pair_rhs(p["t"], low_lanes))
        for p in ps:
            rhs = []
            for cs, kb, b, g in zip(p["cs"], p["kb"], p["b_cols"], p["g_cols"]):
                rhs.append(jnp.concatenate([p["v_ref"][p["rows"], cs] * b, kb * jnp.exp(g)], axis=1))
            p["sol"] = _sdot(_pair_lhs(_pair_rows(p["t"], low_lanes)), _rhs_split(jnp.concatenate(rhs, axis=0)))
        for p in ps:
            p["v_new"], p["o"] = [], []
            for i, (cs, slot, g) in enumerate(zip(p["cs"], p["slots"], p["g_cols"])):
                sol = p["sol"][i * CHUNK:(i + 1) * CHUNK]
                q = p["q_ref"][p["rows"], cs]
                ws = _bdot(jnp.concatenate([sol[:, GDN_DV:], q * jnp.exp(g)], axis=0), st_ref[slot])
                p["v_new"].append(sol[:, :GDN_DV] - ws[:CHUNK])
                p["o"].append(ws[CHUNK:])
        for p in ps:
            o_intra = _bdot(_pair_rows(p["qk"], low_lanes), jnp.concatenate(p["v_new"], axis=0))
            for i, (cs, slot, g, g_last) in enumerate(zip(p["cs"], p["slots"], p["g_cols"], p["g_last"])):
                k = p["k_ref"][p["rows"], cs]
                p["o_ref"][p["rows"], cs] = p["o"][i] + o_intra[i * CHUNK:(i + 1) * CHUNK]
                st_ref[slot] = (st_ref[slot] * jnp.exp(g_last)
                                + _bdot_tn(k * jnp.exp(g_last - g), p["v_new"][i]))
        return carry

    lax.fori_loop(0, n, body, 0)


def _gdn(qkv, gates, alog_row, dt_row, *, tb=512):
    s = qkv.shape[0]
    tb = min(tb, s)
    nb = s // tb
    w = GDN_HEADS * GDN_DK
    fwd = lambda width, cb: pl.BlockSpec((tb, width), lambda i: (i, cb))
    bwd = lambda width, cb: pl.BlockSpec((tb, width), lambda i: (nb - 1 - i, cb))
    row = pl.BlockSpec((1, LANES), lambda i: (0, 0))
    return pl.pallas_call(
        functools.partial(_gdn_kernel, tb=tb), grid=(nb,),
        in_specs=[row, row,
                  fwd(w, 0), fwd(w, 1), fwd(w, 2), fwd(LANES, 0),
                  bwd(w, 0), bwd(w, 1), bwd(w, 2), bwd(LANES, 0)],
        out_specs=[fwd(w, 0), bwd(w, 0)],
        out_shape=[jax.ShapeDtypeStruct((s, w), F32)] * 2,
        scratch_shapes=[pltpu.VMEM((2 * GDN_HEADS, GDN_DK, GDN_DV), F32)],
        compiler_params=_params(("arbitrary",), 40), name="gdn_scan",
    )(alog_row, dt_row, qkv, qkv, qkv, gates, qkv, qkv, qkv, gates)


def _rot_cols(w_pe):
    half = w_pe.shape[-1] // 2
    return jnp.concatenate([-w_pe[..., half:], w_pe[..., :half]], axis=-1)


def _pad_lanes(v, fill=0.0):
    v = v.reshape(1, -1).astype(F32)
    return jnp.pad(v, ((0, 0), (0, LANES - v.shape[1])), constant_values=fill)


def kernel(x, positions, norm_mix_w, norm_ffn_w, final_norm_w, hg_lb_logits, even_w_in, hg_norm_w,
           mla_q_norm_w, mla_w_q_b, mla_kv_norm_w, mla_w_kv_b, even_w_out, odd_w_in, ret_norm_w,
           gdn_conv_w, gdn_a_log, gdn_dt_bias, gdn_norm_w, odd_w_out, ffn_w_up, ffn_w_down):
    b, s, d = x.shape
    assert b == 1 and s % CHUNK == 0
    xs = x.reshape(s, d)
    row = lambda v: v.reshape(1, -1).astype(F32)

    pos_col = positions.reshape(s, 1)
    inv_ret = ROPE_THETA ** (-jnp.arange(RET_DK // 2, dtype=F32) / (RET_DK // 2))
    inv_mla = ROPE_THETA ** (-jnp.arange(MLA_ROPE // 2, dtype=F32) / (MLA_ROPE // 2))
    inv_row = row(jnp.concatenate([inv_ret, inv_mla, jnp.zeros((LANES - RET_DK // 2 - MLA_ROPE // 2,), F32)]))
    cos_ret, sin_ret, cos_mla, sin_mla = _rope_tables(pos_col, inv_row)

    n_main = HG_HEADS * (3 * HG_DK + 2 * HG_DV) + MLA_Q_RANK
    wt_in = even_w_in[0].T.astype(BF16)
    wt_kpe = wt_in[n_main + MLA_KV_RANK:]
    wt_side = jnp.concatenate([wt_in[n_main:], _rot_cols(wt_kpe.T).T], axis=0)
    h_main, h_side = _norm_matmul(xs, row(norm_mix_w[0]), wt_in, n_main, None, wt_side, tn=512)

    o_f, o_b = _hgrn2(h_main, hg_lb_logits.astype(F32), layer=0)

    wq = mla_w_q_b[0].reshape(MLA_Q_RANK, MLA_HEADS, MLA_NOPE + MLA_ROPE)
    wq_pe = wq[..., MLA_NOPE:]
    wq = jnp.concatenate([wq[..., :MLA_NOPE], wq_pe, _rot_cols(wq_pe)], axis=-1)
    wq = wq.reshape(MLA_Q_RANK, MLA_HEADS * MLA_QK_PAD).astype(BF16)
    q, k, v = _mla_proj(h_main, h_side, row(mla_q_norm_w[0]), row(mla_kv_norm_w[0]), wq,
                        mla_w_kv_b[0].astype(BF16), cos_mla, sin_mla)
    o_attn, (w_up, w_down, w_out_even, w_out_odd, wt_in_odd) = _attention(
        q, k, v, [ffn_w_up, ffn_w_down, even_w_out, odd_w_out, jnp.swapaxes(odd_w_in, 1, 2)])

    wa = HG_HEADS * HG_DV
    xs = _mix_out(xs, w_out_even[0],
                  [(o_f, 0, wa, False), (o_b, 0, wa, False), (h_main, 4, wa, False), (row(hg_norm_w[0]), 0, wa, True),
                   (o_attn, 0, MLA_HEADS * MLA_V, False)],
                  ((HG_HEADS, HG_DV), None))
    xs = _ffn(xs, row(norm_ffn_w[0]), w_up, w_down, row(final_norm_w), layer=0, final_norm=False)

    n_ret = 2 * RET_HEADS * RET_DK + 2 * RET_HEADS * RET_DV
    n_qkv = GDN_HEADS * (2 * GDN_DK + GDN_DV)
    n_gate = 4 * GDN_HEADS
    wt_in = wt_in_odd[0]
    wt_gate = wt_in[n_ret + n_qkv + n_gate:]
    wt_side = jnp.pad(wt_in[n_ret + n_qkv:n_ret + n_qkv + n_gate], ((0, LANES - n_gate), (0, 0)))
    h_main, h_gates = _norm_matmul(xs, row(norm_mix_w[1]), wt_in, n_ret + n_qkv, wt_gate, wt_side, tn=1024)

    r_f, r_b = _retention(h_main, cos_ret, sin_ret)
    qkv = _gdn_prep(h_main, gdn_conv_w[0].astype(F32))
    g_f, g_b = _gdn(qkv, h_gates, _pad_lanes(gdn_a_log[0]), _pad_lanes(gdn_dt_bias[0]))

    wr, wg = RET_HEADS * RET_DV, GDN_HEADS * GDN_DV
    xs = _mix_out(xs, w_out_odd[0],
                  [(r_f, 0, wr, False), (r_b, 0, wr, False), (h_main, 2, wr, False), (row(ret_norm_w[0]), 0, wr, True),
                   (g_f, 0, wg, False), (g_b, 0, wg, False), (h_main, 6, wg, False), (row(gdn_norm_w[0]), 0, wg, True)],
                  ((RET_HEADS, RET_DV), (GDN_HEADS, GDN_DV)))
    xs = _ffn(xs, row(norm_ffn_w[1]), w_up, w_down, row(final_norm_w), layer=1, final_norm=True)
    return xs.reshape(b, s, d)
```

```python
import functools
import math

import jax
import jax.numpy as jnp
from jax import lax
from jax.experimental import pallas as pl
from jax.experimental.pallas import tpu as pltpu

F32 = jnp.float32
BF16 = jnp.bfloat16

EPS = 1e-6
CHUNK = 64
ROPE_THETA = 10000.0
LANES = 128
SUBLANES = 8

HG_HEADS, HG_DK, HG_DV = 8, 128, 128
MLA_HEADS, MLA_NOPE, MLA_ROPE, MLA_V = 8, 128, 64, 128
MLA_Q_RANK, MLA_KV_RANK = 512, 256
MLA_QK_PAD = 256
RET_HEADS, RET_DK, RET_DV = 4, 128, 256
RET_UNROLL = 2
GDN_HEADS, GDN_DK, GDN_DV = 8, 128, 128
CONV_WIDTH = 5

NT_DIMS = (((1,), (1,)), ((), ()))
TN_DIMS = (((0,), (0,)), ((), ()))


def _params(semantics, vmem_mib):
    return pltpu.CompilerParams(dimension_semantics=semantics, vmem_limit_bytes=vmem_mib * 1024 * 1024)


def _rms_rows(x, w):
    ms = jnp.mean(x * x, axis=-1, keepdims=True)
    return x * lax.rsqrt(ms + EPS) * w


def _silu(x):
    return x * jax.nn.sigmoid(x)


def _bdot(a, b):
    return jnp.dot(a.astype(BF16), b.astype(BF16), preferred_element_type=F32)


def _bdot_nt(a, b):
    return lax.dot_general(a.astype(BF16), b.astype(BF16), NT_DIMS, preferred_element_type=F32)


def _bdot_tn(a, b):
    return lax.dot_general(a.astype(BF16), b.astype(BF16), TN_DIMS, preferred_element_type=F32)


def _rhs_split(b):
    hi = b.astype(BF16)
    lo = (b - hi.astype(F32)).astype(BF16)
    return jnp.concatenate([hi, hi, lo], axis=0)


def _rhs_split3(b):
    b1 = b.astype(BF16)
    r = b - b1.astype(F32)
    b2 = r.astype(BF16)
    b3 = (r - b2.astype(F32)).astype(BF16)
    return jnp.concatenate([b1, b2, b3], axis=0)


def _sdot(lhs_split, rhs_split):
    return jnp.dot(lhs_split, rhs_split, preferred_element_type=F32)


def _tri3(rev):
    row = lax.broadcasted_iota(jnp.int32, (CHUNK, 3 * CHUNK), 0)
    col = lax.broadcasted_iota(jnp.int32, (CHUNK, 3 * CHUNK), 1) & (CHUNK - 1)
    return jnp.where(row <= col if rev else row >= col, 1.0, 0.0).astype(BF16)


def _rope_table_kernel(pos_ref, inv_ref, w_ref, cos_ret_ref, sin_ret_ref, cos_mla_ref, sin_mla_ref, wb_ref):
    wb_ref[...] = w_ref[...].astype(BF16)
    ang = pos_ref[...].astype(F32) * inv_ref[...]
    cos, sin = jnp.cos(ang), jnp.sin(ang)
    lane = lax.broadcasted_iota(jnp.int32, cos.shape, 1)
    half, quarter = LANES // 2, LANES // 4
    cos_ret_ref[...] = jnp.where(lane < half, cos, pltpu.roll(cos, half, 1))
    sin_ret_ref[...] = jnp.where(lane < half, -sin, pltpu.roll(sin, half, 1))

    def mla(t):
        return jnp.where(lane < quarter, pltpu.roll(t, half, 1),
                         jnp.where(lane < half, pltpu.roll(t, half + quarter, 1), 0.0))

    cos_mla_ref[...] = mla(cos)
    sin_mla_ref[...] = mla(sin)


def _rope_tables(pos_col, inv_row, w_stack):
    s = pos_col.shape[0]
    tm = min(512, s)
    steps = s // tm
    _, rows, cols = w_stack.shape
    block = next(b for b in range(-(-rows // steps), rows + 1) if rows % b == 0 and b % (2 * SUBLANES) == 0)
    w_spec = pl.BlockSpec((1, block, cols), lambda i: (0, jnp.minimum(i, rows // block - 1), 0))
    tab = pl.BlockSpec((tm, LANES), lambda i: (i, 0))
    return pl.pallas_call(
        _rope_table_kernel, grid=(steps,),
        in_specs=[pl.BlockSpec((tm, 1), lambda i: (i, 0)), pl.BlockSpec((1, LANES), lambda i: (0, 0)), w_spec],
        out_specs=[tab] * 4 + [w_spec],
        out_shape=[jax.ShapeDtypeStruct((s, LANES), F32)] * 4 + [jax.ShapeDtypeStruct(w_stack.shape, BF16)],
        compiler_params=_params(("arbitrary",), 32), name="rope_tables",
    )(pos_col, inv_row, w_stack)


NORM_ROWS = 128


def _norm_rows_to(x_ref, nw_ref, xn_ref, tm):
    def body(r, carry):
        rows = pl.ds(pl.multiple_of(r * NORM_ROWS, NORM_ROWS), NORM_ROWS)
        xn_ref[rows, :] = _rms_rows(x_ref[rows, :], nw_ref[...]).astype(BF16)
        return carry
    lax.fori_loop(0, tm // NORM_ROWS, body, 0)


def _norm_mm_kernel(*refs, tm, n_a):
    x_ref, nw_ref, wa_ref = refs[:3]
    wb_ref = refs[3] if len(refs) == 8 else None
    wst_ref, o_ref, os_ref, xn_ref = refs[-4:]
    j = pl.program_id(1)

    @pl.when(j == 0)
    def _():
        _norm_rows_to(x_ref, nw_ref, xn_ref, tm)
        os_ref[...] = lax.dot_general(xn_ref[...], wst_ref[...], NT_DIMS, preferred_element_type=F32)

    def project(wt_ref):
        o_ref[...] = lax.dot_general(xn_ref[...], wt_ref[...], NT_DIMS, preferred_element_type=F32)

    if wb_ref is None:
        project(wa_ref)
    else:
        pl.when(j < n_a)(lambda: project(wa_ref))
        pl.when(j >= n_a)(lambda: project(wb_ref))


def _norm_matmul(x, nw, wt_a, n_a_rows, wt_b, wt_side, *, tm=1024, tn=512):
    s, d = x.shape
    ns = wt_side.shape[0]
    tm = min(tm, s)
    n_a = n_a_rows // tn
    n = n_a_rows + (0 if wt_b is None else wt_b.shape[0])
    w_specs = [pl.BlockSpec((tn, d), lambda i, j: (jnp.minimum(j, n_a - 1), 0))]
    weights = [wt_a]
    if wt_b is not None:
        w_specs.append(pl.BlockSpec((tn, d), lambda i, j: (jnp.maximum(j - n_a, 0), 0),
                                    pipeline_mode=pl.Buffered(1)))
        weights.append(wt_b)
    return pl.pallas_call(
        functools.partial(_norm_mm_kernel, tm=tm, n_a=n_a), grid=(s // tm, n // tn),
        in_specs=[pl.BlockSpec((tm, d), lambda i, j: (i, 0)),
                  pl.BlockSpec((1, d), lambda i, j: (0, 0)),
                  *w_specs,
                  pl.BlockSpec((ns, d), lambda i, j: (0, 0))],
        out_specs=[pl.BlockSpec((tm, tn), lambda i, j: (i, j)),
                   pl.BlockSpec((tm, ns), lambda i, j: (i, 0))],
        out_shape=[jax.ShapeDtypeStruct((s, n), F32), jax.ShapeDtypeStruct((s, ns), F32)],
        scratch_shapes=[pltpu.VMEM((tm, d), BF16)],
        compiler_params=_params(("parallel", "arbitrary"), 52), name="norm_matmul",
    )(x, nw, *weights, wt_side)


def _gla_chunks(ps, st_ref):
    for p in ps:
        p["qe"] = p["q"] * jnp.exp(p["b"] - p["b_mid"])
        p["ke"] = p["k"] * jnp.exp(p["b_mid"] - p["b"])
        p["scores"] = jnp.where(p["mask"], _bdot_nt(p["qe"], p["ke"]), 0.0)
    for p in ps:
        q_dec = p["qe"] * jnp.exp(p["b_mid"])
        p["o"] = _bdot(p["scores"], p["v"]) + _bdot_nt(q_dec, st_ref[p["slot"]])
    for p in ps:
        k_dec = p["ke"] * jnp.exp(p["b_last"] - p["b_mid"])
        st_ref[p["slot"]] = st_ref[p["slot"]] * jnp.exp(p["b_last"]) + _bdot_tn(p["v"], k_dec)
        p["o_ref"][p["rows"], p["cols"]] = p["o"]


def _chunk_masks():
    row = lax.broadcasted_iota(jnp.int32, (CHUNK, CHUNK), 0)
    col = lax.broadcasted_iota(jnp.int32, (CHUNK, CHUNK), 1)
    return row, col


def _hgrn_kernel(logit_ref, qf_ref, ff_ref, vf_ref, qb_ref, fb_ref, vb_ref, of_ref, ob_ref, st_ref,
                 *, tb, layer):
    @pl.when(pl.program_id(0) == 0)
    def _():
        st_ref[...] = jnp.zeros_like(st_ref)

    lg = logit_ref[...]
    e = jnp.exp(lg - jnp.max(lg, axis=0, keepdims=True))
    lb = jnp.sum(e[0:layer + 1], axis=0, keepdims=True) / jnp.sum(e, axis=0, keepdims=True)
    row, col = _chunk_masks()
    causal, anti = row >= col, row <= col
    tri3_f, tri3_b = _tri3(False), _tri3(True)
    n = tb // CHUNK
    mid = CHUNK // 2
    scale = HG_DK ** -0.5

    def problem(q_ref, f_ref, v_ref, o_ref, rows, h, rev):
        cs = slice(h * HG_DK, (h + 1) * HG_DK)
        lbh = lb[:, cs]
        sig = jax.nn.sigmoid(f_ref[rows, cs])
        f = lbh + (1.0 - lbh) * sig
        b = _sdot(tri3_b if rev else tri3_f, _rhs_split3(jnp.log(f)))
        if rev:
            b_last, b_mid = b[0:1], b[CHUNK - 1 - mid:CHUNK - mid]
        else:
            b_last, b_mid = b[CHUNK - 1:CHUNK], b[mid:mid + 1]
        return dict(q=_silu(q_ref[rows, cs]) * scale, k=(1.0 - lbh) * (1.0 - sig), v=v_ref[rows, cs],
                    b=b, b_mid=b_mid, b_last=b_last, mask=anti if rev else causal,
                    slot=(HG_HEADS if rev else 0) + h, o_ref=o_ref, rows=rows, cols=cs)

    def body(c, carry):
        rf = pl.ds(pl.multiple_of(c * CHUNK, CHUNK), CHUNK)
        rb = pl.ds(pl.multiple_of((n - 1 - c) * CHUNK, CHUNK), CHUNK)
        ps = [problem(qf_ref, ff_ref, vf_ref, of_ref, rf, h, False) for h in range(HG_HEADS)]
        ps += [problem(qb_ref, fb_ref, vb_ref, ob_ref, rb, h, True) for h in range(HG_HEADS)]
        _gla_chunks(ps, st_ref)
        return carry

    lax.fori_loop(0, n, body, 0)


def _hgrn2(h_main, lb_logits, *, layer, tb=512):
    s = h_main.shape[0]
    tb = min(tb, s)
    nb = s // tb
    w = HG_HEADS * HG_DK
    fwd = lambda cb: pl.BlockSpec((tb, w), lambda i: (i, cb))
    bwd = lambda cb: pl.BlockSpec((tb, w), lambda i: (nb - 1 - i, cb))
    return pl.pallas_call(
        functools.partial(_hgrn_kernel, tb=tb, layer=layer), grid=(nb,),
        in_specs=[pl.BlockSpec(lb_logits.shape, lambda i: (0, 0)),
                  fwd(0), fwd(1), fwd(3), bwd(0), bwd(2), bwd(3)],
        out_specs=[fwd(0), bwd(0)],
        out_shape=[jax.ShapeDtypeStruct((s, w), F32)] * 2,
        scratch_shapes=[pltpu.VMEM((2 * HG_HEADS, HG_DV, HG_DK), F32)],
        compiler_params=_params(("arbitrary",), 40), name="hgrn2_scan",
    )(lb_logits, h_main, h_main, h_main, h_main, h_main, h_main)


def _ret_kernel(qf_ref, kf_ref, vf_ref, cf_ref, sf_ref, qb_ref, kb_ref, vb_ref, cb_ref, sb_ref,
                of_ref, ob_ref, st_ref, *, tb):
    @pl.when(pl.program_id(0) == 0)
    def _():
        st_ref[...] = jnp.zeros_like(st_ref)

    row, col = _chunk_masks()
    n = tb // CHUNK
    scale = RET_DK ** -0.5
    t_idx = lax.broadcasted_iota(jnp.int32, (CHUNK, RET_DK), 0).astype(F32)
    log_gamma = [math.log1p(-2.0 ** (-5 - h)) for h in range(RET_HEADS)]

    def rope(x, cos, sin):
        return x * cos + pltpu.roll(x, RET_DK // 2, 1) * sin

    def decay_tables(h, rev):
        lgam = log_gamma[RET_HEADS - 1 - h] if rev else log_gamma[h]
        steps = (CHUNK - t_idx) if rev else (t_idx + 1.0)
        dist = ((col - row) if rev else (row - col)).astype(F32)
        return dict(intra=jnp.where(dist >= 0.0, jnp.exp(dist * lgam), 0.0),
                    q_fac=jnp.exp(steps * lgam), k_fac=jnp.exp((CHUNK - steps) * lgam),
                    st_fac=math.exp(CHUNK * lgam))

    tables = {(h, rev): decay_tables(h, rev) for h in range(RET_HEADS) for rev in (False, True)}

    def problem(q_ref, k_ref, v_ref, c_ref, s_ref, o_ref, rows, h, rev):
        cs = slice(h * RET_DK, (h + 1) * RET_DK)
        vs = slice(h * RET_DV, (h + 1) * RET_DV)
        cos, sin = c_ref[rows, :], s_ref[rows, :]
        return dict(q=rope(q_ref[rows, cs], cos, sin) * scale, k=rope(k_ref[rows, cs], cos, sin),
                    v=v_ref[rows, vs], slot=(RET_HEADS if rev else 0) + h, o_ref=o_ref, rows=rows, cols=vs,
                    **tables[(h, rev)])

    def body(c, carry):
        groups = []
        for u in range(RET_UNROLL):
            rf = pl.ds(pl.multiple_of((RET_UNROLL * c + u) * CHUNK, CHUNK), CHUNK)
            rb = pl.ds(pl.multiple_of((n - 1 - RET_UNROLL * c - u) * CHUNK, CHUNK), CHUNK)
            groups.append(
                [problem(qf_ref, kf_ref, vf_ref, cf_ref, sf_ref, of_ref, rf, h, False) for h in range(RET_HEADS)]
                + [problem(qb_ref, kb_ref, vb_ref, cb_ref, sb_ref, ob_ref, rb, h, True) for h in range(RET_HEADS)])
        for ps in groups:
            for p in ps:
                p["scores"] = _bdot_nt(p["q"], p["k"]) * p["intra"]
                p["update"] = _bdot_tn(p["v"], p["k"] * p["k_fac"])
        for ps in groups:
            for p in ps:
                p["o"] = _bdot(p["scores"], p["v"]) + _bdot_nt(p["q"] * p["q_fac"], st_ref[p["slot"]])
            for p in ps:
                st_ref[p["slot"]] = st_ref[p["slot"]] * p["st_fac"] + p["update"]
                p["o_ref"][p["rows"], p["cols"]] = p["o"]
        return carry

    assert n % RET_UNROLL == 0
    lax.fori_loop(0, n // RET_UNROLL, body, 0)


def _retention(h_main, cos_tab, sin_tab, *, tb=512):
    s = h_main.shape[0]
    tb = min(tb, s)
    nb = s // tb
    wk, wv = RET_HEADS * RET_DK, RET_HEADS * RET_DV
    fwd = lambda w, cb: pl.BlockSpec((tb, w), lambda i: (i, cb))
    bwd = lambda w, cb: pl.BlockSpec((tb, w), lambda i: (nb - 1 - i, cb))
    return pl.pallas_call(
        functools.partial(_ret_kernel, tb=tb), grid=(nb,),
        in_specs=[fwd(wk, 0), fwd(wk, 1), fwd(wv, 1), fwd(LANES, 0), fwd(LANES, 0),
                  bwd(wk, 0), bwd(wk, 1), bwd(wv, 1), bwd(LANES, 0), bwd(LANES, 0)],
        out_specs=[fwd(wv, 0), bwd(wv, 0)],
        out_shape=[jax.ShapeDtypeStruct((s, wv), F32)] * 2,
        scratch_shapes=[pltpu.VMEM((2 * RET_HEADS, RET_DV, RET_DK), F32)],
        compiler_params=_params(("arbitrary",), 40), name="retention_scan",
    )(h_main, h_main, h_main, cos_tab, sin_tab, h_main, h_main, h_main, cos_tab, sin_tab)


def _gated_head_norm(of_ref, ob_ref, gate_ref, nw_ref, rows, heads, hd):
    o = of_ref[rows, :] + ob_ref[rows, :]
    parts = []
    for h in range(heads):
        y = o[:, h * hd:(h + 1) * hd]
        parts.append(y * lax.rsqrt(jnp.mean(y * y, axis=-1, keepdims=True) + EPS))
    y = jnp.concatenate(parts, axis=-1)
    return y * nw_ref[...] * _silu(gate_ref[rows, :])


def _mixout_kernel(*refs, tm, groups):
    x_ref, w_ref = refs[0], refs[1]
    out_ref, lhs_ref = refs[-2], refs[-1]
    grefs = refs[2:-2]

    def body(r, carry):
        rows = pl.ds(pl.multiple_of(r * NORM_ROWS, NORM_ROWS), NORM_ROWS)
        pos, col = 0, 0
        for g in groups:
            if g is None:
                val, width = grefs[pos][rows, :], grefs[pos].shape[1]
                pos += 1
            else:
                heads, hd = g
                val, width = _gated_head_norm(*grefs[pos:pos + 4], rows, heads, hd), heads * hd
                pos += 4
            lhs_ref[rows, col:col + width] = val.astype(BF16)
            col += width
        return carry
    lax.fori_loop(0, tm // NORM_ROWS, body, 0)

    out_ref[...] = x_ref[...] + jnp.dot(lhs_ref[...], w_ref[...], preferred_element_type=F32)


def _mix_out(x, w_out, group_args, groups, *, tm=256):
    s, d = x.shape
    tm = min(tm, s)
    specs = [pl.BlockSpec((tm, d), lambda i: (i, 0)),
             pl.BlockSpec(w_out.shape, lambda i: (0, 0), pipeline_mode=pl.Buffered(1))]
    arrays = [x, w_out]
    for arr, cb, width, rowvec in group_args:
        arrays.append(arr)
        if rowvec:
            specs.append(pl.BlockSpec((1, width), lambda i: (0, 0)))
        else:
            specs.append(pl.BlockSpec((tm, width), lambda i, cb=cb: (i, cb)))
    return pl.pallas_call(
        functools.partial(_mixout_kernel, tm=tm, groups=groups), grid=(s // tm,),
        in_specs=specs,
        out_specs=pl.BlockSpec((tm, d), lambda i: (i, 0)),
        out_shape=jax.ShapeDtypeStruct((s, d), F32),
        scratch_shapes=[pltpu.VMEM((tm, w_out.shape[0]), BF16)],
        compiler_params=_params(("parallel",), 40), name="mix_out",
    )(*arrays)


def _rot_pair(pr, cos, sin):
    return pr * cos + pltpu.roll(pr, MLA_ROPE, 1) * sin


def _mla_proj_kernel(cq_ref, ckv_ref, kpe_ref, qnw_ref, kvnw_ref, wq_ref, wkv_ref, cos_ref, sin_ref,
                     q_ref, k_ref, v_ref, *, scale):
    cos, sin = cos_ref[...], sin_ref[...]
    cqn = _rms_rows(cq_ref[...], qnw_ref[...]).astype(BF16)
    ckvn = _rms_rows(ckv_ref[...], kvnw_ref[...]).astype(BF16)
    pe = _rot_pair(kpe_ref[...], cos, sin).astype(BF16)
    lane = lax.broadcasted_iota(jnp.int32, pe.shape, 1)
    ones_col = jnp.where(lane == 0, 1.0, 0.0).astype(BF16)
    for h in range(MLA_HEADS):
        cols = slice(h * MLA_QK_PAD, (h + 1) * MLA_QK_PAD)
        yq = jnp.dot(cqn, wq_ref[:, cols], preferred_element_type=F32)
        q_rope = _rot_pair(yq[:, MLA_NOPE:], cos, sin)
        q_ref[h] = (jnp.concatenate([yq[:, :MLA_NOPE], q_rope], axis=1) * scale).astype(BF16)
        ykv = jnp.dot(ckvn, wkv_ref[:, cols], preferred_element_type=F32)
        k_ref[h] = jnp.concatenate([ykv[:, :MLA_NOPE].astype(BF16), pe], axis=1)
        v_ref[h] = jnp.concatenate([ykv[:, MLA_NOPE:].astype(BF16), ones_col], axis=1)


def _mla_proj(h_main, h_side, q_norm_w, kv_norm_w, wq, wkv, cos_tab, sin_tab, *, tm=512):
    s = h_main.shape[0]
    tm = min(tm, s)
    cq_block = (HG_HEADS * (3 * HG_DK + 2 * HG_DV)) // MLA_Q_RANK
    scale = (MLA_NOPE + MLA_ROPE) ** -0.5 * math.log2(math.e)
    full = lambda a: pl.BlockSpec(a.shape, lambda i: (0, 0))
    tab = pl.BlockSpec((tm, LANES), lambda i: (i, 0))
    head_out = pl.BlockSpec((MLA_HEADS, tm, MLA_QK_PAD), lambda i: (0, i, 0))
    return pl.pallas_call(
        functools.partial(_mla_proj_kernel, scale=scale), grid=(s // tm,),
        in_specs=[pl.BlockSpec((tm, MLA_Q_RANK), lambda i: (i, cq_block)),
                  pl.BlockSpec((tm, MLA_KV_RANK), lambda i: (i, 0)),
                  pl.BlockSpec((tm, LANES), lambda i: (i, MLA_KV_RANK // LANES)),
                  full(q_norm_w), full(kv_norm_w), full(wq), full(wkv), tab, tab],
        out_specs=[head_out, head_out, head_out],
        out_shape=[jax.ShapeDtypeStruct((MLA_HEADS, s, MLA_QK_PAD), BF16)] * 3,
        compiler_params=_params(("parallel",), 40), name="mla_proj",
    )(h_main, h_side, h_side, q_norm_w, kv_norm_w, wq, wkv, cos_tab, sin_tab)


ATTN_SPLIT = 2
ATTN_ROWS = 32


def _attn_kernel(q_ref, k_ref, v_ref, *refs, tk, nk, n_cast):
    cast_in, o_ref, cast_out = refs[:n_cast], refs[n_cast], refs[n_cast + 1:2 * n_cast + 1]
    s_ref, p_ref, m_ref, alpha_ref, acc_ref = refs[2 * n_cast + 1:]
    tq = q_ref.shape[1]
    sub = tq // ATTN_SPLIT
    m_ref[...] = jnp.full(m_ref.shape, -jnp.inf, F32)
    acc_ref[...] = jnp.zeros(acc_ref.shape, F32)

    def softmax_rows(i):
        for rb in range(sub // ATTN_ROWS):
            loc = slice(rb * ATTN_ROWS, (rb + 1) * ATTN_ROWS)
            glob = slice(i * sub + rb * ATTN_ROWS, i * sub + (rb + 1) * ATTN_ROWS)
            sc = s_ref[i, loc, :]
            cols = [sc[:, c * LANES:(c + 1) * LANES] for c in range(tk // LANES)]
            col_max = functools.reduce(jnp.maximum, cols)
            m_old = m_ref[glob, :]
            m_new = jnp.maximum(m_old, jnp.max(col_max, axis=-1, keepdims=True))
            m_ref[glob, :] = m_new
            alpha_ref[glob, :] = jnp.exp2(m_old - m_new)
            p_ref[i, loc, :] = jnp.concatenate([jnp.exp2(c - m_new) for c in cols], axis=1).astype(BF16)

    def body(j, carry):
        rows = pl.ds(pl.multiple_of(j * tk, tk), tk)
        k, v = k_ref[0, rows, :], v_ref[0, rows, :]
        for i in range(ATTN_SPLIT):
            s_ref[i] = lax.dot_general(q_ref[0, i * sub:(i + 1) * sub, :], k, NT_DIMS, preferred_element_type=F32)
        for i in range(ATTN_SPLIT):
            softmax_rows(i)
            rs = slice(i * sub, (i + 1) * sub)
            alpha = alpha_ref[rs, :]
            acc_ref[rs, :] = (jnp.concatenate([alpha, alpha], axis=1) * acc_ref[rs, :]
                              + jnp.dot(p_ref[i], v, preferred_element_type=F32))
        for src, dst in zip(cast_in, cast_out):
            if src.shape[1] % (2 * SUBLANES * nk) == 0:
                piece = src.shape[1] // nk
                part = pl.ds(pl.multiple_of(j * piece, piece), piece)
                dst[0, part, :] = src[0, part, :].astype(BF16)
        return carry

    for src, dst in zip(cast_in, cast_out):
        if src.shape[1] % (2 * SUBLANES * nk) != 0:
            dst[...] = src[...].astype(BF16)

    lax.fori_loop(0, nk, body, 0)
    acc = acc_ref[...]
    o_ref[...] = (acc[:, :MLA_V] / acc[:, MLA_V:MLA_V + 1]).astype(BF16)


def _attention(q, k, v, to_cast, *, tq=1024, tk=2048):
    heads, s, _ = q.shape
    tq, tk = min(tq, s), min(tk, s)
    sub = tq // ATTN_SPLIT
    nq, nk = s // tq, s // tk
    steps = heads * nq
    cast_specs = []
    for a in to_cast:
        layers, rows, cols = a.shape
        block = next(b for b in range(-(-layers * rows // steps), rows + 1)
                     if rows % b == 0 and b % (2 * SUBLANES) == 0)
        per_layer, last = rows // block, layers * rows // block - 1

        def index(h, i, per_layer=per_layer, last=last):
            t = jnp.minimum(h * nq + i, last)
            return t // per_layer, t % per_layer, 0

        cast_specs.append(pl.BlockSpec((1, block, cols), index))
    outs = pl.pallas_call(
        functools.partial(_attn_kernel, tk=tk, nk=nk, n_cast=len(to_cast)), grid=(heads, nq),
        in_specs=[pl.BlockSpec((1, tq, MLA_QK_PAD), lambda h, i: (h, i, 0)),
                  pl.BlockSpec((1, s, MLA_QK_PAD), lambda h, i: (h, 0, 0)),
                  pl.BlockSpec((1, s, 2 * MLA_V), lambda h, i: (h, 0, 0))] + cast_specs,
        out_specs=[pl.BlockSpec((tq, MLA_V), lambda h, i: (i, h))] + cast_specs,
        out_shape=[jax.ShapeDtypeStruct((s, heads * MLA_V), BF16)]
                  + [jax.ShapeDtypeStruct(a.shape, BF16) for a in to_cast],
        scratch_shapes=[pltpu.VMEM((ATTN_SPLIT, sub, tk), F32), pltpu.VMEM((ATTN_SPLIT, sub, tk), BF16),
                        pltpu.VMEM((tq, LANES), F32), pltpu.VMEM((tq, LANES), F32),
                        pltpu.VMEM((tq, 2 * MLA_V), F32)],
        compiler_params=_params(("arbitrary", "arbitrary"), 56), name="mla_attention",
    )(q, k, v, *to_cast)
    return outs[0], outs[1:]


def _ffn_kernel(x_ref, nw_ref, wu_ref, wd_ref, fw_ref, o_ref, xn_ref, *, tm, nf, final_norm):
    f = pl.program_id(1)

    @pl.when(f == 0)
    def _():
        def init(r, carry):
            rows = pl.ds(pl.multiple_of(r * NORM_ROWS, NORM_ROWS), NORM_ROWS)
            x = x_ref[rows, :]
            o_ref[rows, :] = x
            xn_ref[rows, :] = _rms_rows(x, nw_ref[...]).astype(BF16)
            return carry
        lax.fori_loop(0, tm // NORM_ROWS, init, 0)

    a = jnp.maximum(jnp.dot(xn_ref[...], wu_ref[0], preferred_element_type=F32), 0.0)
    o_ref[...] += jnp.dot((a * a).astype(BF16), wd_ref[0], preferred_element_type=F32)

    if final_norm:
        @pl.when(f == nf - 1)
        def _():
            def body(r, carry):
                rows = pl.ds(pl.multiple_of(r * NORM_ROWS, NORM_ROWS), NORM_ROWS)
                o_ref[rows, :] = _rms_rows(o_ref[rows, :], fw_ref[...])
                return carry
            lax.fori_loop(0, tm // NORM_ROWS, body, 0)


def _ffn(x, nw, w_up, w_down, final_w, *, layer, final_norm, tm=512, tf=1024):
    s, d = x.shape
    dff = w_up.shape[2]
    tm = min(tm, s)
    nf = dff // tf
    return pl.pallas_call(
        functools.partial(_ffn_kernel, tm=tm, nf=nf, final_norm=final_norm), grid=(s // tm, nf),
        in_specs=[pl.BlockSpec((tm, d), lambda i, f: (i, 0)),
                  pl.BlockSpec((1, d), lambda i, f: (0, 0)),
                  pl.BlockSpec((1, d, tf), lambda i, f: (layer, 0, f)),
                  pl.BlockSpec((1, tf, d), lambda i, f: (layer, f, 0)),
                  pl.BlockSpec((1, d), lambda i, f: (0, 0))],
        out_specs=pl.BlockSpec((tm, d), lambda i, f: (i, 0)),
        out_shape=jax.ShapeDtypeStruct((s, d), F32),
        scratch_shapes=[pltpu.VMEM((tm, d), BF16)],
        compiler_params=_params(("parallel", "arbitrary"), 48), name="ffn",
    )(x, nw, w_up, w_down, final_w)


def _gdn_prep_kernel(x_ref, xp_ref, xn_ref, cw_ref, o_ref, *, tm, nblk):
    i, sec = pl.program_id(0), pl.program_id(1)
    x = x_ref[...]
    prev = jnp.where(i > 0, xp_ref[...], 0.0)
    nxt = jnp.where(i < nblk - 1, xn_ref[...], 0.0)
    rows = lax.broadcasted_iota(jnp.int32, prev.shape, 0)
    half = CONV_WIDTH // 2

    def shifted(d):
        if d == 0:
            return x
        r = pltpu.roll(x, (-d) % tm, 0)
        if d < 0:
            edge = r[:SUBLANES]
            for t in range(-d):
                edge = jnp.where(rows == t, prev[SUBLANES + t + d:SUBLANES + t + d + 1], edge)
            return jnp.concatenate([edge, r[SUBLANES:]], axis=0)
        edge = r[tm - SUBLANES:]
        for t in range(SUBLANES - d, SUBLANES):
            edge = jnp.where(rows == t, nxt[t + d - SUBLANES:t + d - SUBLANES + 1], edge)
        return jnp.concatenate([r[:tm - SUBLANES], edge], axis=0)

    y = shifted(-half) * cw_ref[0:1]
    for j in range(1, CONV_WIDTH):
        y = y + shifted(j - half) * cw_ref[j:j + 1]
    y = _silu(y)
    parts = []
    for h in range(GDN_HEADS):
        a = y[:, h * GDN_DK:(h + 1) * GDN_DK]
        parts.append(a * lax.rsqrt(jnp.sum(a * a, axis=-1, keepdims=True) + EPS))
    nrm = jnp.concatenate(parts, axis=-1) * jnp.where(sec == 0, GDN_DK ** -0.5, 1.0)
    o_ref[...] = jnp.where(sec == 2, y, nrm)


def _gdn_prep(h_main, conv_w, *, tm=256):
    s = h_main.shape[0]
    tm = min(tm, s)
    nblk = s // tm
    w = GDN_HEADS * GDN_DK
    base = (2 * RET_HEADS * RET_DK + 2 * RET_HEADS * RET_DV) // w
    per = tm // SUBLANES
    return pl.pallas_call(
        functools.partial(_gdn_prep_kernel, tm=tm, nblk=nblk), grid=(nblk, 3),
        in_specs=[pl.BlockSpec((tm, w), lambda i, c: (i, base + c)),
                  pl.BlockSpec((SUBLANES, w), lambda i, c: (jnp.maximum(i * per - 1, 0), base + c)),
                  pl.BlockSpec((SUBLANES, w), lambda i, c: (jnp.minimum((i + 1) * per, s // SUBLANES - 1), base + c)),
                  pl.BlockSpec((CONV_WIDTH, w), lambda i, c: (0, c))],
        out_specs=pl.BlockSpec((tm, w), lambda i, c: (i, c)),
        out_shape=jax.ShapeDtypeStruct((s, 3 * w), F32),
        compiler_params=_params(("parallel", "arbitrary"), 32), name="gdn_prep",
    )(h_main, h_main, h_main, conv_w)


def _softplus(x):
    return jnp.maximum(x, 0.0) + jnp.log1p(jnp.exp(-jnp.abs(x)))


def _pair_lhs(x):
    hi = x.astype(BF16)
    lo = (x - hi.astype(F32)).astype(BF16)
    return jnp.concatenate([hi, lo, hi], axis=1)


def _pair_rhs(y, low_lanes):
    hi = y.astype(BF16)
    lo = (y - hi.astype(F32)).astype(BF16)
    zero = jnp.zeros_like(hi)

    def bd(a):
        return jnp.concatenate([jnp.where(low_lanes, a, zero), jnp.where(low_lanes, zero, a)], axis=0)

    bd_hi = bd(hi)
    return jnp.concatenate([bd_hi, bd_hi, bd(lo)], axis=0)


def _pair_lhs2(x):
    hi = x.astype(BF16)
    lo = (x - hi.astype(F32)).astype(BF16)
    return jnp.concatenate([hi, lo], axis=1)


def _pair_rhs1(y, low_lanes):
    hi = y.astype(BF16)
    zero = jnp.zeros_like(hi)
    bd = jnp.concatenate([jnp.where(low_lanes, hi, zero), jnp.where(low_lanes, zero, hi)], axis=0)
    return jnp.concatenate([bd, bd], axis=0)


def _pair_rows(x, low_lanes):
    return jnp.concatenate([jnp.where(low_lanes, x, 0.0), jnp.where(low_lanes, 0.0, x)], axis=0)


def _gdn_kernel(alog_ref, dt_ref, qf_ref, kf_ref, vf_ref, gf_ref, qb_ref, kb_ref, vb_ref, gb_ref,
                of_ref, ob_ref, st_ref, *, tb):
    @pl.when(pl.program_id(0) == 0)
    def _():
        st_ref[...] = jnp.zeros_like(st_ref)

    row = lax.broadcasted_iota(jnp.int32, (CHUNK, LANES), 0)
    lane = lax.broadcasted_iota(jnp.int32, (CHUNK, LANES), 1)
    col = lane & (CHUNK - 1)
    low_lanes = lane < CHUNK
    low_row = low_lanes[0:1]
    causal, anti = row >= col, row <= col
    strict_c, strict_a = row > col, row < col
    tri3_f, tri3_b = _tri3(False), _tri3(True)
    eye = (row == col).astype(F32)
    same16 = (row >> 4) == (col >> 4)
    same32 = (row >> 5) == (col >> 5)
    in32 = jnp.logical_and(same32, jnp.logical_not(same16))
    n = tb // CHUNK
    neg_a = -jnp.exp(alog_ref[...])
    dt = dt_ref[...]

    def pairs(q_ref, k_ref, v_ref, g_ref, o_ref, rows, rev):
        raw = g_ref[rows, :]
        la = neg_a * _softplus(raw + dt)
        beta = jax.nn.sigmoid(raw)
        g = _sdot(tri3_b if rev else tri3_f, _rhs_split3(la))
        g_rows = jnp.concatenate([g, g], axis=0).T
        la0 = GDN_HEADS if rev else 0
        b0 = (3 if rev else 2) * GDN_HEADS
        last = 0 if rev else CHUNK - 1
        out = []
        for j in range(GDN_HEADS // 2):
            hs = (2 * j, 2 * j + 1)
            g_cols = [jnp.broadcast_to(g[:, la0 + h:la0 + h + 1], (CHUNK, LANES)) for h in hs]
            b_cols = [jnp.broadcast_to(beta[:, b0 + h:b0 + h + 1], (CHUNK, LANES)) for h in hs]
            g_col = jnp.where(low_lanes, g_cols[0], g_cols[1])
            g_row = jnp.where(low_row, g_rows[la0 + hs[0]:la0 + hs[0] + 1], g_rows[la0 + hs[1]:la0 + hs[1] + 1])
            out.append(dict(
                q_ref=q_ref, k_ref=k_ref, v_ref=v_ref, o_ref=o_ref, rows=rows,
                cs=[slice(h * GDN_DK, (h + 1) * GDN_DK) for h in hs],
                slots=[(GDN_HEADS if rev else 0) + h for h in hs],
                strict=strict_a if rev else strict_c,
                decay=jnp.exp(jnp.where(anti if rev else causal, g_col - g_row, -jnp.inf)),
                g_cols=g_cols, b_cols=b_cols,
                g_last=[g[last:last + 1, la0 + h:la0 + h + 1] for h in hs]))
        return out

    def body(c, carry):
        rf = pl.ds(pl.multiple_of(c * CHUNK, CHUNK), CHUNK)
        rb = pl.ds(pl.multiple_of((n - 1 - c) * CHUNK, CHUNK), CHUNK)
        ps = (pairs(qf_ref, kf_ref, vf_ref, gf_ref, of_ref, rf, False)
              + pairs(qb_ref, kb_ref, vb_ref, gb_ref, ob_ref, rb, True))
        for p in ps:
            ks = [p["k_ref"][p["rows"], cs] for cs in p["cs"]]
            qs = [p["q_ref"][p["rows"], cs] for cs in p["cs"]]
            p["kb"] = [k * b for k, b in zip(ks, p["b_cols"])]
            lhs = jnp.concatenate([jnp.concatenate([kb, q], axis=0) for kb, q in zip(p["kb"], qs)], axis=1)
            zero = jnp.zeros_like(ks[0])
            k_bd = jnp.concatenate([jnp.concatenate([ks[0], zero], axis=1),
                                    jnp.concatenate([zero, ks[1]], axis=1)], axis=0)
            kq = _bdot_nt(lhs, k_bd)
            a = jnp.where(p["strict"], kq[:CHUNK] * p["decay"], 0.0)
            p["qk"] = kq[CHUNK:] * p["decay"]
            n1 = -jnp.where(same16, a, 0.0)
            p["n_l"], p["n_r"], p["t"] = _pair_lhs2(n1), _pair_rhs1(n1, low_lanes), eye + n1
            p["e32"] = _pair_rhs1(jnp.where(in32, a, 0.0), low_lanes)
            p["e64"] = _pair_rhs1(jnp.where(same32, 0.0, a), low_lanes)
        for step in range(3):
            for p in ps:
                nn = _sdot(p["n_l"], p["n_r"])
                p["n_r"] = _pair_rhs(nn, low_lanes)
                if step < 2:
                    p["n_l"] = _pair_lhs(nn)
            for p in ps:
                p["t"] = p["t"] + _sdot(_pair_lhs(p["t"]), p["n_r"])
        for e in ("e32", "e64"):
            for p in ps:
                p["x"] = _sdot(_pair_lhs2(p["t"]), p[e])
            for p in ps:
                p["t"] = p["t"] - _sdot(_pair_lhs(p["x"]), _pair_rhs(p["t"], low_lanes))
        for p in ps:
            rhs = []
            for cs, kb, b, g in zip(p["cs"], p["kb"], p["b_cols"], p["g_cols"]):
                rhs.append(jnp.concatenate([p["v_ref"][p["rows"], cs] * b, kb * jnp.exp(g)], axis=1))
            p["sol"] = _sdot(_pair_lhs(_pair_rows(p["t"], low_lanes)), _rhs_split(jnp.concatenate(rhs, axis=0)))
        for p in ps:
            p["v_new"], p["o"] = [], []
            for i, (cs, slot, g) in enumerate(zip(p["cs"], p["slots"], p["g_cols"])):
                sol = p["sol"][i * CHUNK:(i + 1) * CHUNK]
                q = p["q_ref"][p["rows"], cs]
                ws = _bdot(jnp.concatenate([sol[:, GDN_DV:], q * jnp.exp(g)], axis=0), st_ref[slot])
                p["v_new"].append(sol[:, :GDN_DV] - ws[:CHUNK])
                p["o"].append(ws[CHUNK:])
        for p in ps:
            o_intra = _bdot(_pair_rows(p["qk"], low_lanes), jnp.concatenate(p["v_new"], axis=0))
            for i, (cs, slot, g, g_last) in enumerate(zip(p["cs"], p["slots"], p["g_cols"], p["g_last"])):
                k = p["k_ref"][p["rows"], cs]
                p["o_ref"][p["rows"], cs] = p["o"][i] + o_intra[i * CHUNK:(i + 1) * CHUNK]
                st_ref[slot] = (st_ref[slot] * jnp.exp(g_last)
                                + _bdot_tn(k * jnp.exp(g_last - g), p["v_new"][i]))
        return carry

    lax.fori_loop(0, n, body, 0)


def _gdn(qkv, gates, alog_row, dt_row, *, tb=512):
    s = qkv.shape[0]
    tb = min(tb, s)
    nb = s // tb
    w = GDN_HEADS * GDN_DK
    fwd = lambda width, cb: pl.BlockSpec((tb, width), lambda i: (i, cb))
    bwd = lambda width, cb: pl.BlockSpec((tb, width), lambda i: (nb - 1 - i, cb))
    row = pl.BlockSpec((1, LANES), lambda i: (0, 0))
    return pl.pallas_call(
        functools.partial(_gdn_kernel, tb=tb), grid=(nb,),
        in_specs=[row, row,
                  fwd(w, 0), fwd(w, 1), fwd(w, 2), fwd(LANES, 0),
                  bwd(w, 0), bwd(w, 1), bwd(w, 2), bwd(LANES, 0)],
        out_specs=[fwd(w, 0), bwd(w, 0)],
        out_shape=[jax.ShapeDtypeStruct((s, w), F32)] * 2,
        scratch_shapes=[pltpu.VMEM((2 * GDN_HEADS, GDN_DK, GDN_DV), F32)],
        compiler_params=_params(("arbitrary",), 40), name="gdn_scan",
    )(alog_row, dt_row, qkv, qkv, qkv, gates, qkv, qkv, qkv, gates)


def _rot_cols(w_pe):
    half = w_pe.shape[-1] // 2
    return jnp.concatenate([-w_pe[..., half:], w_pe[..., :half]], axis=-1)


def _pad_lanes(v, fill=0.0):
    v = v.reshape(1, -1).astype(F32)
    return jnp.pad(v, ((0, 0), (0, LANES - v.shape[1])), constant_values=fill)


def kernel(x, positions, norm_mix_w, norm_ffn_w, final_norm_w, hg_lb_logits, even_w_in, hg_norm_w,
           mla_q_norm_w, mla_w_q_b, mla_kv_norm_w, mla_w_kv_b, even_w_out, odd_w_in, ret_norm_w,
           gdn_conv_w, gdn_a_log, gdn_dt_bias, gdn_norm_w, odd_w_out, ffn_w_up, ffn_w_down):
    b, s, d = x.shape
    assert b == 1 and s % CHUNK == 0
    xs = x.reshape(s, d)
    row = lambda v: v.reshape(1, -1).astype(F32)

    pos_col = positions.reshape(s, 1)
    inv_ret = ROPE_THETA ** (-jnp.arange(RET_DK // 2, dtype=F32) / (RET_DK // 2))
    inv_mla = ROPE_THETA ** (-jnp.arange(MLA_ROPE // 2, dtype=F32) / (MLA_ROPE // 2))
    inv_row = row(jnp.concatenate([inv_ret, inv_mla, jnp.zeros((LANES - RET_DK // 2 - MLA_ROPE // 2,), F32)]))
    cos_ret, sin_ret, cos_mla, sin_mla, wt_in_even = _rope_tables(pos_col, inv_row, jnp.swapaxes(even_w_in, 1, 2))

    n_main = HG_HEADS * (3 * HG_DK + 2 * HG_DV) + MLA_Q_RANK
    wt_in = wt_in_even[0]
    wt_kpe = wt_in[n_main + MLA_KV_RANK:]
    wt_side = jnp.concatenate([wt_in[n_main:], _rot_cols(wt_kpe.T).T], axis=0)
    h_main, h_side = _norm_matmul(xs, row(norm_mix_w[0]), wt_in, n_main, None, wt_side, tn=512)

    o_f, o_b = _hgrn2(h_main, hg_lb_logits.astype(F32), layer=0)

    wq = mla_w_q_b[0].reshape(MLA_Q_RANK, MLA_HEADS, MLA_NOPE + MLA_ROPE)
    wq_pe = wq[..., MLA_NOPE:]
    wq = jnp.concatenate([wq[..., :MLA_NOPE], wq_pe, _rot_cols(wq_pe)], axis=-1)
    wq = wq.reshape(MLA_Q_RANK, MLA_HEADS * MLA_QK_PAD).astype(BF16)
    q, k, v = _mla_proj(h_main, h_side, row(mla_q_norm_w[0]), row(mla_kv_norm_w[0]), wq,
                        mla_w_kv_b[0].astype(BF16), cos_mla, sin_mla)
    o_attn, (w_up, w_down, w_out_even, w_out_odd, wt_in_odd) = _attention(
        q, k, v, [ffn_w_up, ffn_w_down, even_w_out, odd_w_out, jnp.swapaxes(odd_w_in, 1, 2)])

    wa = HG_HEADS * HG_DV
    xs = _mix_out(xs, w_out_even[0],
                  [(o_f, 0, wa, False), (o_b, 0, wa, False), (h_main, 4, wa, False), (row(hg_norm_w[0]), 0, wa, True),
                   (o_attn, 0, MLA_HEADS * MLA_V, False)],
                  ((HG_HEADS, HG_DV), None))
    xs = _ffn(xs, row(norm_ffn_w[0]), w_up, w_down, row(final_norm_w), layer=0, final_norm=False)

    n_ret = 2 * RET_HEADS * RET_DK + 2 * RET_HEADS * RET_DV
    n_qkv = GDN_HEADS * (2 * GDN_DK + GDN_DV)
    n_gate = 4 * GDN_HEADS
    wt_in = wt_in_odd[0]
    wt_gate = wt_in[n_ret + n_qkv + n_gate:]
    wt_side = jnp.pad(wt_in[n_ret + n_qkv:n_ret + n_qkv + n_gate], ((0, LANES - n_gate), (0, 0)))
    h_main, h_gates = _norm_matmul(xs, row(norm_mix_w[1]), wt_in, n_ret + n_qkv, wt_gate, wt_side, tn=1024)

    r_f, r_b = _retention(h_main, cos_ret, sin_ret)
    qkv = _gdn_prep(h_main, gdn_conv_w[0].astype(F32))
    g_f, g_b = _gdn(qkv, h_gates, _pad_lanes(gdn_a_log[0]), _pad_lanes(gdn_dt_bias[0]))

    wr, wg = RET_HEADS * RET_DV, GDN_HEADS * GDN_DV
    xs = _mix_out(xs, w_out_odd[0],
                  [(r_f, 0, wr, False), (r_b, 0, wr, False), (h_main, 2, wr, False), (row(ret_norm_w[0]), 0, wr, True),
                   (g_f, 0, wg, False), (g_b, 0, wg, False), (h_main, 6, wg, False), (row(gdn_norm_w[0]), 0, wg, True)],
                  ((RET_HEADS, RET_DV), (GDN_HEADS, GDN_DV)))
    xs = _ffn(xs, row(norm_ffn_w[1]), w_up, w_down, row(final_norm_w), layer=1, final_norm=True)
    return xs.reshape(b, s, d)
```

```python
import functools
import math

import jax
import jax.numpy as jnp
from jax import lax
from jax.experimental import pallas as pl
from jax.experimental.pallas import tpu as pltpu

F32 = jnp.float32
BF16 = jnp.bfloat16

EPS = 1e-6
CHUNK = 64
ROPE_THETA = 10000.0
LANES = 128
SUBLANES = 8

HG_HEADS, HG_DK, HG_DV = 8, 128, 128
MLA_HEADS, MLA_NOPE, MLA_ROPE, MLA_V = 8, 128, 64, 128
MLA_Q_RANK, MLA_KV_RANK = 512, 256
MLA_QK_PAD = 256
RET_HEADS, RET_DK, RET_DV = 4, 128, 256
RET_UNROLL = 2
GDN_HEADS, GDN_DK, GDN_DV = 8, 128, 128
CONV_WIDTH = 5

NT_DIMS = (((1,), (1,)), ((), ()))
TN_DIMS = (((0,), (0,)), ((), ()))


def _params(semantics, vmem_mib):
    return pltpu.CompilerParams(dimension_semantics=semantics, vmem_limit_bytes=vmem_mib * 1024 * 1024)


def _rms_rows(x, w):
    ms = jnp.mean(x * x, axis=-1, keepdims=True)
    return x * lax.rsqrt(ms + EPS) * w


def _silu(x):
    return x * jax.nn.sigmoid(x)


def _bdot(a, b):
    return jnp.dot(a.astype(BF16), b.astype(BF16), preferred_element_type=F32)


def _bdot_nt(a, b):
    return lax.dot_general(a.astype(BF16), b.astype(BF16), NT_DIMS, preferred_element_type=F32)


def _bdot_tn(a, b):
    return lax.dot_general(a.astype(BF16), b.astype(BF16), TN_DIMS, preferred_element_type=F32)


def _rhs_split(b):
    hi = b.astype(BF16)
    lo = (b - hi.astype(F32)).astype(BF16)
    return jnp.concatenate([hi, hi, lo], axis=0)


def _rhs_split3(b):
    b1 = b.astype(BF16)
    r = b - b1.astype(F32)
    b2 = r.astype(BF16)
    b3 = (r - b2.astype(F32)).astype(BF16)
    return jnp.concatenate([b1, b2, b3], axis=0)


def _sdot(lhs_split, rhs_split):
    return jnp.dot(lhs_split, rhs_split, preferred_element_type=F32)


def _tri3(rev):
    row = lax.broadcasted_iota(jnp.int32, (CHUNK, 3 * CHUNK), 0)
    col = lax.broadcasted_iota(jnp.int32, (CHUNK, 3 * CHUNK), 1) & (CHUNK - 1)
    return jnp.where(row <= col if rev else row >= col, 1.0, 0.0).astype(BF16)


def _rope_table_kernel(pos_ref, inv_ref, cos_ret_ref, sin_ret_ref, cos_mla_ref, sin_mla_ref):
    ang = pos_ref[...].astype(F32) * inv_ref[...]
    cos, sin = jnp.cos(ang), jnp.sin(ang)
    lane = lax.broadcasted_iota(jnp.int32, cos.shape, 1)
    half, quarter = LANES // 2, LANES // 4
    cos_ret_ref[...] = jnp.where(lane < half, cos, pltpu.roll(cos, half, 1))
    sin_ret_ref[...] = jnp.where(lane < half, -sin, pltpu.roll(sin, half, 1))

    def mla(t):
        return jnp.where(lane < quarter, pltpu.roll(t, half, 1),
                         jnp.where(lane < half, pltpu.roll(t, half + quarter, 1), 0.0))

    cos_mla_ref[...] = mla(cos)
    sin_mla_ref[...] = mla(sin)


def _rope_tables(pos_col, inv_row):
    s = pos_col.shape[0]
    tm = min(512, s)
    tab = pl.BlockSpec((tm, LANES), lambda i: (i, 0))
    return pl.pallas_call(
        _rope_table_kernel, grid=(s // tm,),
        in_specs=[pl.BlockSpec((tm, 1), lambda i: (i, 0)), pl.BlockSpec((1, LANES), lambda i: (0, 0))],
        out_specs=[tab] * 4,
        out_shape=[jax.ShapeDtypeStruct((s, LANES), F32)] * 4,
        compiler_params=_params(("parallel",), 16), name="rope_tables",
    )(pos_col, inv_row)


NORM_ROWS = 128


def _norm_rows_to(x_ref, nw_ref, xn_ref, tm):
    def body(r, carry):
        rows = pl.ds(pl.multiple_of(r * NORM_ROWS, NORM_ROWS), NORM_ROWS)
        xn_ref[rows, :] = _rms_rows(x_ref[rows, :], nw_ref[...]).astype(BF16)
        return carry
    lax.fori_loop(0, tm // NORM_ROWS, body, 0)


def _norm_mm_kernel(*refs, tm, n_a):
    x_ref, nw_ref, wa_ref = refs[:3]
    wb_ref = refs[3] if len(refs) == 8 else None
    wst_ref, o_ref, os_ref, xn_ref = refs[-4:]
    j = pl.program_id(1)

    @pl.when(j == 0)
    def _():
        _norm_rows_to(x_ref, nw_ref, xn_ref, tm)
        os_ref[...] = lax.dot_general(xn_ref[...], wst_ref[...], NT_DIMS, preferred_element_type=F32)

    def project(wt_ref):
        o_ref[...] = lax.dot_general(xn_ref[...], wt_ref[...], NT_DIMS, preferred_element_type=F32)

    if wb_ref is None:
        project(wa_ref)
    else:
        pl.when(j < n_a)(lambda: project(wa_ref))
        pl.when(j >= n_a)(lambda: project(wb_ref))


def _norm_matmul(x, nw, wt_a, n_a_rows, wt_b, wt_side, *, tm=1024, tn=512):
    s, d = x.shape
    ns = wt_side.shape[0]
    tm = min(tm, s)
    n_a = n_a_rows // tn
    n = n_a_rows + (0 if wt_b is None else wt_b.shape[0])
    w_specs = [pl.BlockSpec((tn, d), lambda i, j: (jnp.minimum(j, n_a - 1), 0))]
    weights = [wt_a]
    if wt_b is not None:
        w_specs.append(pl.BlockSpec((tn, d), lambda i, j: (jnp.maximum(j - n_a, 0), 0),
                                    pipeline_mode=pl.Buffered(1)))
        weights.append(wt_b)
    return pl.pallas_call(
        functools.partial(_norm_mm_kernel, tm=tm, n_a=n_a), grid=(s // tm, n // tn),
        in_specs=[pl.BlockSpec((tm, d), lambda i, j: (i, 0)),
                  pl.BlockSpec((1, d), lambda i, j: (0, 0)),
                  *w_specs,
                  pl.BlockSpec((ns, d), lambda i, j: (0, 0))],
        out_specs=[pl.BlockSpec((tm, tn), lambda i, j: (i, j)),
                   pl.BlockSpec((tm, ns), lambda i, j: (i, 0))],
        out_shape=[jax.ShapeDtypeStruct((s, n), F32), jax.ShapeDtypeStruct((s, ns), F32)],
        scratch_shapes=[pltpu.VMEM((tm, d), BF16)],
        compiler_params=_params(("parallel", "arbitrary"), 52), name="norm_matmul",
    )(x, nw, *weights, wt_side)


def _gla_chunks(ps, st_ref):
    for p in ps:
        p["qe"] = p["q"] * jnp.exp(p["b"] - p["b_mid"])
        p["ke"] = p["k"] * jnp.exp(p["b_mid"] - p["b"])
        p["scores"] = jnp.where(p["mask"], _bdot_nt(p["qe"], p["ke"]), 0.0)
    for p in ps:
        q_dec = p["qe"] * jnp.exp(p["b_mid"])
        p["o"] = _bdot(p["scores"], p["v"]) + _bdot_nt(q_dec, st_ref[p["slot"]])
    for p in ps:
        k_dec = p["ke"] * jnp.exp(p["b_last"] - p["b_mid"])
        st_ref[p["slot"]] = st_ref[p["slot"]] * jnp.exp(p["b_last"]) + _bdot_tn(p["v"], k_dec)
        p["o_ref"][p["rows"], p["cols"]] = p["o"]


def _chunk_masks():
    row = lax.broadcasted_iota(jnp.int32, (CHUNK, CHUNK), 0)
    col = lax.broadcasted_iota(jnp.int32, (CHUNK, CHUNK), 1)
    return row, col


def _hgrn_kernel(logit_ref, qf_ref, ff_ref, vf_ref, qb_ref, fb_ref, vb_ref, of_ref, ob_ref, st_ref,
                 *, tb, layer):
    @pl.when(pl.program_id(0) == 0)
    def _():
        st_ref[...] = jnp.zeros_like(st_ref)

    lg = logit_ref[...]
    e = jnp.exp(lg - jnp.max(lg, axis=0, keepdims=True))
    lb = jnp.sum(e[0:layer + 1], axis=0, keepdims=True) / jnp.sum(e, axis=0, keepdims=True)
    row, col = _chunk_masks()
    causal, anti = row >= col, row <= col
    tri3_f, tri3_b = _tri3(False), _tri3(True)
    n = tb // CHUNK
    mid = CHUNK // 2
    scale = HG_DK ** -0.5

    def problem(q_ref, f_ref, v_ref, o_ref, rows, h, rev):
        cs = slice(h * HG_DK, (h + 1) * HG_DK)
        lbh = lb[:, cs]
        sig = jax.nn.sigmoid(f_ref[rows, cs])
        f = lbh + (1.0 - lbh) * sig
        b = _sdot(tri3_b if rev else tri3_f, _rhs_split3(jnp.log(f)))
        if rev:
            b_last, b_mid = b[0:1], b[CHUNK - 1 - mid:CHUNK - mid]
        else:
            b_last, b_mid = b[CHUNK - 1:CHUNK], b[mid:mid + 1]
        return dict(q=_silu(q_ref[rows, cs]) * scale, k=(1.0 - lbh) * (1.0 - sig), v=v_ref[rows, cs],
                    b=b, b_mid=b_mid, b_last=b_last, mask=anti if rev else causal,
                    slot=(HG_HEADS if rev else 0) + h, o_ref=o_ref, rows=rows, cols=cs)

    def body(c, carry):
        rf = pl.ds(pl.multiple_of(c * CHUNK, CHUNK), CHUNK)
        rb = pl.ds(pl.multiple_of((n - 1 - c) * CHUNK, CHUNK), CHUNK)
        ps = [problem(qf_ref, ff_ref, vf_ref, of_ref, rf, h, False) for h in range(HG_HEADS)]
        ps += [problem(qb_ref, fb_ref, vb_ref, ob_ref, rb, h, True) for h in range(HG_HEADS)]
        _gla_chunks(ps, st_ref)
        return carry

    lax.fori_loop(0, n, body, 0)


def _hgrn2(h_main, lb_logits, *, layer, tb=512):
    s = h_main.shape[0]
    tb = min(tb, s)
    nb = s // tb
    w = HG_HEADS * HG_DK
    fwd = lambda cb: pl.BlockSpec((tb, w), lambda i: (i, cb))
    bwd = lambda cb: pl.BlockSpec((tb, w), lambda i: (nb - 1 - i, cb))
    return pl.pallas_call(
        functools.partial(_hgrn_kernel, tb=tb, layer=layer), grid=(nb,),
        in_specs=[pl.BlockSpec(lb_logits.shape, lambda i: (0, 0)),
                  fwd(0), fwd(1), fwd(3), bwd(0), bwd(2), bwd(3)],
        out_specs=[fwd(0), bwd(0)],
        out_shape=[jax.ShapeDtypeStruct((s, w), F32)] * 2,
        scratch_shapes=[pltpu.VMEM((2 * HG_HEADS, HG_DV, HG_DK), F32)],
        compiler_params=_params(("arbitrary",), 40), name="hgrn2_scan",
    )(lb_logits, h_main, h_main, h_main, h_main, h_main, h_main)


def _ret_kernel(qf_ref, kf_ref, vf_ref, cf_ref, sf_ref, qb_ref, kb_ref, vb_ref, cb_ref, sb_ref,
                of_ref, ob_ref, st_ref, *, tb):
    @pl.when(pl.program_id(0) == 0)
    def _():
        st_ref[...] = jnp.zeros_like(st_ref)

    row, col = _chunk_masks()
    n = tb // CHUNK
    scale = RET_DK ** -0.5
    t_idx = lax.broadcasted_iota(jnp.int32, (CHUNK, RET_DK), 0).astype(F32)
    log_gamma = [math.log1p(-2.0 ** (-5 - h)) for h in range(RET_HEADS)]

    def rope(x, cos, sin):
        return x * cos + pltpu.roll(x, RET_DK // 2, 1) * sin

    def decay_tables(h, rev):
        lgam = log_gamma[RET_HEADS - 1 - h] if rev else log_gamma[h]
        steps = (CHUNK - t_idx) if rev else (t_idx + 1.0)
        dist = ((col - row) if rev else (row - col)).astype(F32)
        return dict(intra=jnp.where(dist >= 0.0, jnp.exp(dist * lgam), 0.0),
                    q_fac=jnp.exp(steps * lgam), k_fac=jnp.exp((CHUNK - steps) * lgam),
                    st_fac=math.exp(CHUNK * lgam))

    tables = {(h, rev): decay_tables(h, rev) for h in range(RET_HEADS) for rev in (False, True)}

    def problem(q_ref, k_ref, v_ref, c_ref, s_ref, o_ref, rows, h, rev):
        cs = slice(h * RET_DK, (h + 1) * RET_DK)
        vs = slice(h * RET_DV, (h + 1) * RET_DV)
        cos, sin = c_ref[rows, :], s_ref[rows, :]
        return dict(q=rope(q_ref[rows, cs], cos, sin) * scale, k=rope(k_ref[rows, cs], cos, sin),
                    v=v_ref[rows, vs], slot=(RET_HEADS if rev else 0) + h, o_ref=o_ref, rows=rows, cols=vs,
                    **tables[(h, rev)])

    def body(c, carry):
        groups = []
        for u in range(RET_UNROLL):
            rf = pl.ds(pl.multiple_of((RET_UNROLL * c + u) * CHUNK, CHUNK), CHUNK)
            rb = pl.ds(pl.multiple_of((n - 1 - RET_UNROLL * c - u) * CHUNK, CHUNK), CHUNK)
            groups.append(
                [problem(qf_ref, kf_ref, vf_ref, cf_ref, sf_ref, of_ref, rf, h, False) for h in range(RET_HEADS)]
                + [problem(qb_ref, kb_ref, vb_ref, cb_ref, sb_ref, ob_ref, rb, h, True) for h in range(RET_HEADS)])
        for ps in groups:
            for p in ps:
                p["scores"] = _bdot_nt(p["q"], p["k"]) * p["intra"]
                p["update"] = _bdot_tn(p["v"], p["k"] * p["k_fac"])
        for ps in groups:
            for p in ps:
                p["o"] = _bdot(p["scores"], p["v"]) + _bdot_nt(p["q"] * p["q_fac"], st_ref[p["slot"]])
            for p in ps:
                st_ref[p["slot"]] = st_ref[p["slot"]] * p["st_fac"] + p["update"]
                p["o_ref"][p["rows"], p["cols"]] = p["o"]
        return carry

    assert n % RET_UNROLL == 0
    lax.fori_loop(0, n // RET_UNROLL, body, 0)


def _retention(h_main, cos_tab, sin_tab, *, tb=512):
    s = h_main.shape[0]
    tb = min(tb, s)
    nb = s // tb
    wk, wv = RET_HEADS * RET_DK, RET_HEADS * RET_DV
    fwd = lambda w, cb: pl.BlockSpec((tb, w), lambda i: (i, cb))
    bwd = lambda w, cb: pl.BlockSpec((tb, w), lambda i: (nb - 1 - i, cb))
    return pl.pallas_call(
        functools.partial(_ret_kernel, tb=tb), grid=(nb,),
        in_specs=[fwd(wk, 0), fwd(wk, 1), fwd(wv, 1), fwd(LANES, 0), fwd(LANES, 0),
                  bwd(wk, 0), bwd(wk, 1), bwd(wv, 1), bwd(LANES, 0), bwd(LANES, 0)],
        out_specs=[fwd(wv, 0), bwd(wv, 0)],
        out_shape=[jax.ShapeDtypeStruct((s, wv), F32)] * 2,
        scratch_shapes=[pltpu.VMEM((2 * RET_HEADS, RET_DV, RET_DK), F32)],
        compiler_params=_params(("arbitrary",), 40), name="retention_scan",
    )(h_main, h_main, h_main, cos_tab, sin_tab, h_main, h_main, h_main, cos_tab, sin_tab)


def _gated_head_norm(of_ref, ob_ref, gate_ref, nw_ref, rows, heads, hd):
    o = of_ref[rows, :] + ob_ref[rows, :]
    parts = []
    for h in range(heads):
        y = o[:, h * hd:(h + 1) * hd]
        parts.append(y * lax.rsqrt(jnp.mean(y * y, axis=-1, keepdims=True) + EPS))
    y = jnp.concatenate(parts, axis=-1)
    return y * nw_ref[...] * _silu(gate_ref[rows, :])


def _mixout_kernel(*refs, tm, groups):
    x_ref, w_ref = refs[0], refs[1]
    out_ref, lhs_ref = refs[-2], refs[-1]
    grefs = refs[2:-2]

    def body(r, carry):
        rows = pl.ds(pl.multiple_of(r * NORM_ROWS, NORM_ROWS), NORM_ROWS)
        pos, col = 0, 0
        for g in groups:
            if g is None:
                val, width = grefs[pos][rows, :], grefs[pos].shape[1]
                pos += 1
            else:
                heads, hd = g
                val, width = _gated_head_norm(*grefs[pos:pos + 4], rows, heads, hd), heads * hd
                pos += 4
            lhs_ref[rows, col:col + width] = val.astype(BF16)
            col += width
        return carry
    lax.fori_loop(0, tm // NORM_ROWS, body, 0)

    out_ref[...] = x_ref[...] + jnp.dot(lhs_ref[...], w_ref[...], preferred_element_type=F32)


def _mix_out(x, w_out, group_args, groups, *, tm=256):
    s, d = x.shape
    tm = min(tm, s)
    specs = [pl.BlockSpec((tm, d), lambda i: (i, 0)),
             pl.BlockSpec(w_out.shape, lambda i: (0, 0), pipeline_mode=pl.Buffered(1))]
    arrays = [x, w_out]
    for arr, cb, width, rowvec in group_args:
        arrays.append(arr)
        if rowvec:
            specs.append(pl.BlockSpec((1, width), lambda i: (0, 0)))
        else:
            specs.append(pl.BlockSpec((tm, width), lambda i, cb=cb: (i, cb)))
    return pl.pallas_call(
        functools.partial(_mixout_kernel, tm=tm, groups=groups), grid=(s // tm,),
        in_specs=specs,
        out_specs=pl.BlockSpec((tm, d), lambda i: (i, 0)),
        out_shape=jax.ShapeDtypeStruct((s, d), F32),
        scratch_shapes=[pltpu.VMEM((tm, w_out.shape[0]), BF16)],
        compiler_params=_params(("parallel",), 40), name="mix_out",
    )(*arrays)


def _rot_pair(pr, cos, sin):
    return pr * cos + pltpu.roll(pr, MLA_ROPE, 1) * sin


def _mla_proj_kernel(cq_ref, ckv_ref, kpe_ref, qnw_ref, kvnw_ref, wq_ref, wkv_ref, cos_ref, sin_ref,
                     q_ref, k_ref, v_ref, *, scale):
    cos, sin = cos_ref[...], sin_ref[...]
    cqn = _rms_rows(cq_ref[...], qnw_ref[...]).astype(BF16)
    ckvn = _rms_rows(ckv_ref[...], kvnw_ref[...]).astype(BF16)
    pe = _rot_pair(kpe_ref[...], cos, sin).astype(BF16)
    lane = lax.broadcasted_iota(jnp.int32, pe.shape, 1)
    ones_col = jnp.where(lane == 0, 1.0, 0.0).astype(BF16)
    for h in range(MLA_HEADS):
        cols = slice(h * MLA_QK_PAD, (h + 1) * MLA_QK_PAD)
        yq = jnp.dot(cqn, wq_ref[:, cols], preferred_element_type=F32)
        q_rope = _rot_pair(yq[:, MLA_NOPE:], cos, sin)
        q_ref[h] = (jnp.concatenate([yq[:, :MLA_NOPE], q_rope], axis=1) * scale).astype(BF16)
        ykv = jnp.dot(ckvn, wkv_ref[:, cols], preferred_element_type=F32)
        k_ref[h] = jnp.concatenate([ykv[:, :MLA_NOPE].astype(BF16), pe], axis=1)
        v_ref[h] = jnp.concatenate([ykv[:, MLA_NOPE:].astype(BF16), ones_col], axis=1)


def _mla_proj(h_main, h_side, q_norm_w, kv_norm_w, wq, wkv, cos_tab, sin_tab, *, tm=512):
    s = h_main.shape[0]
    tm = min(tm, s)
    cq_block = (HG_HEADS * (3 * HG_DK + 2 * HG_DV)) // MLA_Q_RANK
    scale = (MLA_NOPE + MLA_ROPE) ** -0.5 * math.log2(math.e)
    full = lambda a: pl.BlockSpec(a.shape, lambda i: (0, 0))
    tab = pl.BlockSpec((tm, LANES), lambda i: (i, 0))
    head_out = pl.BlockSpec((MLA_HEADS, tm, MLA_QK_PAD), lambda i: (0, i, 0))
    return pl.pallas_call(
        functools.partial(_mla_proj_kernel, scale=scale), grid=(s // tm,),
        in_specs=[pl.BlockSpec((tm, MLA_Q_RANK), lambda i: (i, cq_block)),
                  pl.BlockSpec((tm, MLA_KV_RANK), lambda i: (i, 0)),
                  pl.BlockSpec((tm, LANES), lambda i: (i, MLA_KV_RANK // LANES)),
                  full(q_norm_w), full(kv_norm_w), full(wq), full(wkv), tab, tab],
        out_specs=[head_out, head_out, head_out],
        out_shape=[jax.ShapeDtypeStruct((MLA_HEADS, s, MLA_QK_PAD), BF16)] * 3,
        compiler_params=_params(("parallel",), 40), name="mla_proj",
    )(h_main, h_side, h_side, q_norm_w, kv_norm_w, wq, wkv, cos_tab, sin_tab)


ATTN_SPLIT = 2
ATTN_ROWS = 32


def _attn_kernel(q_ref, k_ref, v_ref, *refs, tk, nk, n_cast):
    cast_in, o_ref, cast_out = refs[:n_cast], refs[n_cast], refs[n_cast + 1:2 * n_cast + 1]
    s_ref, p_ref, m_ref, alpha_ref, acc_ref = refs[2 * n_cast + 1:]
    tq = q_ref.shape[1]
    sub = tq // ATTN_SPLIT
    m_ref[...] = jnp.full(m_ref.shape, -jnp.inf, F32)
    acc_ref[...] = jnp.zeros(acc_ref.shape, F32)

    def softmax_rows(i):
        for rb in range(sub // ATTN_ROWS):
            loc = slice(rb * ATTN_ROWS, (rb + 1) * ATTN_ROWS)
            glob = slice(i * sub + rb * ATTN_ROWS, i * sub + (rb + 1) * ATTN_ROWS)
            sc = s_ref[i, loc, :]
            cols = [sc[:, c * LANES:(c + 1) * LANES] for c in range(tk // LANES)]
            col_max = functools.reduce(jnp.maximum, cols)
            m_old = m_ref[glob, :]
            m_new = jnp.maximum(m_old, jnp.max(col_max, axis=-1, keepdims=True))
            m_ref[glob, :] = m_new
            alpha_ref[glob, :] = jnp.exp2(m_old - m_new)
            p_ref[i, loc, :] = jnp.concatenate([jnp.exp2(c - m_new) for c in cols], axis=1).astype(BF16)

    def body(j, carry):
        rows = pl.ds(pl.multiple_of(j * tk, tk), tk)
        k, v = k_ref[0, rows, :], v_ref[0, rows, :]
        for i in range(ATTN_SPLIT):
            s_ref[i] = lax.dot_general(q_ref[0, i * sub:(i + 1) * sub, :], k, NT_DIMS, preferred_element_type=F32)
        for i in range(ATTN_SPLIT):
            softmax_rows(i)
            rs = slice(i * sub, (i + 1) * sub)
            alpha = alpha_ref[rs, :]
            acc_ref[rs, :] = (jnp.concatenate([alpha, alpha], axis=1) * acc_ref[rs, :]
                              + jnp.dot(p_ref[i], v, preferred_element_type=F32))
        for src, dst in zip(cast_in, cast_out):
            if src.shape[1] % (2 * SUBLANES * nk) == 0:
                piece = src.shape[1] // nk
                part = pl.ds(pl.multiple_of(j * piece, piece), piece)
                dst[0, part, :] = src[0, part, :].astype(BF16)
        return carry

    for src, dst in zip(cast_in, cast_out):
        if src.shape[1] % (2 * SUBLANES * nk) != 0:
            dst[...] = src[...].astype(BF16)

    lax.fori_loop(0, nk, body, 0)
    acc = acc_ref[...]
    o_ref[...] = (acc[:, :MLA_V] / acc[:, MLA_V:MLA_V + 1]).astype(BF16)


def _attention(q, k, v, to_cast, *, tq=1024, tk=2048):
    heads, s, _ = q.shape
    tq, tk = min(tq, s), min(tk, s)
    sub = tq // ATTN_SPLIT
    nq, nk = s // tq, s // tk
    steps = heads * nq
    cast_specs = []
    for a in to_cast:
        layers, rows, cols = a.shape
        block = next(b for b in range(-(-layers * rows // steps), rows + 1)
                     if rows % b == 0 and b % (2 * SUBLANES) == 0)
        per_layer, last = rows // block, layers * rows // block - 1

        def index(h, i, per_layer=per_layer, last=last):
            t = jnp.minimum(h * nq + i, last)
            return t // per_layer, t % per_layer, 0

        cast_specs.append(pl.BlockSpec((1, block, cols), index))
    outs = pl.pallas_call(
        functools.partial(_attn_kernel, tk=tk, nk=nk, n_cast=len(to_cast)), grid=(heads, nq),
        in_specs=[pl.BlockSpec((1, tq, MLA_QK_PAD), lambda h, i: (h, i, 0)),
                  pl.BlockSpec((1, s, MLA_QK_PAD), lambda h, i: (h, 0, 0)),
                  pl.BlockSpec((1, s, 2 * MLA_V), lambda h, i: (h, 0, 0))] + cast_specs,
        out_specs=[pl.BlockSpec((tq, MLA_V), lambda h, i: (i, h))] + cast_specs,
        out_shape=[jax.ShapeDtypeStruct((s, heads * MLA_V), BF16)]
                  + [jax.ShapeDtypeStruct(a.shape, BF16) for a in to_cast],
        scratch_shapes=[pltpu.VMEM((ATTN_SPLIT, sub, tk), F32), pltpu.VMEM((ATTN_SPLIT, sub, tk), BF16),
                        pltpu.VMEM((tq, LANES), F32), pltpu.VMEM((tq, LANES), F32),
                        pltpu.VMEM((tq, 2 * MLA_V), F32)],
        compiler_params=_params(("arbitrary", "arbitrary"), 56), name="mla_attention",
    )(q, k, v, *to_cast)
    return outs[0], outs[1:]


def _ffn_kernel(x_ref, nw_ref, wu_ref, wd_ref, fw_ref, o_ref, xn_ref, *, tm, nf, final_norm):
    f = pl.program_id(1)

    @pl.when(f == 0)
    def _():
        _norm_rows_to(x_ref, nw_ref, xn_ref, tm)
        o_ref[...] = x_ref[...]

    a = jnp.maximum(jnp.dot(xn_ref[...], wu_ref[0], preferred_element_type=F32), 0.0)
    o_ref[...] += jnp.dot((a * a).astype(BF16), wd_ref[0], preferred_element_type=F32)

    if final_norm:
        @pl.when(f == nf - 1)
        def _():
            def body(r, carry):
                rows = pl.ds(pl.multiple_of(r * NORM_ROWS, NORM_ROWS), NORM_ROWS)
                o_ref[rows, :] = _rms_rows(o_ref[rows, :], fw_ref[...])
                return carry
            lax.fori_loop(0, tm // NORM_ROWS, body, 0)


def _ffn(x, nw, w_up, w_down, final_w, *, layer, final_norm, tm=512, tf=2048):
    s, d = x.shape
    dff = w_up.shape[2]
    tm = min(tm, s)
    nf = dff // tf
    return pl.pallas_call(
        functools.partial(_ffn_kernel, tm=tm, nf=nf, final_norm=final_norm), grid=(s // tm, nf),
        in_specs=[pl.BlockSpec((tm, d), lambda i, f: (i, 0), pipeline_mode=pl.Buffered(1)),
                  pl.BlockSpec((1, d), lambda i, f: (0, 0)),
                  pl.BlockSpec((1, d, tf), lambda i, f: (layer, 0, f)),
                  pl.BlockSpec((1, tf, d), lambda i, f: (layer, f, 0)),
                  pl.BlockSpec((1, d), lambda i, f: (0, 0))],
        out_specs=pl.BlockSpec((tm, d), lambda i, f: (i, 0)),
        out_shape=jax.ShapeDtypeStruct((s, d), F32),
        scratch_shapes=[pltpu.VMEM((tm, d), BF16)],
        compiler_params=_params(("parallel", "arbitrary"), 60), name="ffn",
    )(x, nw, w_up, w_down, final_w)


def _gdn_prep_kernel(x_ref, xp_ref, xn_ref, cw_ref, o_ref, *, tm, nblk):
    i, sec = pl.program_id(0), pl.program_id(1)
    x = x_ref[...]
    prev = jnp.where(i > 0, xp_ref[...], 0.0)
    nxt = jnp.where(i < nblk - 1, xn_ref[...], 0.0)
    rows = lax.broadcasted_iota(jnp.int32, prev.shape, 0)
    half = CONV_WIDTH // 2

    def shifted(d):
        if d == 0:
            return x
        r = pltpu.roll(x, (-d) % tm, 0)
        if d < 0:
            edge = r[:SUBLANES]
            for t in range(-d):
                edge = jnp.where(rows == t, prev[SUBLANES + t + d:SUBLANES + t + d + 1], edge)
            return jnp.concatenate([edge, r[SUBLANES:]], axis=0)
        edge = r[tm - SUBLANES:]
        for t in range(SUBLANES - d, SUBLANES):
            edge = jnp.where(rows == t, nxt[t + d - SUBLANES:t + d - SUBLANES + 1], edge)
        return jnp.concatenate([r[:tm - SUBLANES], edge], axis=0)

    y = shifted(-half) * cw_ref[0:1]
    for j in range(1, CONV_WIDTH):
        y = y + shifted(j - half) * cw_ref[j:j + 1]
    y = _silu(y)
    parts = []
    for h in range(GDN_HEADS):
        a = y[:, h * GDN_DK:(h + 1) * GDN_DK]
        parts.append(a * lax.rsqrt(jnp.sum(a * a, axis=-1, keepdims=True) + EPS))
    nrm = jnp.concatenate(parts, axis=-1) * jnp.where(sec == 0, GDN_DK ** -0.5, 1.0)
    o_ref[...] = jnp.where(sec == 2, y, nrm)


def _gdn_prep(h_main, conv_w, *, tm=256):
    s = h_main.shape[0]
    tm = min(tm, s)
    nblk = s // tm
    w = GDN_HEADS * GDN_DK
    base = (2 * RET_HEADS * RET_DK + 2 * RET_HEADS * RET_DV) // w
    per = tm // SUBLANES
    return pl.pallas_call(
        functools.partial(_gdn_prep_kernel, tm=tm, nblk=nblk), grid=(nblk, 3),
        in_specs=[pl.BlockSpec((tm, w), lambda i, c: (i, base + c)),
                  pl.BlockSpec((SUBLANES, w), lambda i, c: (jnp.maximum(i * per - 1, 0), base + c)),
                  pl.BlockSpec((SUBLANES, w), lambda i, c: (jnp.minimum((i + 1) * per, s // SUBLANES - 1), base + c)),
                  pl.BlockSpec((CONV_WIDTH, w), lambda i, c: (0, c))],
        out_specs=pl.BlockSpec((tm, w), lambda i, c: (i, c)),
        out_shape=jax.ShapeDtypeStruct((s, 3 * w), F32),
        compiler_params=_params(("parallel", "arbitrary"), 32), name="gdn_prep",
    )(h_main, h_main, h_main, conv_w)


def _softplus(x):
    return jnp.maximum(x, 0.0) + jnp.log1p(jnp.exp(-jnp.abs(x)))


def _pair_lhs(x):
    hi = x.astype(BF16)
    lo = (x - hi.astype(F32)).astype(BF16)
    return jnp.concatenate([hi, lo, hi], axis=1)


def _pair_rhs(y, low_lanes):
    hi = y.astype(BF16)
    lo = (y - hi.astype(F32)).astype(BF16)
    zero = jnp.zeros_like(hi)

    def bd(a):
        return jnp.concatenate([jnp.where(low_lanes, a, zero), jnp.where(low_lanes, zero, a)], axis=0)

    bd_hi = bd(hi)
    return jnp.concatenate([bd_hi, bd_hi, bd(lo)], axis=0)


def _pair_lhs2(x):
    hi = x.astype(BF16)
    lo = (x - hi.astype(F32)).astype(BF16)
    return jnp.concatenate([hi, lo], axis=1)


def _pair_rhs1(y, low_lanes):
    hi = y.astype(BF16)
    zero = jnp.zeros_like(hi)
    bd = jnp.concatenate([jnp.where(low_lanes, hi, zero), jnp.where(low_lanes, zero, hi)], axis=0)
    return jnp.concatenate([bd, bd], axis=0)


def _pair_rows(x, low_lanes):
    return jnp.concatenate([jnp.where(low_lanes, x, 0.0), jnp.where(low_lanes, 0.0, x)], axis=0)


def _gdn_kernel(alog_ref, dt_ref, qf_ref, kf_ref, vf_ref, gf_ref, qb_ref, kb_ref, vb_ref, gb_ref,
                of_ref, ob_ref, st_ref, *, tb):
    @pl.when(pl.program_id(0) == 0)
    def _():
        st_ref[...] = jnp.zeros_like(st_ref)

    row = lax.broadcasted_iota(jnp.int32, (CHUNK, LANES), 0)
    lane = lax.broadcasted_iota(jnp.int32, (CHUNK, LANES), 1)
    col = lane & (CHUNK - 1)
    low_lanes = lane < CHUNK
    low_row = low_lanes[0:1]
    causal, anti = row >= col, row <= col
    strict_c, strict_a = row > col, row < col
    tri3_f, tri3_b = _tri3(False), _tri3(True)
    eye = (row == col).astype(F32)
    same16 = (row >> 4) == (col >> 4)
    same32 = (row >> 5) == (col >> 5)
    in32 = jnp.logical_and(same32, jnp.logical_not(same16))
    n = tb // CHUNK
    neg_a = -jnp.exp(alog_ref[...])
    dt = dt_ref[...]

    def pairs(q_ref, k_ref, v_ref, g_ref, o_ref, rows, rev):
        raw = g_ref[rows, :]
        la = neg_a * _softplus(raw + dt)
        beta = jax.nn.sigmoid(raw)
        g = _sdot(tri3_b if rev else tri3_f, _rhs_split3(la))
        g_rows = jnp.concatenate([g, g], axis=0).T
        la0 = GDN_HEADS if rev else 0
        b0 = (3 if rev else 2) * GDN_HEADS
        last = 0 if rev else CHUNK - 1
        out = []
        for j in range(GDN_HEADS // 2):
            hs = (2 * j, 2 * j + 1)
            g_cols = [jnp.broadcast_to(g[:, la0 + h:la0 + h + 1], (CHUNK, LANES)) for h in hs]
            b_cols = [jnp.broadcast_to(beta[:, b0 + h:b0 + h + 1], (CHUNK, LANES)) for h in hs]
            g_col = jnp.where(low_lanes, g_cols[0], g_cols[1])
            g_row = jnp.where(low_row, g_rows[la0 + hs[0]:la0 + hs[0] + 1], g_rows[la0 + hs[1]:la0 + hs[1] + 1])
            out.append(dict(
                q_ref=q_ref, k_ref=k_ref, v_ref=v_ref, o_ref=o_ref, rows=rows,
                cs=[slice(h * GDN_DK, (h + 1) * GDN_DK) for h in hs],
                slots=[(GDN_HEADS if rev else 0) + h for h in hs],
                strict=strict_a if rev else strict_c,
                decay=jnp.exp(jnp.where(anti if rev else causal, g_col - g_row, -jnp.inf)),
                g_cols=g_cols, b_cols=b_cols,
                g_last=[g[last:last + 1, la0 + h:la0 + h + 1] for h in hs]))
        return out

    def body(c, carry):
        rf = pl.ds(pl.multiple_of(c * CHUNK, CHUNK), CHUNK)
        rb = pl.ds(pl.multiple_of((n - 1 - c) * CHUNK, CHUNK), CHUNK)
        ps = (pairs(qf_ref, kf_ref, vf_ref, gf_ref, of_ref, rf, False)
              + pairs(qb_ref, kb_ref, vb_ref, gb_ref, ob_ref, rb, True))
        for p in ps:
            ks = [p["k_ref"][p["rows"], cs] for cs in p["cs"]]
            qs = [p["q_ref"][p["rows"], cs] for cs in p["cs"]]
            p["kb"] = [k * b for k, b in zip(ks, p["b_cols"])]
            lhs = jnp.concatenate([jnp.concatenate([kb, q], axis=0) for kb, q in zip(p["kb"], qs)], axis=1)
            zero = jnp.zeros_like(ks[0])
            k_bd = jnp.concatenate([jnp.concatenate([ks[0], zero], axis=1),
                                    jnp.concatenate([zero, ks[1]], axis=1)], axis=0)
            kq = _bdot_nt(lhs, k_bd)
            a = jnp.where(p["strict"], kq[:CHUNK] * p["decay"], 0.0)
            p["qk"] = kq[CHUNK:] * p["decay"]
            n1 = -jnp.where(same16, a, 0.0)
            p["n_l"], p["n_r"], p["t"] = _pair_lhs2(n1), _pair_rhs1(n1, low_lanes), eye + n1
            p["e32"] = _pair_rhs1(jnp.where(in32, a, 0.0), low_lanes)
            p["e64"] = _pair_rhs1(jnp.where(same32, 0.0, a), low_lanes)
        for step in range(3):
            for p in ps:
                nn = _sdot(p["n_l"], p["n_r"])
                p["n_r"] = _pair_rhs(nn, low_lanes)
                if step < 2:
                    p["n_l"] = _pair_lhs(nn)
            for p in ps:
                p["t"] = p["t"] + _sdot(_pair_lhs(p["t"]), p["n_r"])
        for e in ("e32", "e64"):
            for p in ps:
                p["x"] = _sdot(_pair_lhs2(p["t"]), p[e])
            for p in ps:
                p["t"] = p["t"] - _sdot(_pair_lhs(p["x"]), _pair_rhs(p["t"], low_lanes))
        for p in ps:
            rhs = []
            for cs, kb, b, g in zip(p["cs"], p["kb"], p["b_cols"], p["g_cols"]):
                rhs.append(jnp.concatenate([p["v_ref"][p["rows"], cs] * b, kb * jnp.exp(g)], axis=1))
            p["sol"] = _sdot(_pair_lhs(_pair_rows(p["t"], low_lanes)), _rhs_split(jnp.concatenate(rhs, axis=0)))
        for p in ps:
            p["v_new"], p["o"] = [], []
            for i, (cs, slot, g) in enumerate(zip(p["cs"], p["slots"], p["g_cols"])):
                sol = p["sol"][i * CHUNK:(i + 1) * CHUNK]
                q = p["q_ref"][p["rows"], cs]
                ws = _bdot(jnp.concatenate([sol[:, GDN_DV:], q * jnp.exp(g)], axis=0), st_ref[slot])
                p["v_new"].append(sol[:, :GDN_DV] - ws[:CHUNK])
                p["o"].append(ws[CHUNK:])
        for p in ps:
            o_intra = _bdot(_pair_rows(p["qk"], low_lanes), jnp.concatenate(p["v_new"], axis=0))
            for i, (cs, slot, g, g_last) in enumerate(zip(p["cs"], p["slots"], p["g_cols"], p["g_last"])):
                k = p["k_ref"][p["rows"], cs]
                p["o_ref"][p["rows"], cs] = p["o"][i] + o_intra[i * CHUNK:(i + 1) * CHUNK]
                st_ref[slot] = (st_ref[slot] * jnp.exp(g_last)
                                + _bdot_tn(k * jnp.exp(g_last - g), p["v_new"][i]))
        return carry

    lax.fori_loop(0, n, body, 0)


def _gdn(qkv, gates, alog_row, dt_row, *, tb=512):
    s = qkv.shape[0]
    tb = min(tb, s)
    nb = s // tb
    w = GDN_HEADS * GDN_DK
    fwd = lambda width, cb: pl.BlockSpec((tb, width), lambda i: (i, cb))
    bwd = lambda width, cb: pl.BlockSpec((tb, width), lambda i: (nb - 1 - i, cb))
    row = pl.BlockSpec((1, LANES), lambda i: (0, 0))
    return pl.pallas_call(
        functools.partial(_gdn_kernel, tb=tb), grid=(nb,),
        in_specs=[row, row,
                  fwd(w, 0), fwd(w, 1), fwd(w, 2), fwd(LANES, 0),
                  bwd(w, 0), bwd(w, 1), bwd(w, 2), bwd(LANES, 0)],
        out_specs=[fwd(w, 0), bwd(w, 0)],
        out_shape=[jax.ShapeDtypeStruct((s, w), F32)] * 2,
        scratch_shapes=[pltpu.VMEM((2 * GDN_HEADS, GDN_DK, GDN_DV), F32)],
        compiler_params=_params(("arbitrary",), 40), name="gdn_scan",
    )(alog_row, dt_row, qkv, qkv, qkv, gates, qkv, qkv, qkv, gates)


def _rot_cols(w_pe):
    half = w_pe.shape[-1] // 2
    return jnp.concatenate([-w_pe[..., half:], w_pe[..., :half]], axis=-1)


def _pad_lanes(v, fill=0.0):
    v = v.reshape(1, -1).astype(F32)
    return jnp.pad(v, ((0, 0), (0, LANES - v.shape[1])), constant_values=fill)


def kernel(x, positions, norm_mix_w, norm_ffn_w, final_norm_w, hg_lb_logits, even_w_in, hg_norm_w,
           mla_q_norm_w, mla_w_q_b, mla_kv_norm_w, mla_w_kv_b, even_w_out, odd_w_in, ret_norm_w,
           gdn_conv_w, gdn_a_log, gdn_dt_bias, gdn_norm_w, odd_w_out, ffn_w_up, ffn_w_down):
    b, s, d = x.shape
    assert b == 1 and s % CHUNK == 0
    xs = x.reshape(s, d)
    row = lambda v: v.reshape(1, -1).astype(F32)

    pos_col = positions.reshape(s, 1)
    inv_ret = ROPE_THETA ** (-jnp.arange(RET_DK // 2, dtype=F32) / (RET_DK // 2))
    inv_mla = ROPE_THETA ** (-jnp.arange(MLA_ROPE // 2, dtype=F32) / (MLA_ROPE // 2))
    inv_row = row(jnp.concatenate([inv_ret, inv_mla, jnp.zeros((LANES - RET_DK // 2 - MLA_ROPE // 2,), F32)]))
    cos_ret, sin_ret, cos_mla, sin_mla = _rope_tables(pos_col, inv_row)

    n_main = HG_HEADS * (3 * HG_DK + 2 * HG_DV) + MLA_Q_RANK
    wt_in = even_w_in[0].T.astype(BF16)
    wt_kpe = wt_in[n_main + MLA_KV_RANK:]
    wt_side = jnp.concatenate([wt_in[n_main:], _rot_cols(wt_kpe.T).T], axis=0)
    h_main, h_side = _norm_matmul(xs, row(norm_mix_w[0]), wt_in, n_main, None, wt_side, tn=512)

    o_f, o_b = _hgrn2(h_main, hg_lb_logits.astype(F32), layer=0)

    wq = mla_w_q_b[0].reshape(MLA_Q_RANK, MLA_HEADS, MLA_NOPE + MLA_ROPE)
    wq_pe = wq[..., MLA_NOPE:]
    wq = jnp.concatenate([wq[..., :MLA_NOPE], wq_pe, _rot_cols(wq_pe)], axis=-1)
    wq = wq.reshape(MLA_Q_RANK, MLA_HEADS * MLA_QK_PAD).astype(BF16)
    q, k, v = _mla_proj(h_main, h_side, row(mla_q_norm_w[0]), row(mla_kv_norm_w[0]), wq,
                        mla_w_kv_b[0].astype(BF16), cos_mla, sin_mla)
    o_attn, (w_up, w_down, w_out_even, w_out_odd, wt_in_odd) = _attention(
        q, k, v, [ffn_w_up, ffn_w_down, even_w_out, odd_w_out, jnp.swapaxes(odd_w_in, 1, 2)])

    wa = HG_HEADS * HG_DV
    xs = _mix_out(xs, w_out_even[0],
                  [(o_f, 0, wa, False), (o_b, 0, wa, False), (h_main, 4, wa, False), (row(hg_norm_w[0]), 0, wa, True),
                   (o_attn, 0, MLA_HEADS * MLA_V, False)],
                  ((HG_HEADS, HG_DV), None))
    xs = _ffn(xs, row(norm_ffn_w[0]), w_up, w_down, row(final_norm_w), layer=0, final_norm=False)

    n_ret = 2 * RET_HEADS * RET_DK + 2 * RET_HEADS * RET_DV
    n_qkv = GDN_HEADS * (2 * GDN_DK + GDN_DV)
    n_gate = 4 * GDN_HEADS
    wt_in = wt_in_odd[0]
    wt_gate = wt_in[n_ret + n_qkv + n_gate:]
    wt_side = jnp.pad(wt_in[n_ret + n_qkv:n_ret + n_qkv + n_gate], ((0, LANES - n_gate), (0, 0)))
    h_main, h_gates = _norm_matmul(xs, row(norm_mix_w[1]), wt_in, n_ret + n_qkv, wt_gate, wt_side, tn=1024)

    r_f, r_b = _retention(h_main, cos_ret, sin_ret)
    qkv = _gdn_prep(h_main, gdn_conv_w[0].astype(F32))
    g_f, g_b = _gdn(qkv, h_gates, _pad_lanes(gdn_a_log[0]), _pad_lanes(gdn_dt_bias[0]))

    wr, wg = RET_HEADS * RET_DV, GDN_HEADS * GDN_DV
    xs = _mix_out(xs, w_out_odd[0],
                  [(r_f, 0, wr, False), (r_b, 0, wr, False), (h_main, 2, wr, False), (row(ret_norm_w[0]), 0, wr, True),
                   (g_f, 0, wg, False), (g_b, 0, wg, False), (h_main, 6, wg, False), (row(gdn_norm_w[0]), 0, wg, True)],
                  ((RET_HEADS, RET_DV), (GDN_HEADS, GDN_DV)))
    xs = _ffn(xs, row(norm_ffn_w[1]), w_up, w_down, row(final_norm_w), layer=1, final_norm=True)
    return xs.reshape(b, s, d)
```
